```python
import math
import jax, jax.numpy as jnp
from jax import lax
import numpy as np

D_MODEL = 1024
BATCH = 8
SEQ = 4096
DEPTH = 2
DEC_BATCH = 128
DEC_SEQ = 8
PAST_LEN = 16384
PAGE_SIZE = 128

N_MIXERS = 2
N_SSM_LAYERS = (DEPTH + 1) // 2
N_SWA_LAYERS = DEPTH // 2

SSM_EXPAND = 2
SSM_D_INNER = SSM_EXPAND * D_MODEL
SSM_HEAD_DIM = 64
SSM_HEADS = SSM_D_INNER // SSM_HEAD_DIM
SSM_GROUPS = 4
SSM_HPG = SSM_HEADS // SSM_GROUPS
SSM_STATE = 128
SSM_CONV = 4
SSM_CHUNK = 128
SSM_CONV_DIM = SSM_D_INNER + 2 * SSM_GROUPS * SSM_STATE
SSM_IN_DIM = SSM_D_INNER + SSM_CONV_DIM + SSM_HEADS

ATTN_HEAD_DIM = 64
ATTN_HEADS = D_MODEL // ATTN_HEAD_DIM
ATTN_KV_HEADS = 4
ATTN_REP = ATTN_HEADS // ATTN_KV_HEADS
WINDOW = 128
SWA_BLOCK = WINDOW
QKV_DIM = (ATTN_HEADS + 2 * ATTN_KV_HEADS) * ATTN_HEAD_DIM

REL_BUCKETS = 32
REL_MAX_DIST = WINDOW

D_FF = 2816
FFN_RES = 0.5
N_SUB = 3
RMS_EPS = 1e-6

kernel_name = 'hybrid_ssd_swa_macaron_step'


def rms_norm(x, g):
    xf = x.astype(jnp.float32)
    y = xf * lax.rsqrt(jnp.mean(xf * xf, axis=-1, keepdims=True) + RMS_EPS)
    return (y * g.astype(jnp.float32)).astype(x.dtype)


def modulated_norm(x, g, m):
    return rms_norm(x, g) * (1 + m[:, :, 1]) + m[:, :, 0]


def swiglu(h, w_in, w_out):
    g, u = jnp.split(h @ w_in, 2, axis=-1)
    return (jax.nn.silu(g) * u) @ w_out


def t5_bucket(dist):
    exact = REL_BUCKETS // 2
    d = jnp.maximum(dist, 0)
    df = jnp.maximum(d, 1).astype(jnp.float32)
    large = exact + (jnp.log(df / exact) / math.log(REL_MAX_DIST / exact) * (REL_BUCKETS - exact)).astype(jnp.int32)
    large = jnp.minimum(large, REL_BUCKETS - 1)
    return jnp.where(d < exact, d, large)


def causal_dwconv(xpad, w, b, l):
    return sum(xpad[:, k:k + l] * w[k] for k in range(SSM_CONV)) + b


def ssd_scan(xdt, dA, Bm, Cm, h0):
    bsz, l = xdt.shape[:2]
    q = min(SSM_CHUNK, l)
    nc = -(-l // q)
    pad = nc * q - l
    if pad:
        padw = lambda a: jnp.pad(a, [(0, 0), (0, pad)] + [(0, 0)] * (a.ndim - 2))
        xdt, dA, Bm, Cm = padw(xdt), padw(dA), padw(Bm), padw(Cm)
    xc = xdt.reshape(bsz, nc, q, SSM_GROUPS, SSM_HPG, SSM_HEAD_DIM)
    Bc = Bm.reshape(bsz, nc, q, SSM_GROUPS, SSM_STATE)
    Cc = Cm.reshape(bsz, nc, q, SSM_GROUPS, SSM_STATE)
    a_cs = jnp.cumsum(dA.reshape(bsz, nc, q, SSM_GROUPS, SSM_HPG), axis=2)
    causal = jnp.tril(jnp.ones((q, q), bool))[None, None, :, :, None, None]
    seg = a_cs[:, :, :, None] - a_cs[:, :, None]
    decay = jnp.exp(jnp.where(causal, seg, -jnp.inf))
    cb = jnp.einsum('bclgn,bcsgn->bclsg', Cc, Bc)
    w = (decay * cb[..., None]).astype(xc.dtype)
    y_diag = jnp.einsum('bclsgr,bcsgrp->bclgrp', w, xc)
    decay_to_end = jnp.exp(a_cs[:, :, -1:] - a_cs)
    chunk_states = jnp.einsum('bclgn,bclgrp->bcgrpn', Bc, xc * decay_to_end[..., None])
    chunk_decay = jnp.exp(a_cs[:, :, -1])

    def step(hc, inp):
        s, a = inp
        return hc * a[..., None, None] + s, hc

    h_T, h_in = lax.scan(step, h0.astype(jnp.float32),
                         (jnp.moveaxis(chunk_states, 1, 0).astype(jnp.float32), jnp.moveaxis(chunk_decay, 1, 0)))
    y_off = jnp.einsum('bclgn,cbgrpn->bclgrp', Cc, h_in) * jnp.exp(a_cs)[..., None]
    y = (y_diag + y_off).reshape(bsz, nc * q, SSM_GROUPS, SSM_HPG, SSM_HEAD_DIM)[:, :l]
    return y.astype(xdt.dtype), h_T


def ssd_mixer(h, conv_buf, h0, in_w, conv_w, conv_b, dt_bias, a_log, d_skip, norm_w, out_w):
    bsz, l, _ = h.shape
    zxbcdt = h @ in_w
    z = zxbcdt[..., :SSM_D_INNER]
    xbc = zxbcdt[..., SSM_D_INNER:SSM_D_INNER + SSM_CONV_DIM]
    dt = zxbcdt[..., SSM_D_INNER + SSM_CONV_DIM:]
    xpad = jnp.concatenate([conv_buf.astype(xbc.dtype), xbc], axis=1)
    new_conv = xpad[:, -(SSM_CONV - 1):]
    xbc = jax.nn.silu(causal_dwconv(xpad, conv_w, conv_b, l))
    gn = SSM_GROUPS * SSM_STATE
    xs = xbc[..., :SSM_D_INNER].reshape(bsz, l, SSM_GROUPS, SSM_HPG, SSM_HEAD_DIM)
    Bm = xbc[..., SSM_D_INNER:SSM_D_INNER + gn].reshape(bsz, l, SSM_GROUPS, SSM_STATE)
    Cm = xbc[..., SSM_D_INNER + gn:].reshape(bsz, l, SSM_GROUPS, SSM_STATE)
    dt = jax.nn.softplus((dt + dt_bias).astype(jnp.float32)).reshape(bsz, l, SSM_GROUPS, SSM_HPG)
    A = -jnp.exp(a_log.astype(jnp.float32)).reshape(SSM_GROUPS, SSM_HPG)
    xdt = xs * dt[..., None].astype(xs.dtype)
    y, h_T = ssd_scan(xdt, dt * A, Bm, Cm, h0.reshape(bsz, SSM_GROUPS, SSM_HPG, SSM_HEAD_DIM, SSM_STATE))
    y = y + xs * d_skip.reshape(SSM_GROUPS, SSM_HPG, 1)
    yg = (y.reshape(bsz, l, SSM_D_INNER) * jax.nn.silu(z)).astype(jnp.float32).reshape(bsz, l, SSM_GROUPS, -1)
    yg = yg * lax.rsqrt(jnp.mean(yg * yg, axis=-1, keepdims=True) + RMS_EPS)
    yn = (yg.reshape(bsz, l, SSM_D_INNER) * norm_w.astype(jnp.float32)).astype(h.dtype)
    new_h = h_T.reshape(bsz, SSM_HEADS, SSM_HEAD_DIM, SSM_STATE).astype(h.dtype)
    return yn @ out_w, new_conv, new_h


def window_attention(q, k, v, q_pos, k_pos, sinks, rel_bias):
    logits = jnp.einsum('...qhrd,...khd->...hrqk', q, k).astype(jnp.float32) * (ATTN_HEAD_DIM ** -0.5)
    dist = q_pos[..., :, None] - k_pos[..., None, :]
    valid = (dist >= 0) & (dist <= WINDOW) & (k_pos[..., None, :] >= 0)
    bias = rel_bias[t5_bucket(dist)].astype(jnp.float32)
    bias = jnp.moveaxis(bias.reshape(bias.shape[:-1] + (ATTN_KV_HEADS, ATTN_REP)), (-2, -1), (-4, -3))
    logits = jnp.where(valid[..., None, None, :, :], logits + bias, -jnp.inf)
    s = sinks.astype(jnp.float32).reshape(ATTN_KV_HEADS, ATTN_REP, 1, 1)
    m = jnp.maximum(jnp.max(logits, axis=-1, keepdims=True), s)
    e = jnp.exp(logits - m)
    p = e / (jnp.sum(e, axis=-1, keepdims=True) + jnp.exp(s - m))
    return jnp.einsum('...hrqk,...khd->...qhrd', p.astype(v.dtype), v)


def swa_mixer(h, k_buf, v_buf, start, qkv_w, qkv_b, sinks, o_w, o_b, rel_bias):
    bsz, l, _ = h.shape
    qkv = h @ qkv_w + qkv_b
    nq = ATTN_HEADS * ATTN_HEAD_DIM
    nkv = ATTN_KV_HEADS * ATTN_HEAD_DIM
    q = qkv[..., :nq].reshape(bsz, l, ATTN_KV_HEADS, ATTN_REP, ATTN_HEAD_DIM)
    k = qkv[..., nq:nq + nkv].reshape(bsz, l, ATTN_KV_HEADS, ATTN_HEAD_DIM)
    v = qkv[..., nq + nkv:].reshape(bsz, l, ATTN_KV_HEADS, ATTN_HEAD_DIM)
    if k_buf is None:
        nb = l // SWA_BLOCK
        qb = q.reshape(bsz, nb, SWA_BLOCK, ATTN_KV_HEADS, ATTN_REP, ATTN_HEAD_DIM)
        kb = k.reshape(bsz, nb, SWA_BLOCK, ATTN_KV_HEADS, ATTN_HEAD_DIM)
        vb = v.reshape(bsz, nb, SWA_BLOCK, ATTN_KV_HEADS, ATTN_HEAD_DIM)
        zk = jnp.zeros_like(kb[:, :1])
        kk = jnp.concatenate([jnp.concatenate([zk, kb[:, :-1]], axis=1), kb], axis=2)
        vv = jnp.concatenate([jnp.concatenate([zk, vb[:, :-1]], axis=1), vb], axis=2)
        starts = start + jnp.arange(nb)[:, None] * SWA_BLOCK
        q_pos = starts + jnp.arange(SWA_BLOCK)
        k_pos = starts - SWA_BLOCK + jnp.arange(2 * SWA_BLOCK)
        out = window_attention(qb, kk, vv, q_pos, k_pos, sinks, rel_bias)
        nbuf = min(WINDOW, l)
        new_k, new_v = k[:, -nbuf:], v[:, -nbuf:]
    else:
        nbuf = k_buf.shape[1]
        kk = jnp.concatenate([k_buf.astype(k.dtype), k], axis=1)
        vv = jnp.concatenate([v_buf.astype(v.dtype), v], axis=1)
        q_pos = start + jnp.arange(l)
        k_pos = start - nbuf + jnp.arange(nbuf + l)
        out = window_attention(q, kk, vv, q_pos, k_pos, sinks, rel_bias)
        new_k, new_v = kk[:, -nbuf:], vv[:, -nbuf:]
    out = out.reshape(bsz, l, nq) @ o_w + o_b
    return out, new_k, new_v


def trunk(x, c, ssm_h0s, conv_bufs, k_bufs, v_bufs, start,
          ada_w, ada_b, norm_pre, norm_post, ffn_w_in, ffn_w_out,
          ssm_in_w, ssm_conv_w, ssm_conv_b, ssm_dt_bias, ssm_a_log, ssm_d, ssm_norm_w, ssm_out_w,
          attn_qkv_w, attn_qkv_b, attn_sinks, attn_o_w, attn_o_b, rel_bias):
    bsz = x.shape[0]
    new_ssm, new_conv, new_k, new_v = [], [], [], []
    cs = jax.nn.silu(c)
    for i in range(DEPTH):
        mod = (cs @ ada_w[i] + ada_b[i]).reshape(bsz, 1, N_SUB, 3, D_MODEL)
        hin = modulated_norm(x, norm_pre[i, 0], mod[:, :, 0])
        f = swiglu(hin, ffn_w_in[i, 0], ffn_w_out[i, 0])
        x = x + FFN_RES * mod[:, :, 0, 2] * rms_norm(f, norm_post[i, 0])
        hin = modulated_norm(x, norm_pre[i, 1], mod[:, :, 1])
        j = i // N_MIXERS
        if i % N_MIXERS == 0:
            o, conv_j, ssm_j = ssd_mixer(hin, conv_bufs[j], ssm_h0s[j], ssm_in_w[j], ssm_conv_w[j], ssm_conv_b[j],
                                         ssm_dt_bias[j], ssm_a_log[j], ssm_d[j], ssm_norm_w[j], ssm_out_w[j])
            new_conv.append(conv_j)
            new_ssm.append(ssm_j)
        else:
            kb = None if k_bufs is None else k_bufs[j]
            vb = None if v_bufs is None else v_bufs[j]
            o, k_j, v_j = swa_mixer(hin, kb, vb, start, attn_qkv_w[j], attn_qkv_b[j], attn_sinks[j],
                                    attn_o_w[j], attn_o_b[j], rel_bias)
            new_k.append(k_j)
            new_v.append(v_j)
        x = x + mod[:, :, 1, 2] * rms_norm(o, norm_post[i, 1])
        hin = modulated_norm(x, norm_pre[i, 2], mod[:, :, 2])
        f = swiglu(hin, ffn_w_in[i, 1], ffn_w_out[i, 1])
        x = x + FFN_RES * mod[:, :, 2, 2] * rms_norm(f, norm_post[i, 2])
    return x, jnp.stack(new_ssm), jnp.stack(new_conv), jnp.stack(new_k), jnp.stack(new_v)


def setup_inputs(seed: int = 0) -> dict:
    key = jax.random.key(seed)
    ks = jax.random.split(key, 32)
    nrm = lambda k, shape, s: jax.random.normal(k, shape, jnp.float32) * s
    nbuf = min(WINDOW, PAST_LEN)
    dt0 = jnp.exp(jax.random.uniform(ks[14], (N_SSM_LAYERS, SSM_HEADS), jnp.float32, math.log(1e-3), math.log(1e-1)))
    return {
        'x_prompt': nrm(ks[0], (BATCH, SEQ, D_MODEL), 1.0),
        'x_sample': nrm(ks[1], (DEC_BATCH, DEC_SEQ, D_MODEL), 1.0),
        'state_ssm': nrm(ks[2], (N_SSM_LAYERS, DEC_BATCH, SSM_HEADS, SSM_HEAD_DIM, SSM_STATE), 0.1),
        'state_conv': nrm(ks[3], (N_SSM_LAYERS, DEC_BATCH, SSM_CONV - 1, SSM_CONV_DIM), 1.0),
        'cache_k': nrm(ks[4], (N_SWA_LAYERS, DEC_BATCH, nbuf, ATTN_KV_HEADS, ATTN_HEAD_DIM), 1.0),
        'cache_v': nrm(ks[5], (N_SWA_LAYERS, DEC_BATCH, nbuf, ATTN_KV_HEADS, ATTN_HEAD_DIM), 1.0),
        'c_prompt': nrm(ks[6], (BATCH, D_MODEL), 1.0),
        'c_sample': nrm(ks[7], (DEC_BATCH, D_MODEL), 1.0),
        'ada_w': nrm(ks[8], (DEPTH, D_MODEL, N_SUB * 3 * D_MODEL), 0.5 * D_MODEL ** -0.5),
        'ada_b': nrm(ks[9], (DEPTH, N_SUB * 3 * D_MODEL), 0.02),
        'norm_pre': 1.0 + nrm(ks[10], (DEPTH, N_SUB, D_MODEL), 0.05),
        'norm_post': 1.0 + nrm(ks[11], (DEPTH, N_SUB, D_MODEL), 0.05),
        'ffn_w_in': nrm(ks[12], (DEPTH, 2, D_MODEL, 2 * D_FF), D_MODEL ** -0.5),
        'ffn_w_out': nrm(ks[13], (DEPTH, 2, D_FF, D_MODEL), D_FF ** -0.5),
        'ssm_in_w': nrm(ks[15], (N_SSM_LAYERS, D_MODEL, SSM_IN_DIM), D_MODEL ** -0.5),
        'ssm_conv_w': nrm(ks[16], (N_SSM_LAYERS, SSM_CONV, SSM_CONV_DIM), SSM_CONV ** -0.5),
        'ssm_conv_b': nrm(ks[17], (N_SSM_LAYERS, SSM_CONV_DIM), 0.02),
        'ssm_dt_bias': dt0 + jnp.log(-jnp.expm1(-dt0)),
        'ssm_a_log': jnp.log(jax.random.uniform(ks[18], (N_SSM_LAYERS, SSM_HEADS), jnp.float32, 1.0, 16.0)),
        'ssm_d': 1.0 + nrm(ks[19], (N_SSM_LAYERS, SSM_HEADS), 0.1),
        'ssm_norm_w': 1.0 + nrm(ks[20], (N_SSM_LAYERS, SSM_D_INNER), 0.05),
        'ssm_out_w': nrm(ks[21], (N_SSM_LAYERS, SSM_D_INNER, D_MODEL), SSM_D_INNER ** -0.5),
        'attn_qkv_w': nrm(ks[22], (N_SWA_LAYERS, D_MODEL, QKV_DIM), D_MODEL ** -0.5),
        'attn_qkv_b': nrm(ks[23], (N_SWA_LAYERS, QKV_DIM), 0.02),
        'attn_sinks': nrm(ks[24], (N_SWA_LAYERS, ATTN_HEADS), 1.0),
        'attn_o_w': nrm(ks[25], (N_SWA_LAYERS, ATTN_HEADS * ATTN_HEAD_DIM, D_MODEL), (ATTN_HEADS * ATTN_HEAD_DIM) ** -0.5),
        'attn_o_b': nrm(ks[26], (N_SWA_LAYERS, D_MODEL), 0.02),
        'rel_bias': nrm(ks[27], (REL_BUCKETS, ATTN_HEADS), 0.5),
    }


def reference(x_prompt, x_sample, state_ssm, state_conv, cache_k, cache_v, c_prompt, c_sample,
              ada_w, ada_b, norm_pre, norm_post, ffn_w_in, ffn_w_out,
              ssm_in_w, ssm_conv_w, ssm_conv_b, ssm_dt_bias, ssm_a_log, ssm_d, ssm_norm_w, ssm_out_w,
              attn_qkv_w, attn_qkv_b, attn_sinks, attn_o_w, attn_o_b, rel_bias):
    weights = (ada_w, ada_b, norm_pre, norm_post, ffn_w_in, ffn_w_out,
               ssm_in_w, ssm_conv_w, ssm_conv_b, ssm_dt_bias, ssm_a_log, ssm_d, ssm_norm_w, ssm_out_w,
               attn_qkv_w, attn_qkv_b, attn_sinks, attn_o_w, attn_o_b, rel_bias)
    bsz = x_prompt.shape[0]
    zero_ssm = jnp.zeros((N_SSM_LAYERS, bsz, SSM_HEADS, SSM_HEAD_DIM, SSM_STATE), x_prompt.dtype)
    zero_conv = jnp.zeros((N_SSM_LAYERS, bsz, SSM_CONV - 1, SSM_CONV_DIM), x_prompt.dtype)
    y_prompt, ssm_p, conv_p, k_p, v_p = trunk(x_prompt, c_prompt, zero_ssm, zero_conv, None, None, 0, *weights)
    y_sample, ssm_s, conv_s, k_s, v_s = trunk(x_sample, c_sample, state_ssm, state_conv, cache_k, cache_v, PAST_LEN, *weights)
    return (y_prompt, y_sample, ssm_p, conv_p, k_p, v_p, ssm_s, conv_s, k_s, v_s)
```

```python
import functools
import math

import numpy as np
import jax
import jax.numpy as jnp
from jax import lax
from jax.experimental import pallas as pl
from jax.experimental.pallas import tpu as pltpu

F32 = jnp.float32
BF16 = jnp.bfloat16

D_MODEL = 1024
N_SUB = 3
RMS_EPS = 1e-6
FFN_RES = 0.5
D_FF = 2816

SSM_D_INNER = 2048
SSM_HEAD_DIM = 64
SSM_HEADS = 32
SSM_GROUPS = 4
SSM_HPG = 8
SSM_STATE = 128
SSM_CONV = 4
SSM_CHUNK = 128
SSM_GN = SSM_GROUPS * SSM_STATE
SSM_CONV_DIM = SSM_D_INNER + 2 * SSM_GN
SSM_GROUP_WIDTH = SSM_HPG * SSM_HEAD_DIM

ATTN_HEAD_DIM = 64
ATTN_HEADS = 16
ATTN_KV_HEADS = 4
ATTN_REP = 4
WINDOW = 128
REL_BUCKETS = 32
ATTN_Q_DIM = ATTN_HEADS * ATTN_HEAD_DIM
ATTN_KV_DIM = ATTN_KV_HEADS * ATTN_HEAD_DIM

LANES = 128
SUBLANES = 8
HALF = LANES // 2
VMEM_LIMIT_BYTES = 56 * 1024 * 1024
CONV_PAD = SUBLANES


def _dot(a, b):
    return jnp.dot(a, b, preferred_element_type=F32)


def _dot_nt(a, b):
    return lax.dot_general(a, b, (((1,), (1,)), ((), ())), preferred_element_type=F32)


def _dot_tn(a, b):
    return lax.dot_general(a, b, (((0,), (0,)), ((), ())), preferred_element_type=F32)


def _silu(x):
    return x / (1.0 + jnp.exp(-x))


def _rms(x, g):
    return x * lax.rsqrt(jnp.mean(x * x, axis=-1, keepdims=True) + RMS_EPS) * g


def _mod_norm(x, g, scale, shift):
    return _rms(x, g) * (1.0 + scale) + shift


def _split3(x):
    hi = x.astype(BF16)
    r1 = x - hi.astype(F32)
    mid = r1.astype(BF16)
    lo = (r1 - mid.astype(F32)).astype(BF16)
    return hi, mid, lo


def _const_spec(shape, single_buffer=False):
    nd = len(shape)
    kw = {"pipeline_mode": pl.Buffered(1)} if single_buffer else {}
    return pl.BlockSpec(shape, lambda *_: (0,) * nd, **kw)


def _params(n_grid):
    return pltpu.CompilerParams(
        dimension_semantics=("arbitrary",) * n_grid,
        vmem_limit_bytes=VMEM_LIMIT_BYTES,
    )


def _ada_kernel(c_ref, w_ref, b_ref, o_ref):
    cs = _silu(c_ref[...]).astype(BF16)
    o_ref[0] = _dot(cs, w_ref[0].astype(BF16)) + b_ref[0]


def _ada(c_all, ada_w, ada_b, tn=1152):
    depth, d, n = ada_w.shape
    bc = c_all.shape[0]
    return pl.pallas_call(
        _ada_kernel,
        out_shape=jax.ShapeDtypeStruct((depth, bc, n), F32),
        grid=(depth, n // tn),
        in_specs=[
            pl.BlockSpec((bc, d), lambda l, j: (0, 0)),
            pl.BlockSpec((1, d, tn), lambda l, j: (l, 0, j)),
            pl.BlockSpec((1, 1, tn), lambda l, j: (l, 0, j)),
        ],
        out_specs=pl.BlockSpec((1, bc, tn), lambda l, j: (l, 0, j)),
        compiler_params=_params(2),
        name="ada_mod",
    )(c_all, ada_w, ada_b.reshape(depth, 1, n))


def _t5_bucket_table():
    i = np.arange(WINDOW)[:, None]
    j = np.arange(2 * WINDOW)[None, :]
    dist = i + WINDOW - j
    exact = REL_BUCKETS // 2
    df = np.maximum(dist, 1).astype(np.float32)
    large = exact + (np.log(df / np.float32(exact)) / np.float32(math.log(WINDOW / exact))
                     * np.float32(REL_BUCKETS - exact)).astype(np.int32)
    large = np.minimum(large, REL_BUCKETS - 1)
    bucket = np.where(dist < exact, dist, large)
    valid = (dist >= 0) & (dist <= WINDOW)
    return np.where(valid, bucket, -1).astype(np.int32)


def _bias_kernel(rb_ref, idx_ref, o_ref):
    h = pl.program_id(0)
    idx = idx_ref[...]
    acc = jnp.full(idx.shape, -jnp.inf, F32)
    for b in range(REL_BUCKETS):
        acc = jnp.where(idx == b, rb_ref[b, h], acc)
    o_ref[0] = acc


def _bias_table(rel_bias):
    idx = jnp.asarray(_t5_bucket_table())
    return pl.pallas_call(
        _bias_kernel,
        out_shape=jax.ShapeDtypeStruct((ATTN_HEADS, WINDOW, 2 * WINDOW), F32),
        grid=(ATTN_HEADS,),
        in_specs=[
            pl.BlockSpec(memory_space=pltpu.SMEM),
            pl.BlockSpec((WINDOW, 2 * WINDOW), lambda h: (0, 0)),
        ],
        out_specs=pl.BlockSpec((1, WINDOW, 2 * WINDOW), lambda h: (h, 0, 0)),
        compiler_params=_params(1),
        name="rel_bias_table",
    )(rel_bias, idx)


def _ffn_kernel(x_ref, sh_ref, sc_ref, gt_ref, gpre_ref, gpost_ref, win_ref, wout_ref, o_ref, *, ff_chunk):
    x = x_ref[...]
    bt, lt, d = x.shape
    h = _mod_norm(x, gpre_ref[...], sc_ref[...], sh_ref[...])
    hb = h.reshape(bt * lt, d).astype(BF16)
    acc = None
    for c in range(D_FF // ff_chunk):
        lo = c * ff_chunk
        g = _dot(hb, win_ref[:, lo:lo + ff_chunk])
        u = _dot(hb, win_ref[:, D_FF + lo:D_FF + lo + ff_chunk])
        a = (_silu(g) * u).astype(BF16)
        part = _dot(a, wout_ref[lo:lo + ff_chunk, :])
        acc = part if acc is None else acc + part
    f = acc.reshape(bt, lt, d)
    o_ref[...] = x + FFN_RES * gt_ref[...] * _rms(f, gpost_ref[...])


def _mod_specs(bt, sub):
    return [pl.BlockSpec((bt, None, 1, D_MODEL), functools.partial(lambda k, i, j: (i, k, 0, 0), sub * 3 + k))
            for k in range(3)]


def _ffn(x, mod, sub, gpre, gpost, w_in, w_out, bt, lt, ff_chunk=1408):
    b, l, d = x.shape
    xspec = pl.BlockSpec((bt, lt, d), lambda i, j: (i, j, 0))
    return pl.pallas_call(
        functools.partial(_ffn_kernel, ff_chunk=ff_chunk),
        out_shape=jax.ShapeDtypeStruct(x.shape, F32),
        grid=(b // bt, l // lt),
        in_specs=[xspec] + _mod_specs(bt, sub) + [
            _const_spec((1, 1, d)), _const_spec((1, 1, d)),
            _const_spec(w_in.shape, True), _const_spec(w_out.shape, True)],
        out_specs=xspec,
        compiler_params=_params(2),
        name="ffn_sublayer",
    )(x, mod, mod, mod, gpre, gpost, w_in, w_out)


def _ssd_chunk(r0, qc, seg, s_first, *, xbc_s, dt_s, da_s, y_s, e2_ref, h_in, h_out, consts):
    tu, eye, causal, lane_lo = consts
    rows = pl.ds(r0, qc)
    hi, mid, lo = _split3(da_s[rows, :])
    cs2 = _dot(tu, hi) + _dot(tu, mid) + _dot(tu, lo)
    a_cs = cs2[:qc]
    ea = jnp.exp(a_cs)
    dte = jnp.exp(cs2[qc:])

    stack = jnp.concatenate([dt_s[rows, :], ea, dte], axis=0)
    s_hi = stack.astype(BF16)
    s_lo = (stack - s_hi.astype(F32)).astype(BF16)
    sx = _dot(jnp.concatenate([s_hi, s_lo], axis=1), e2_ref[...])
    dt_x, ea_x, dte_x = sx[:qc], sx[qc:2 * qc], sx[2 * qc:]

    xdt = xbc_s[rows, 0:SSM_D_INNER] * dt_x
    xdt_b = xdt.astype(BF16)
    xd_b = (xdt * dte_x).astype(BF16)
    bm = xbc_s[rows, SSM_D_INNER:SSM_D_INNER + SSM_GN].astype(BF16)
    cm = xbc_s[rows, SSM_D_INNER + SSM_GN:SSM_CONV_DIM].astype(BF16)

    a_hi, a_mid, a_lo = _split3(a_cs)
    a_cs_t = _dot_nt(eye, a_hi) + _dot_nt(eye, a_mid) + _dot_nt(eye, a_lo)

    zero_b = jnp.zeros((), BF16)
    for g in range(SSM_GROUPS):
        gsl = slice(g * SSM_STATE, (g + 1) * SSM_STATE)
        cb = _dot_nt(cm[:, gsl], bm[:, gsl])
        for pr in range(SSM_HPG // 2):
            h0 = g * SSM_HPG + 2 * pr
            psl = slice((h0 // 2) * LANES, (h0 // 2 + 1) * LANES)
            xp = xdt_b[:, psl]
            acc = None
            for half in range(2):
                h = h0 + half
                seg_sum = a_cs[:, h:h + 1] - a_cs_t[h:h + 1, :]
                decay = jnp.exp(jnp.where(causal, seg_sum, -jnp.inf))
                w = (decay * cb).astype(BF16)
                xh = jnp.where(lane_lo if half == 0 else jnp.logical_not(lane_lo), xp, zero_b)
                o = _dot(w, xh)
                acc = o if acc is None else acc + o
            y_s[rows, psl] = acc

    for t in range(qc // seg):
        tr = slice(t * seg, (t + 1) * seg)
        trows = pl.ds(r0 + t * seg, seg)
        last = t * seg + seg - 1
        for g in range(SSM_GROUPS):
            gsl = slice(g * SSM_STATE, (g + 1) * SSM_STATE)
            csl = slice(g * SSM_GROUP_WIDTH, (g + 1) * SSM_GROUP_WIDTH)
            hsl = slice(g * SSM_HPG, (g + 1) * SSM_HPG)
            hg = h_in[s_first + t, hsl].reshape(SSM_GROUP_WIDTH, SSM_STATE)
            y_off = _dot_nt(cm[tr, gsl], hg.astype(BF16)) * ea_x[tr, csl]
            y_s[trows, csl] = y_s[trows, csl] + y_off
            upd = _dot_tn(xd_b[tr, csl], bm[tr, gsl])
            cdec = jnp.concatenate(
                [jnp.broadcast_to(ea[last:last + 1, g * SSM_HPG + r:g * SSM_HPG + r + 1],
                                  (SSM_HEAD_DIM, SSM_STATE)) for r in range(SSM_HPG)], axis=0)
            h_out[s_first + t, hsl] = (hg * cdec + upd).reshape(SSM_HPG, SSM_HEAD_DIM, SSM_STATE)


def _ssm_kernel(*refs, has_state, qc, seg):
    if has_state:
        (x_ref, sh_ref, sc_ref, gt_ref, gpre_ref, gpost_ref, wz_ref, wxbc_ref, wdt_ref, cw_ref, cb_ref,
         dtb_ref, alog_ref, dx_ref, nw_ref, wo_ref, e2_ref, conv_in_ref, h0_ref,
         o_ref, conv_out_ref, h_out_ref, xpad, z_s, xbc_s, dt_s, da_s, y_s) = refs
    else:
        (x_ref, sh_ref, sc_ref, gt_ref, gpre_ref, gpost_ref, wz_ref, wxbc_ref, wdt_ref, cw_ref, cb_ref,
         dtb_ref, alog_ref, dx_ref, nw_ref, wo_ref, e2_ref,
         o_ref, conv_out_ref, h_out_ref, xpad, z_s, xbc_s, dt_s, da_s, y_s) = refs
        conv_in_ref = h0_ref = None
    j = pl.program_id(1)
    x = x_ref[...]
    bt, lt, d = x.shape
    m = bt * lt
    hist = SSM_CONV - 1

    hb = _mod_norm(x, gpre_ref[...], sc_ref[...], sh_ref[...]).reshape(m, d).astype(BF16)
    z_s[...] = _dot(hb, wz_ref[...])
    xpad[:, CONV_PAD:CONV_PAD + lt, :] = _dot(hb, wxbc_ref[...]).reshape(bt, lt, SSM_CONV_DIM)
    dt_raw = _dot(hb, wdt_ref[...]) + dtb_ref[...]
    dt = jnp.maximum(dt_raw, 0.0) + jnp.log1p(jnp.exp(-jnp.abs(dt_raw)))
    dt_s[...] = dt
    da_s[...] = dt * (-jnp.exp(alog_ref[...]))

    if has_state:
        xpad[:, CONV_PAD - hist:CONV_PAD, :] = conv_in_ref[...]
    else:
        @pl.when(j == 0)
        def _():
            xpad[:, CONV_PAD - hist:CONV_PAD, :] = jnp.zeros((bt, hist, SSM_CONV_DIM), F32)
            h_out_ref[...] = jnp.zeros(h_out_ref.shape, F32)
    conv = cb_ref[...]
    for k in range(SSM_CONV):
        conv = conv + xpad[:, CONV_PAD - hist + k:CONV_PAD - hist + k + lt, :] * cw_ref[k:k + 1, :]
    xbc_s[...] = _silu(conv).reshape(m, SSM_CONV_DIM)
    new_hist = xpad[:, CONV_PAD + lt - hist:CONV_PAD + lt, :]
    conv_out_ref[...] = new_hist
    if not has_state:
        xpad[:, CONV_PAD - hist:CONV_PAD, :] = new_hist

    r = lax.broadcasted_iota(jnp.int32, (qc, qc), 0)
    c = lax.broadcasted_iota(jnp.int32, (qc, qc), 1)
    seg_shift = seg.bit_length() - 1
    same = jnp.right_shift(r, seg_shift) == jnp.right_shift(c, seg_shift)
    causal = same & (r >= c)
    tri = jnp.where(causal, 1.0, 0.0)
    upper = jnp.where(same & (c > r), 1.0, 0.0)
    tu = jnp.concatenate([tri, upper], axis=0).astype(BF16)
    er = lax.broadcasted_iota(jnp.int32, (LANES, LANES), 0)
    ec = lax.broadcasted_iota(jnp.int32, (LANES, LANES), 1)
    eye = jnp.where(er == ec, 1.0, 0.0).astype(BF16)
    lane_lo = lax.broadcasted_iota(jnp.int32, (qc, LANES), 1) < HALF
    consts = (tu, eye, causal, lane_lo)

    scr = dict(xbc_s=xbc_s, dt_s=dt_s, da_s=da_s, y_s=y_s, e2_ref=e2_ref, consts=consts)
    if has_state:
        _ssd_chunk(0, qc, seg, 0, h_in=h0_ref, h_out=h_out_ref, **scr)
    else:
        def body(ci, carry):
            _ssd_chunk(pl.multiple_of(ci * qc, qc), qc, seg, 0, h_in=h_out_ref, h_out=h_out_ref, **scr)
            return carry
        lax.fori_loop(0, m // qc, body, 0)

    yg = (y_s[...] + xbc_s[:, 0:SSM_D_INNER] * dx_ref[...]) * _silu(z_s[...])
    parts = []
    for g in range(SSM_GROUPS):
        v = yg[:, g * SSM_GROUP_WIDTH:(g + 1) * SSM_GROUP_WIDTH]
        parts.append(v * lax.rsqrt(jnp.mean(v * v, axis=-1, keepdims=True) + RMS_EPS))
    yn = (jnp.concatenate(parts, axis=1) * nw_ref[...]).astype(BF16)
    out = _dot(yn, wo_ref[...]).reshape(bt, lt, d)
    o_ref[...] = x + gt_ref[...] * _rms(out, gpost_ref[...])


def _ssm(x, mod, gpre, gpost, w, conv_in, h0, bt, lt):
    b, l, d = x.shape
    has_state = h0 is not None
    m = bt * lt
    if has_state:
        qc, seg = m, lt
    else:
        qc, seg = SSM_CHUNK, SSM_CHUNK
    hist = SSM_CONV - 1
    xspec = pl.BlockSpec((bt, lt, d), lambda i, j: (i, j, 0))
    conv_spec = pl.BlockSpec((bt, hist, SSM_CONV_DIM), lambda i, j: (i, 0, 0))
    state_spec = pl.BlockSpec((bt, SSM_HEADS, SSM_HEAD_DIM, SSM_STATE), lambda i, j: (i, 0, 0, 0))
    weights = [w["wz"], w["wxbc"], w["wdt"], w["conv_w"], w["conv_b"], w["dt_bias"], w["a_log"],
               w["d_x"], w["norm_w"], w["wo"], w["e2"]]
    in_specs = ([xspec] + _mod_specs(bt, 1) + [_const_spec((1, 1, d)), _const_spec((1, 1, d))]
                + [_const_spec(a.shape, a.dtype == BF16) for a in weights])
    args = [x, mod, mod, mod, gpre, gpost] + weights
    if has_state:
        in_specs += [conv_spec, state_spec]
        args += [conv_in, h0]
    return pl.pallas_call(
        functools.partial(_ssm_kernel, has_state=has_state, qc=qc, seg=seg),
        out_shape=(jax.ShapeDtypeStruct(x.shape, F32),
                   jax.ShapeDtypeStruct((b, hist, SSM_CONV_DIM), F32),
                   jax.ShapeDtypeStruct((b, SSM_HEADS, SSM_HEAD_DIM, SSM_STATE), F32)),
        grid=(b // bt, l // lt),
        in_specs=in_specs,
        out_specs=(xspec, conv_spec, state_spec),
        scratch_shapes=[
            pltpu.VMEM((bt, CONV_PAD + lt, SSM_CONV_DIM), F32),
            pltpu.VMEM((m, SSM_D_INNER), F32),
            pltpu.VMEM((m, SSM_CONV_DIM), F32),
            pltpu.VMEM((m, LANES), F32),
            pltpu.VMEM((m, LANES), F32),
            pltpu.VMEM((m, SSM_D_INNER), F32),
        ],
        compiler_params=_params(2),
        name="ssm_sublayer_state" if has_state else "ssm_sublayer",
    )(*args)


def _ssm_weights(in_w, conv_w, conv_b, dt_bias, a_log, d_skip, norm_w, out_w):
    pad = LANES - SSM_HEADS
    heads = np.arange(SSM_D_INNER) // SSM_HEAD_DIM
    e = (np.arange(LANES)[:, None] == heads[None, :]).astype(np.float32)
    return dict(
        wz=in_w[:, :SSM_D_INNER].astype(BF16),
        wxbc=in_w[:, SSM_D_INNER:SSM_D_INNER + SSM_CONV_DIM].astype(BF16),
        wdt=jnp.pad(in_w[:, SSM_D_INNER + SSM_CONV_DIM:], ((0, 0), (0, pad))).astype(BF16),
        conv_w=conv_w,
        conv_b=conv_b.reshape(1, SSM_CONV_DIM),
        dt_bias=jnp.pad(dt_bias, (0, pad)).reshape(1, LANES),
        a_log=jnp.pad(a_log, (0, pad)).reshape(1, LANES),
        d_x=jnp.repeat(d_skip, SSM_HEAD_DIM).reshape(1, SSM_D_INNER),
        norm_w=norm_w.reshape(1, SSM_D_INNER),
        wo=out_w.astype(BF16),
        e2=jnp.asarray(np.concatenate([e, e], axis=0), BF16),
    )


def _softmax_sink(logits, sink):
    mx = jnp.maximum(jnp.max(logits, axis=-1, keepdims=True), sink)
    e = jnp.exp(logits - mx)
    denom = jnp.sum(e, axis=-1, keepdims=True) + jnp.exp(sink - mx)
    return e * (1.0 / denom)


def _dedup(t):
    lane_lo = lax.broadcasted_iota(jnp.int32, (t.shape[0], LANES), 1) < HALF
    tiles = [jnp.where(lane_lo, t[:, (2 * i) * LANES:(2 * i + 1) * LANES],
                       t[:, (2 * i + 1) * LANES:(2 * i + 2) * LANES]) for i in range(ATTN_KV_HEADS // 2)]
    return jnp.concatenate(tiles, axis=1)


def _swa_prompt_kernel(sinks_ref, x_ref, sh_ref, sc_ref, gt_ref, gpre_ref, gpost_ref, wq_ref, wk_ref, wv_ref,
                       bq_ref, bk_ref, bv_ref, bias_ref, wo_ref, bo_ref,
                       o_ref, kc_ref, vc_ref, q_s, kbuf, vbuf, att_s):
    j = pl.program_id(1)
    x = x_ref[0]
    tq = x.shape[0]
    kvw = ATTN_KV_HEADS * LANES

    @pl.when(j == 0)
    def _():
        kbuf[0:WINDOW, :] = jnp.zeros((WINDOW, kvw), BF16)
        vbuf[0:WINDOW, :] = jnp.zeros((WINDOW, kvw), BF16)

    hb = _mod_norm(x, gpre_ref[0], sc_ref[0], sh_ref[0]).astype(BF16)
    q_s[...] = ((_dot(hb, wq_ref[...]) + bq_ref[...]) * (ATTN_HEAD_DIM ** -0.5)).astype(BF16)
    k = _dot(hb, wk_ref[...]) + bk_ref[...]
    v = _dot(hb, wv_ref[...]) + bv_ref[...]
    kbuf[WINDOW:WINDOW + tq, :] = k.astype(BF16)
    vbuf[WINDOW:WINDOW + tq, :] = v.astype(BF16)
    kc_ref[0] = _dedup(k[tq - WINDOW:, :])
    vc_ref[0] = _dedup(v[tq - WINDOW:, :])

    lane_lo = lax.broadcasted_iota(jnp.int32, (2 * WINDOW, LANES), 1) < HALF
    col = lax.broadcasted_iota(jnp.int32, (WINDOW, 2 * WINDOW), 1)
    zero_b = jnp.zeros((), BF16)

    def block(bi, carry):
        r0 = pl.multiple_of(bi * WINDOW, WINDOW)
        no_prev = jnp.logical_and(j == 0, bi == 0)
        prev_mask = jnp.where(jnp.logical_and(no_prev, col < WINDOW), -jnp.inf, 0.0)
        for g in range(ATTN_KV_HEADS):
            kd = kbuf[pl.ds(r0, 2 * WINDOW), g * LANES:(g + 1) * LANES]
            vd = vbuf[pl.ds(r0, 2 * WINDOW), g * LANES:(g + 1) * LANES]
            k_half = (jnp.where(lane_lo, kd, zero_b), jnp.where(lane_lo, zero_b, kd))
            v_half = (jnp.where(lane_lo, vd, zero_b), jnp.where(lane_lo, zero_b, vd))
            for pr in range(ATTN_REP // 2):
                pair = g * (ATTN_REP // 2) + pr
                qp = q_s[pl.ds(r0, WINDOW), pair * LANES:(pair + 1) * LANES]
                acc = None
                for half in range(2):
                    h = 2 * pair + half
                    logits = _dot_nt(qp, k_half[half]) + bias_ref[h] + prev_mask
                    p = _softmax_sink(logits, sinks_ref[h]).astype(BF16)
                    o = _dot(p, v_half[half])
                    acc = o if acc is None else acc + o
                att_s[pl.ds(r0, WINDOW), pair * LANES:(pair + 1) * LANES] = acc.astype(BF16)
        return carry

    lax.fori_loop(0, tq // WINDOW, block, 0)

    out = _dot(att_s[...], wo_ref[...]) + bo_ref[...]
    o_ref[0] = x + gt_ref[0] * _rms(out, gpost_ref[0])
    kbuf[0:WINDOW, :] = kbuf[tq:tq + WINDOW, :]
    vbuf[0:WINDOW, :] = vbuf[tq:tq + WINDOW, :]


def _swa_prompt(x, mod, gpre, gpost, w, bias, tq):
    b, l, d = x.shape
    kvw = ATTN_KV_HEADS * LANES
    xspec = pl.BlockSpec((1, tq, d), lambda i, j: (i, j, 0))
    cache_spec = pl.BlockSpec((1, WINDOW, ATTN_KV_DIM), lambda i, j: (i, 0, 0))
    consts = [w["wq"], w["wk_dup"], w["wv_dup"], w["bq"], w["bk_dup"], w["bv_dup"], bias, w["wo"], w["bo"]]
    return pl.pallas_call(
        _swa_prompt_kernel,
        out_shape=(jax.ShapeDtypeStruct(x.shape, F32),
                   jax.ShapeDtypeStruct((b, WINDOW, ATTN_KV_DIM), F32),
                   jax.ShapeDtypeStruct((b, WINDOW, ATTN_KV_DIM), F32)),
        grid=(b // 1, l // tq),
        in_specs=[pl.BlockSpec(memory_space=pltpu.SMEM), xspec] + _mod_specs(1, 1)
        + [_const_spec((1, 1, d)), _const_spec((1, 1, d))]
        + [_const_spec(a.shape, a.dtype == BF16) for a in consts],
        out_specs=(xspec, cache_spec, cache_spec),
        scratch_shapes=[
            pltpu.VMEM((tq, ATTN_Q_DIM), BF16),
            pltpu.VMEM((WINDOW + tq, kvw), BF16),
            pltpu.VMEM((WINDOW + tq, kvw), BF16),
            pltpu.VMEM((tq, ATTN_Q_DIM), BF16),
        ],
        compiler_params=_params(2),
        name="swa_prompt_sublayer",
    )(w["sinks"], x, mod, mod, mod, gpre, gpost, *consts)


def _swa_sample_kernel(x_ref, sh_ref, sc_ref, gt_ref, gpre_ref, gpost_ref, wq_ref, wk_ref, wv_ref,
                       bq_ref, bk_ref, bv_ref, ck_ref, cv_ref, bias_ref, sink_ref, wo_ref, bo_ref,
                       o_ref, ko_ref, vo_ref, q_s, kn_s, vn_s, kbuf, vbuf, att_s):
    x = x_ref[...]
    bt, lt, d = x.shape
    m = bt * lt
    nbuf = ck_ref.shape[1]
    wide = ATTN_KV_HEADS * ATTN_HEAD_DIM

    hb = _mod_norm(x, gpre_ref[...], sc_ref[...], sh_ref[...]).reshape(m, d).astype(BF16)
    q_s[...] = (_dot(hb, wq_ref[...]) + bq_ref[...]) * (ATTN_HEAD_DIM ** -0.5)
    kn_s[...] = _dot(hb, wk_ref[...]) + bk_ref[...]
    vn_s[...] = _dot(hb, wv_ref[...]) + bv_ref[...]
    pad_rows = kbuf.shape[0] - nbuf - lt
    kbuf[nbuf + lt:, :] = jnp.zeros((pad_rows, ATTN_KV_DIM), F32)
    vbuf[nbuf + lt:, :] = jnp.zeros((pad_rows, ATTN_KV_DIM), F32)

    def seq(bi, carry):
        rows = pl.ds(pl.multiple_of(bi * lt, lt), lt)
        kn = kn_s[rows, :]
        vn = vn_s[rows, :]
        ck = ck_ref[bi]
        cv = cv_ref[bi]
        kbuf[0:nbuf, :] = ck
        vbuf[0:nbuf, :] = cv
        kbuf[nbuf:nbuf + lt, :] = kn
        vbuf[nbuf:nbuf + lt, :] = vn
        ko_ref[bi, 0:nbuf - lt, :] = ck[lt:, :]
        vo_ref[bi, 0:nbuf - lt, :] = cv[lt:, :]
        ko_ref[bi, nbuf - lt:nbuf, :] = kn
        vo_ref[bi, nbuf - lt:nbuf, :] = vn
        qbig = jnp.concatenate([q_s[rows, h * wide:(h + 1) * wide] for h in range(ATTN_HEADS)], axis=0)
        logits = _dot_nt(qbig.astype(BF16), kbuf[...].astype(BF16)) + bias_ref[...]
        p = _softmax_sink(logits, sink_ref[:, 0:1]).astype(BF16)
        res = _dot(p, vbuf[...].astype(BF16))
        att_s[rows, :] = jnp.concatenate([res[h * lt:(h + 1) * lt, :] for h in range(ATTN_HEADS)], axis=1)
        return carry

    lax.fori_loop(0, bt, seq, 0)

    out = (_dot(att_s[...].astype(BF16), wo_ref[...]) + bo_ref[...]).reshape(bt, lt, d)
    o_ref[...] = x + gt_ref[...] * _rms(out, gpost_ref[...])


def _swa_sample(x, mod, gpre, gpost, w, bias, cache_k, cache_v, bt):
    b, lt, d = x.shape
    nbuf = cache_k.shape[1]
    m = bt * lt
    wide = ATTN_KV_HEADS * ATTN_HEAD_DIM
    big = ATTN_HEADS * wide
    keys = 2 * WINDOW
    xspec = pl.BlockSpec((bt, lt, d), lambda i: (i, 0, 0))
    cache_spec = pl.BlockSpec((bt, nbuf, ATTN_KV_DIM), lambda i: (i, 0, 0))
    bias_s = bias[:, :lt, :].reshape(ATTN_HEADS * lt, keys)
    sink_col = jnp.broadcast_to(jnp.repeat(w["sinks"], lt)[:, None], (ATTN_HEADS * lt, LANES))
    mods = [pl.BlockSpec((bt, None, 1, D_MODEL), functools.partial(lambda k, i: (i, k, 0, 0), 3 + k))
            for k in range(3)]
    pre = [w["wq_big"], w["wk"], w["wv"], w["bq_big"], w["bk"], w["bv"]]
    post = [bias_s, sink_col, w["wo_big"], w["bo"]]
    return pl.pallas_call(
        _swa_sample_kernel,
        out_shape=(jax.ShapeDtypeStruct(x.shape, F32),
                   jax.ShapeDtypeStruct(cache_k.shape, F32),
                   jax.ShapeDtypeStruct(cache_v.shape, F32)),
        grid=(b // bt,),
        in_specs=[xspec] + mods + [_const_spec((1, 1, d)), _const_spec((1, 1, d))]
        + [_const_spec(a.shape, a.dtype == BF16) for a in pre] + [cache_spec, cache_spec]
        + [_const_spec(a.shape, a.dtype == BF16) for a in post],
        out_specs=(xspec, cache_spec, cache_spec),
        scratch_shapes=[
            pltpu.VMEM((m, big), F32),
            pltpu.VMEM((m, ATTN_KV_DIM), F32),
            pltpu.VMEM((m, ATTN_KV_DIM), F32),
            pltpu.VMEM((keys, ATTN_KV_DIM), F32),
            pltpu.VMEM((keys, ATTN_KV_DIM), F32),
            pltpu.VMEM((m, big), F32),
        ],
        compiler_params=_params(1),
        name="swa_sample_sublayer",
    )(x, mod, mod, mod, gpre, gpost, *pre, cache_k, cache_v, *post)


def _swa_weights(qkv_w, qkv_b, sinks, o_w, o_b):
    d = qkv_w.shape[0]
    wq = qkv_w[:, :ATTN_Q_DIM]
    wk = qkv_w[:, ATTN_Q_DIM:ATTN_Q_DIM + ATTN_KV_DIM]
    wv = qkv_w[:, ATTN_Q_DIM + ATTN_KV_DIM:]
    bq = qkv_b[:ATTN_Q_DIM]
    bk = qkv_b[ATTN_Q_DIM:ATTN_Q_DIM + ATTN_KV_DIM]
    bv = qkv_b[ATTN_Q_DIM + ATTN_KV_DIM:]

    def dup(a):
        a4 = a.reshape(a.shape[:-1] + (ATTN_KV_HEADS, 1, ATTN_HEAD_DIM))
        a4 = jnp.broadcast_to(a4, a.shape[:-1] + (ATTN_KV_HEADS, 2, ATTN_HEAD_DIM))
        return a4.reshape(a.shape[:-1] + (ATTN_KV_HEADS * LANES,))

    own = jnp.asarray((np.arange(ATTN_HEADS)[:, None] // ATTN_REP == np.arange(ATTN_KV_HEADS)[None, :])
                      .astype(np.float32))
    wq_big = (wq.reshape(d, ATTN_HEADS, 1, ATTN_HEAD_DIM) * own[None, :, :, None]).reshape(d, -1)
    bq_big = (bq.reshape(ATTN_HEADS, 1, ATTN_HEAD_DIM) * own[:, :, None]).reshape(1, -1)
    wo_big = (o_w.reshape(ATTN_HEADS, 1, ATTN_HEAD_DIM, d) * own[:, :, None, None]).reshape(-1, d)
    return dict(
        sinks=sinks,
        wq=wq.astype(BF16), wk_dup=dup(wk).astype(BF16), wv_dup=dup(wv).astype(BF16),
        bq=bq.reshape(1, -1), bk_dup=dup(bk).reshape(1, -1), bv_dup=dup(bv).reshape(1, -1),
        wo=o_w.astype(BF16), bo=o_b.reshape(1, d),
        wq_big=wq_big.astype(BF16), bq_big=bq_big, wk=wk.astype(BF16), wv=wv.astype(BF16),
        bk=bk.reshape(1, -1), bv=bv.reshape(1, -1), wo_big=wo_big.astype(BF16),
    )


def _trunk(x, mod_all, state_ssm, state_conv, cache_k, cache_v, norm_pre, norm_post, ffn_in, ffn_out,
           ssm_w, swa_w, bias, tiles):
    bt, lt = tiles["ffn"]
    b, l, d = x.shape
    sample = state_ssm is not None
    outs = {}
    for i in range(2):
        mod = mod_all[i].reshape(b, N_SUB * 3, 1, d)
        gpre = norm_pre[i].reshape(N_SUB, 1, 1, d)
        gpost = norm_post[i].reshape(N_SUB, 1, 1, d)
        x = _ffn(x, mod, 0, gpre[0], gpost[0], ffn_in[i][0], ffn_out[i][0], bt, lt)
        if i == 0:
            sbt, slt = tiles["ssm"]
            x, conv_new, ssm_new = _ssm(x, mod, gpre[1], gpost[1], ssm_w,
                                        state_conv[0] if sample else None,
                                        state_ssm[0] if sample else None, sbt, slt)
            outs["conv"] = conv_new[None]
            outs["ssm"] = ssm_new[None]
        else:
            if sample:
                x, k_new, v_new = _swa_sample(x, mod, gpre[1], gpost[1], swa_w, bias,
                                              cache_k[0].reshape(b, -1, ATTN_KV_DIM),
                                              cache_v[0].reshape(b, -1, ATTN_KV_DIM), tiles["swa"])
            else:
                x, k_new, v_new = _swa_prompt(x, mod, gpre[1], gpost[1], swa_w, bias, tiles["swa"])
            shape = (1, b, -1, ATTN_KV_HEADS, ATTN_HEAD_DIM)
            outs["k"] = k_new.reshape(shape)
            outs["v"] = v_new.reshape(shape)
        x = _ffn(x, mod, 2, gpre[2], gpost[2], ffn_in[i][1], ffn_out[i][1], bt, lt)
    return x, outs["ssm"], outs["conv"], outs["k"], outs["v"]


def kernel(x_prompt, x_sample, state_ssm, state_conv, cache_k, cache_v, c_prompt, c_sample, ada_w, ada_b, norm_pre, norm_post, ffn_w_in, ffn_w_out, ssm_in_w, ssm_conv_w, ssm_conv_b, ssm_dt_bias, ssm_a_log, ssm_d, ssm_norm_w, ssm_out_w, attn_qkv_w, attn_qkv_b, attn_sinks, attn_o_w, attn_o_b, rel_bias):
    nb = x_prompt.shape[0]
    mod_all = _ada(jnp.concatenate([c_prompt, c_sample], axis=0), ada_w, ada_b)
    bias = _bias_table(rel_bias)
    ffn_in = ffn_w_in.astype(BF16)
    ffn_out = ffn_w_out.astype(BF16)
    ssm_w = _ssm_weights(ssm_in_w[0], ssm_conv_w[0], ssm_conv_b[0], ssm_dt_bias[0], ssm_a_log[0], ssm_d[0],
                         ssm_norm_w[0], ssm_out_w[0])
    swa_w = _swa_weights(attn_qkv_w[0], attn_qkv_b[0], attn_sinks[0], attn_o_w[0], attn_o_b[0])
    common = (norm_pre, norm_post, ffn_in, ffn_out, ssm_w, swa_w, bias)

    p_tiles = dict(ffn=(1, 512), ssm=(1, 256), swa=512)
    y_p, ssm_p, conv_p, k_p, v_p = _trunk(x_prompt, mod_all[:, :nb], None, None, None, None, *common, p_tiles)
    ns, ls = x_sample.shape[:2]
    s_tiles = dict(ffn=(min(64, ns), ls), ssm=(min(4, ns), ls), swa=min(16, ns))
    y_s, ssm_s, conv_s, k_s, v_s = _trunk(x_sample, mod_all[:, nb:], state_ssm, state_conv, cache_k, cache_v,
                                          *common, s_tiles)
    return (y_p, y_s, ssm_p, conv_p, k_p, v_p, ssm_s, conv_s, k_s, v_s)
```

```python
import functools
import math

import numpy as np
import jax
import jax.numpy as jnp
from jax import lax
from jax.experimental import pallas as pl
from jax.experimental.pallas import tpu as pltpu

F32 = jnp.float32
BF16 = jnp.bfloat16

D_MODEL = 1024
N_SUB = 3
RMS_EPS = 1e-6
FFN_RES = 0.5
D_FF = 2816

SSM_D_INNER = 2048
SSM_HEAD_DIM = 64
SSM_HEADS = 32
SSM_GROUPS = 4
SSM_HPG = 8
SSM_STATE = 128
SSM_CONV = 4
SSM_CHUNK = 128
SSM_GN = SSM_GROUPS * SSM_STATE
SSM_CONV_DIM = SSM_D_INNER + 2 * SSM_GN
SSM_GROUP_WIDTH = SSM_HPG * SSM_HEAD_DIM

ATTN_HEAD_DIM = 64
ATTN_HEADS = 16
ATTN_KV_HEADS = 4
ATTN_REP = 4
WINDOW = 128
REL_BUCKETS = 32
ATTN_Q_DIM = ATTN_HEADS * ATTN_HEAD_DIM
ATTN_KV_DIM = ATTN_KV_HEADS * ATTN_HEAD_DIM

LANES = 128
SUBLANES = 8
HALF = LANES // 2
VMEM_LIMIT_BYTES = 56 * 1024 * 1024
CONV_PAD = SUBLANES


def _dot(a, b):
    return jnp.dot(a, b, preferred_element_type=F32)


def _dot_nt(a, b):
    return lax.dot_general(a, b, (((1,), (1,)), ((), ())), preferred_element_type=F32)


def _dot_tn(a, b):
    return lax.dot_general(a, b, (((0,), (0,)), ((), ())), preferred_element_type=F32)


def _silu(x):
    return x / (1.0 + jnp.exp(-x))


def _rms(x, g):
    return x * lax.rsqrt(jnp.mean(x * x, axis=-1, keepdims=True) + RMS_EPS) * g


def _mod_norm(x, g, scale, shift):
    return _rms(x, g) * (1.0 + scale) + shift


def _split3(x):
    hi = x.astype(BF16)
    r1 = x - hi.astype(F32)
    mid = r1.astype(BF16)
    lo = (r1 - mid.astype(F32)).astype(BF16)
    return hi, mid, lo


def _const_spec(shape, single_buffer=False):
    nd = len(shape)
    kw = {"pipeline_mode": pl.Buffered(1)} if single_buffer else {}
    return pl.BlockSpec(shape, lambda *_: (0,) * nd, **kw)


def _params(n_grid):
    return pltpu.CompilerParams(
        dimension_semantics=("arbitrary",) * n_grid,
        vmem_limit_bytes=VMEM_LIMIT_BYTES,
    )


def _ada_kernel(c_ref, w_ref, b_ref, o_ref):
    cs = _silu(c_ref[...]).astype(BF16)
    o_ref[0] = _dot(cs, w_ref[0].astype(BF16)) + b_ref[0]


def _ada(c_all, ada_w, ada_b, tn=1152):
    depth, d, n = ada_w.shape
    bc = c_all.shape[0]
    return pl.pallas_call(
        _ada_kernel,
        out_shape=jax.ShapeDtypeStruct((depth, bc, n), F32),
        grid=(depth, n // tn),
        in_specs=[
            pl.BlockSpec((bc, d), lambda l, j: (0, 0)),
            pl.BlockSpec((1, d, tn), lambda l, j: (l, 0, j)),
            pl.BlockSpec((1, 1, tn), lambda l, j: (l, 0, j)),
        ],
        out_specs=pl.BlockSpec((1, bc, tn), lambda l, j: (l, 0, j)),
        compiler_params=_params(2),
        name="ada_mod",
    )(c_all, ada_w, ada_b.reshape(depth, 1, n))


def _t5_bucket_table():
    i = np.arange(WINDOW)[:, None]
    j = np.arange(2 * WINDOW)[None, :]
    dist = i + WINDOW - j
    exact = REL_BUCKETS // 2
    df = np.maximum(dist, 1).astype(np.float32)
    large = exact + (np.log(df / np.float32(exact)) / np.float32(math.log(WINDOW / exact))
                     * np.float32(REL_BUCKETS - exact)).astype(np.int32)
    large = np.minimum(large, REL_BUCKETS - 1)
    bucket = np.where(dist < exact, dist, large)
    valid = (dist >= 0) & (dist <= WINDOW)
    return np.where(valid, bucket, -1).astype(np.int32)


def _bias_kernel(rb_ref, idx_ref, o_ref):
    h = pl.program_id(0)
    idx = idx_ref[...]
    acc = jnp.full(idx.shape, -jnp.inf, F32)
    for b in range(REL_BUCKETS):
        acc = jnp.where(idx == b, rb_ref[b, h], acc)
    o_ref[0] = acc


def _bias_table(rel_bias):
    idx = jnp.asarray(_t5_bucket_table())
    return pl.pallas_call(
        _bias_kernel,
        out_shape=jax.ShapeDtypeStruct((ATTN_HEADS, WINDOW, 2 * WINDOW), F32),
        grid=(ATTN_HEADS,),
        in_specs=[
            pl.BlockSpec(memory_space=pltpu.SMEM),
            pl.BlockSpec((WINDOW, 2 * WINDOW), lambda h: (0, 0)),
        ],
        out_specs=pl.BlockSpec((1, WINDOW, 2 * WINDOW), lambda h: (h, 0, 0)),
        compiler_params=_params(1),
        name="rel_bias_table",
    )(rel_bias, idx)


MXU_TILE = 256
FF_CHUNKS = ((0, 6 * MXU_TILE), (6 * MXU_TILE, D_FF))


def _ffn_kernel(x_ref, sh_ref, sc_ref, gt_ref, gpre_ref, gpost_ref, win_ref, wout_ref, o_ref, *, ff_chunks, n_sub):
    bt, lt, d = x_ref.shape

    def sub_slices(s):
        if bt == 1:
            return slice(None), slice(s * (lt // n_sub), (s + 1) * (lt // n_sub))
        return slice(s * (bt // n_sub), (s + 1) * (bt // n_sub)), slice(None)

    def pre(s):
        bs, ls = sub_slices(s)
        x = x_ref[bs, ls, :]
        h = _mod_norm(x, gpre_ref[...], sc_ref[bs], sh_ref[bs])
        return h.reshape(x.shape[0] * x.shape[1], d).astype(BF16)

    def post(s, acc):
        bs, ls = sub_slices(s)
        x = x_ref[bs, ls, :]
        o_ref[bs, ls, :] = x + FFN_RES * gt_ref[bs] * _rms(acc.reshape(x.shape), gpost_ref[...])

    hb = pre(0)
    prev = None
    for s in range(n_sub):
        acc = None
        hb_next = None
        for c, (lo, hi) in enumerate(ff_chunks):
            g = _dot(hb, win_ref[:, lo:hi])
            u = _dot(hb, win_ref[:, D_FF + lo:D_FF + hi])
            a = (_silu(g) * u).astype(BF16)
            part = _dot(a, wout_ref[lo:hi, :])
            acc = part if acc is None else acc + part
            if c == 0:
                if prev is not None:
                    post(s - 1, prev)
                if s + 1 < n_sub:
                    hb_next = pre(s + 1)
        prev = acc
        hb = hb_next
    post(n_sub - 1, prev)


def _mod_specs(bt, sub):
    return [pl.BlockSpec((bt, None, 1, D_MODEL), functools.partial(lambda k, i, j: (i, k, 0, 0), sub * 3 + k))
            for k in range(3)]


def _ffn(x, mod, sub, gpre, gpost, w_in, w_out, layer, which, bt, lt, n_sub):
    b, l, d = x.shape
    xspec = pl.BlockSpec((bt, lt, d), lambda i, j: (i, j, 0))
    wspec = [pl.BlockSpec((None, None) + w.shape[2:], lambda i, j: (layer, which, 0, 0),
                          pipeline_mode=pl.Buffered(1)) for w in (w_in, w_out)]
    return pl.pallas_call(
        functools.partial(_ffn_kernel, ff_chunks=FF_CHUNKS, n_sub=n_sub),
        out_shape=jax.ShapeDtypeStruct(x.shape, F32),
        grid=(b // bt, l // lt),
        in_specs=[xspec] + _mod_specs(bt, sub) + [
            _const_spec((1, 1, d)), _const_spec((1, 1, d))] + wspec,
        out_specs=xspec,
        compiler_params=_params(2),
        name="ffn_sublayer",
    )(x, mod, mod, mod, gpre, gpost, w_in, w_out)


def _ssd_chunk(r0, qc, seg, s_first, *, xbc_s, dt_s, da_s, y_s, e2_ref, h_in, h_out, consts):
    tu, eye, causal, lane_lo = consts
    rows = pl.ds(r0, qc)
    hi, mid, lo = _split3(da_s[rows, :])
    cs2 = _dot(tu, hi) + _dot(tu, mid) + _dot(tu, lo)
    a_cs = cs2[:qc]
    ea = jnp.exp(a_cs)
    dte = jnp.exp(cs2[qc:])

    stack = jnp.concatenate([dt_s[rows, :], ea, dte], axis=0)
    s_hi = stack.astype(BF16)
    s_lo = (stack - s_hi.astype(F32)).astype(BF16)
    sx = _dot(jnp.concatenate([s_hi, s_lo], axis=1), e2_ref[...])
    dt_x, ea_x, dte_x = sx[:qc], sx[qc:2 * qc], sx[2 * qc:]

    xdt = xbc_s[rows, 0:SSM_D_INNER] * dt_x
    xdt_b = xdt.astype(BF16)
    xd_b = (xdt * dte_x).astype(BF16)
    bm = xbc_s[rows, SSM_D_INNER:SSM_D_INNER + SSM_GN].astype(BF16)
    cm = xbc_s[rows, SSM_D_INNER + SSM_GN:SSM_CONV_DIM].astype(BF16)

    a_hi, a_mid, a_lo = _split3(a_cs)
    a_cs_t = _dot_nt(eye, a_hi) + _dot_nt(eye, a_mid) + _dot_nt(eye, a_lo)

    zero_b = jnp.zeros((), BF16)
    for g in range(SSM_GROUPS):
        gsl = slice(g * SSM_STATE, (g + 1) * SSM_STATE)
        cb = _dot_nt(cm[:, gsl], bm[:, gsl])
        for pr in range(SSM_HPG // 2):
            h0 = g * SSM_HPG + 2 * pr
            psl = slice((h0 // 2) * LANES, (h0 // 2 + 1) * LANES)
            xp = xdt_b[:, psl]
            acc = None
            for half in range(2):
                h = h0 + half
                seg_sum = a_cs[:, h:h + 1] - a_cs_t[h:h + 1, :]
                decay = jnp.exp(jnp.where(causal, seg_sum, -jnp.inf))
                w = (decay * cb).astype(BF16)
                xh = jnp.where(lane_lo if half == 0 else jnp.logical_not(lane_lo), xp, zero_b)
                o = _dot(w, xh)
                acc = o if acc is None else acc + o
            y_s[rows, psl] = acc

    for t in range(qc // seg):
        tr = slice(t * seg, (t + 1) * seg)
        trows = pl.ds(r0 + t * seg, seg)
        last = t * seg + seg - 1
        for g in range(SSM_GROUPS):
            gsl = slice(g * SSM_STATE, (g + 1) * SSM_STATE)
            csl = slice(g * SSM_GROUP_WIDTH, (g + 1) * SSM_GROUP_WIDTH)
            hsl = slice(g * SSM_HPG, (g + 1) * SSM_HPG)
            hg = h_in[s_first + t, hsl].reshape(SSM_GROUP_WIDTH, SSM_STATE)
            y_off = _dot_nt(cm[tr, gsl], hg.astype(BF16)) * ea_x[tr, csl]
            y_s[trows, csl] = y_s[trows, csl] + y_off
            upd = _dot_tn(xd_b[tr, csl], bm[tr, gsl])
            cdec = jnp.concatenate(
                [jnp.broadcast_to(ea[last:last + 1, g * SSM_HPG + r:g * SSM_HPG + r + 1],
                                  (SSM_HEAD_DIM, SSM_STATE)) for r in range(SSM_HPG)], axis=0)
            h_out[s_first + t, hsl] = (hg * cdec + upd).reshape(SSM_HPG, SSM_HEAD_DIM, SSM_STATE)


def _ssm_kernel(*refs, has_state, qc, seg):
    if has_state:
        (x_ref, sh_ref, sc_ref, gt_ref, gpre_ref, gpost_ref, wz_ref, wxbc_ref, wdt_ref, cw_ref, cb_ref,
         dtb_ref, alog_ref, dx_ref, nw_ref, wo_ref, e2_ref, conv_in_ref, h0_ref,
         o_ref, conv_out_ref, h_out_ref, xpad, z_s, xbc_s, dt_s, da_s, y_s) = refs
    else:
        (x_ref, sh_ref, sc_ref, gt_ref, gpre_ref, gpost_ref, wz_ref, wxbc_ref, wdt_ref, cw_ref, cb_ref,
         dtb_ref, alog_ref, dx_ref, nw_ref, wo_ref, e2_ref,
         o_ref, conv_out_ref, h_out_ref, xpad, z_s, xbc_s, dt_s, da_s, y_s) = refs
        conv_in_ref = h0_ref = None
    j = pl.program_id(1)
    x = x_ref[...]
    bt, lt, d = x.shape
    m = bt * lt
    hist = SSM_CONV - 1

    hb = _mod_norm(x, gpre_ref[...], sc_ref[...], sh_ref[...]).reshape(m, d).astype(BF16)
    z_s[...] = _dot(hb, wz_ref[...])
    xpad[:, CONV_PAD:CONV_PAD + lt, :] = _dot(hb, wxbc_ref[...]).reshape(bt, lt, SSM_CONV_DIM)
    dt_raw = _dot(hb, wdt_ref[...]) + dtb_ref[...]
    dt = jnp.maximum(dt_raw, 0.0) + jnp.log1p(jnp.exp(-jnp.abs(dt_raw)))
    dt_s[...] = dt
    da_s[...] = dt * (-jnp.exp(alog_ref[...]))

    if has_state:
        xpad[:, CONV_PAD - hist:CONV_PAD, :] = conv_in_ref[...]
    else:
        @pl.when(j == 0)
        def _():
            xpad[:, CONV_PAD - hist:CONV_PAD, :] = jnp.zeros((bt, hist, SSM_CONV_DIM), F32)
            h_out_ref[...] = jnp.zeros(h_out_ref.shape, F32)
    conv = cb_ref[...]
    for k in range(SSM_CONV):
        conv = conv + xpad[:, CONV_PAD - hist + k:CONV_PAD - hist + k + lt, :] * cw_ref[k:k + 1, :]
    xbc_s[...] = _silu(conv).reshape(m, SSM_CONV_DIM)
    new_hist = xpad[:, CONV_PAD + lt - hist:CONV_PAD + lt, :]
    conv_out_ref[...] = new_hist
    if not has_state:
        xpad[:, CONV_PAD - hist:CONV_PAD, :] = new_hist

    r = lax.broadcasted_iota(jnp.int32, (qc, qc), 0)
    c = lax.broadcasted_iota(jnp.int32, (qc, qc), 1)
    seg_shift = seg.bit_length() - 1
    same = jnp.right_shift(r, seg_shift) == jnp.right_shift(c, seg_shift)
    causal = same & (r >= c)
    tri = jnp.where(causal, 1.0, 0.0)
    upper = jnp.where(same & (c > r), 1.0, 0.0)
    tu = jnp.concatenate([tri, upper], axis=0).astype(BF16)
    er = lax.broadcasted_iota(jnp.int32, (LANES, LANES), 0)
    ec = lax.broadcasted_iota(jnp.int32, (LANES, LANES), 1)
    eye = jnp.where(er == ec, 1.0, 0.0).astype(BF16)
    lane_lo = lax.broadcasted_iota(jnp.int32, (qc, LANES), 1) < HALF
    consts = (tu, eye, causal, lane_lo)

    scr = dict(xbc_s=xbc_s, dt_s=dt_s, da_s=da_s, y_s=y_s, e2_ref=e2_ref, consts=consts)
    if has_state:
        _ssd_chunk(0, qc, seg, 0, h_in=h0_ref, h_out=h_out_ref, **scr)
    else:
        def body(ci, carry):
            _ssd_chunk(pl.multiple_of(ci * qc, qc), qc, seg, 0, h_in=h_out_ref, h_out=h_out_ref, **scr)
            return carry
        lax.fori_loop(0, m // qc, body, 0)

    yg = (y_s[...] + xbc_s[:, 0:SSM_D_INNER] * dx_ref[...]) * _silu(z_s[...])
    parts = []
    for g in range(SSM_GROUPS):
        v = yg[:, g * SSM_GROUP_WIDTH:(g + 1) * SSM_GROUP_WIDTH]
        parts.append(v * lax.rsqrt(jnp.mean(v * v, axis=-1, keepdims=True) + RMS_EPS))
    yn = (jnp.concatenate(parts, axis=1) * nw_ref[...]).astype(BF16)
    out = _dot(yn, wo_ref[...]).reshape(bt, lt, d)
    o_ref[...] = x + gt_ref[...] * _rms(out, gpost_ref[...])


def _ssm(x, mod, gpre, gpost, w, conv_in, h0, bt, lt):
    b, l, d = x.shape
    has_state = h0 is not None
    m = bt * lt
    if has_state:
        qc, seg = m, lt
    else:
        qc, seg = SSM_CHUNK, SSM_CHUNK
    hist = SSM_CONV - 1
    xspec = pl.BlockSpec((bt, lt, d), lambda i, j: (i, j, 0))
    conv_spec = pl.BlockSpec((bt, hist, SSM_CONV_DIM), lambda i, j: (i, 0, 0))
    state_spec = pl.BlockSpec((bt, SSM_HEADS, SSM_HEAD_DIM, SSM_STATE), lambda i, j: (i, 0, 0, 0))
    weights = [w["wz"], w["wxbc"], w["wdt"], w["conv_w"], w["conv_b"], w["dt_bias"], w["a_log"],
               w["d_x"], w["norm_w"], w["wo"], w["e2"]]
    in_specs = ([xspec] + _mod_specs(bt, 1) + [_const_spec((1, 1, d)), _const_spec((1, 1, d))]
                + [_const_spec(a.shape, a.dtype == BF16) for a in weights])
    args = [x, mod, mod, mod, gpre, gpost] + weights
    if has_state:
        in_specs += [conv_spec, state_spec]
        args += [conv_in, h0]
    return pl.pallas_call(
        functools.partial(_ssm_kernel, has_state=has_state, qc=qc, seg=seg),
        out_shape=(jax.ShapeDtypeStruct(x.shape, F32),
                   jax.ShapeDtypeStruct((b, hist, SSM_CONV_DIM), F32),
                   jax.ShapeDtypeStruct((b, SSM_HEADS, SSM_HEAD_DIM, SSM_STATE), F32)),
        grid=(b // bt, l // lt),
        in_specs=in_specs,
        out_specs=(xspec, conv_spec, state_spec),
        scratch_shapes=[
            pltpu.VMEM((bt, CONV_PAD + lt, SSM_CONV_DIM), F32),
            pltpu.VMEM((m, SSM_D_INNER), F32),
            pltpu.VMEM((m, SSM_CONV_DIM), F32),
            pltpu.VMEM((m, LANES), F32),
            pltpu.VMEM((m, LANES), F32),
            pltpu.VMEM((m, SSM_D_INNER), F32),
        ],
        compiler_params=_params(2),
        name="ssm_sublayer_state" if has_state else "ssm_sublayer",
    )(*args)


def _ssm_weights(in_w, conv_w, conv_b, dt_bias, a_log, d_skip, norm_w, out_w):
    pad = LANES - SSM_HEADS
    heads = np.arange(SSM_D_INNER) // SSM_HEAD_DIM
    e = (np.arange(LANES)[:, None] == heads[None, :]).astype(np.float32)
    return dict(
        wz=in_w[:, :SSM_D_INNER].astype(BF16),
        wxbc=in_w[:, SSM_D_INNER:SSM_D_INNER + SSM_CONV_DIM].astype(BF16),
        wdt=jnp.pad(in_w[:, SSM_D_INNER + SSM_CONV_DIM:], ((0, 0), (0, pad))).astype(BF16),
        conv_w=conv_w,
        conv_b=conv_b.reshape(1, SSM_CONV_DIM),
        dt_bias=jnp.pad(dt_bias, (0, pad)).reshape(1, LANES),
        a_log=jnp.pad(a_log, (0, pad)).reshape(1, LANES),
        d_x=jnp.repeat(d_skip, SSM_HEAD_DIM).reshape(1, SSM_D_INNER),
        norm_w=norm_w.reshape(1, SSM_D_INNER),
        wo=out_w.astype(BF16),
        e2=jnp.asarray(np.concatenate([e, e], axis=0), BF16),
    )


SOFTMAX_ROWS = 64


def _softmax_sink(logits, sink):
    mx = jnp.maximum(jnp.max(logits, axis=-1, keepdims=True), sink)
    e = jnp.exp(logits - mx)
    denom = jnp.sum(e, axis=-1, keepdims=True) + jnp.exp(sink - mx)
    return e * (1.0 / denom)


def _dedup(t):
    lane_lo = lax.broadcasted_iota(jnp.int32, (t.shape[0], LANES), 1) < HALF
    tiles = [jnp.where(lane_lo, t[:, (2 * i) * LANES:(2 * i + 1) * LANES],
                       t[:, (2 * i + 1) * LANES:(2 * i + 2) * LANES]) for i in range(ATTN_KV_HEADS // 2)]
    return jnp.concatenate(tiles, axis=1)


def _swa_prompt_kernel(sinks_ref, x_ref, sh_ref, sc_ref, gt_ref, gpre_ref, gpost_ref, wq_ref, wk_ref, wv_ref,
                       bq_ref, bk_ref, bv_ref, bias_ref, wo_ref, bo_ref,
                       o_ref, kc_ref, vc_ref, q_s, kbuf, vbuf, att_s, lg_s, p_s):
    j = pl.program_id(1)
    x = x_ref[0]
    tq = x.shape[0]
    kvw = ATTN_KV_HEADS * LANES

    @pl.when(j == 0)
    def _():
        kbuf[0:WINDOW, :] = jnp.zeros((WINDOW, kvw), BF16)
        vbuf[0:WINDOW, :] = jnp.zeros((WINDOW, kvw), BF16)

    hb = _mod_norm(x, gpre_ref[0], sc_ref[0], sh_ref[0]).astype(BF16)
    q_s[...] = ((_dot(hb, wq_ref[...]) + bq_ref[...]) * (ATTN_HEAD_DIM ** -0.5)).astype(BF16)
    k = _dot(hb, wk_ref[...]) + bk_ref[...]
    v = _dot(hb, wv_ref[...]) + bv_ref[...]
    kbuf[WINDOW:WINDOW + tq, :] = k.astype(BF16)
    vbuf[WINDOW:WINDOW + tq, :] = v.astype(BF16)
    kc_ref[0] = _dedup(k[tq - WINDOW:, :])
    vc_ref[0] = _dedup(v[tq - WINDOW:, :])

    lane_lo = lax.broadcasted_iota(jnp.int32, (2 * WINDOW, LANES), 1) < HALF
    col = lax.broadcasted_iota(jnp.int32, (2 * WINDOW, 2 * WINDOW), 1)
    zero_b = jnp.zeros((), BF16)
    pairs = ATTN_REP // 2

    def block(bi, carry):
        r0 = pl.multiple_of(bi * WINDOW, WINDOW)
        qrows = pl.ds(r0, WINDOW)
        no_prev = jnp.logical_and(j == 0, bi == 0)
        prev_mask = jnp.where(jnp.logical_and(no_prev, col < WINDOW), -jnp.inf, 0.0)
        for g in range(ATTN_KV_HEADS):
            kd = kbuf[pl.ds(r0, 2 * WINDOW), g * LANES:(g + 1) * LANES]
            q2 = jnp.concatenate([q_s[qrows, (g * pairs + pr) * LANES:(g * pairs + pr + 1) * LANES]
                                  for pr in range(pairs)], axis=0)
            for half in range(2):
                kh = jnp.where(lane_lo, kd, zero_b) if half == 0 else jnp.where(lane_lo, zero_b, kd)
                t = 2 * g + half
                lg_s[t] = _dot_nt(q2, kh) + bias_ref[t] + prev_mask
        for t in range(2 * ATTN_KV_HEADS):
            for rc in range(2 * WINDOW // SOFTMAX_ROWS):
                rs = slice(rc * SOFTMAX_ROWS, (rc + 1) * SOFTMAX_ROWS)
                h = 4 * (t // 2) + (t % 2) + 2 * ((rc * SOFTMAX_ROWS) // WINDOW)
                p_s[t, rs, :] = _softmax_sink(lg_s[t, rs, :], sinks_ref[h]).astype(BF16)
        for g in range(ATTN_KV_HEADS):
            vd = vbuf[pl.ds(r0, 2 * WINDOW), g * LANES:(g + 1) * LANES]
            acc = (_dot(p_s[2 * g], jnp.where(lane_lo, vd, zero_b))
                   + _dot(p_s[2 * g + 1], jnp.where(lane_lo, zero_b, vd)))
            for pr in range(pairs):
                pair = g * pairs + pr
                att_s[qrows, pair * LANES:(pair + 1) * LANES] = acc[pr * WINDOW:(pr + 1) * WINDOW].astype(BF16)
        return carry

    lax.fori_loop(0, tq // WINDOW, block, 0)

    out = _dot(att_s[...], wo_ref[...]) + bo_ref[...]
    o_ref[0] = x + gt_ref[0] * _rms(out, gpost_ref[0])
    kbuf[0:WINDOW, :] = kbuf[tq:tq + WINDOW, :]
    vbuf[0:WINDOW, :] = vbuf[tq:tq + WINDOW, :]


def _swa_prompt(x, mod, gpre, gpost, w, bias, tq):
    b, l, d = x.shape
    kvw = ATTN_KV_HEADS * LANES
    pairs = ATTN_REP // 2
    bias = bias.reshape(ATTN_KV_HEADS, pairs, 2, WINDOW, 2 * WINDOW).transpose(0, 2, 1, 3, 4)
    bias = bias.reshape(2 * ATTN_KV_HEADS, pairs * WINDOW, 2 * WINDOW)
    xspec = pl.BlockSpec((1, tq, d), lambda i, j: (i, j, 0))
    cache_spec = pl.BlockSpec((1, WINDOW, ATTN_KV_DIM), lambda i, j: (i, 0, 0))
    consts = [w["wq"], w["wk_dup"], w["wv_dup"], w["bq"], w["bk_dup"], w["bv_dup"], bias, w["wo"], w["bo"]]
    return pl.pallas_call(
        _swa_prompt_kernel,
        out_shape=(jax.ShapeDtypeStruct(x.shape, F32),
                   jax.ShapeDtypeStruct((b, WINDOW, ATTN_KV_DIM), F32),
                   jax.ShapeDtypeStruct((b, WINDOW, ATTN_KV_DIM), F32)),
        grid=(b // 1, l // tq),
        in_specs=[pl.BlockSpec(memory_space=pltpu.SMEM), xspec] + _mod_specs(1, 1)
        + [_const_spec((1, 1, d)), _const_spec((1, 1, d))]
        + [_const_spec(a.shape, a.dtype == BF16) for a in consts],
        out_specs=(xspec, cache_spec, cache_spec),
        scratch_shapes=[
            pltpu.VMEM((tq, ATTN_Q_DIM), BF16),
            pltpu.VMEM((WINDOW + tq, kvw), BF16),
            pltpu.VMEM((WINDOW + tq, kvw), BF16),
            pltpu.VMEM((tq, ATTN_Q_DIM), BF16),
            pltpu.VMEM((2 * ATTN_KV_HEADS, pairs * WINDOW, 2 * WINDOW), F32),
            pltpu.VMEM((2 * ATTN_KV_HEADS, pairs * WINDOW, 2 * WINDOW), BF16),
        ],
        compiler_params=_params(2),
        name="swa_prompt_sublayer",
    )(w["sinks"], x, mod, mod, mod, gpre, gpost, *consts)


def _swa_sample_kernel(x_ref, sh_ref, sc_ref, gt_ref, gpre_ref, gpost_ref, wq_ref, wk_ref, wv_ref,
                       bq_ref, bk_ref, bv_ref, ck_ref, cv_ref, bias_ref, sink_ref, wo_ref, bo_ref,
                       o_ref, ko_ref, vo_ref, q_s, kn_s, vn_s, kbuf, vbuf, att_s):
    x = x_ref[...]
    bt, lt, d = x.shape
    m = bt * lt
    nbuf = ck_ref.shape[1]
    wide = ATTN_KV_HEADS * ATTN_HEAD_DIM

    hb = _mod_norm(x, gpre_ref[...], sc_ref[...], sh_ref[...]).reshape(m, d).astype(BF16)
    q_s[...] = (_dot(hb, wq_ref[...]) + bq_ref[...]) * (ATTN_HEAD_DIM ** -0.5)
    kn_s[...] = _dot(hb, wk_ref[...]) + bk_ref[...]
    vn_s[...] = _dot(hb, wv_ref[...]) + bv_ref[...]
    pad_rows = kbuf.shape[0] - nbuf - lt
    kbuf[nbuf + lt:, :] = jnp.zeros((pad_rows, ATTN_KV_DIM), F32)
    vbuf[nbuf + lt:, :] = jnp.zeros((pad_rows, ATTN_KV_DIM), F32)

    def seq(bi, carry):
        rows = pl.ds(pl.multiple_of(bi * lt, lt), lt)
        kn = kn_s[rows, :]
        vn = vn_s[rows, :]
        ck = ck_ref[bi]
        cv = cv_ref[bi]
        kbuf[0:nbuf, :] = ck
        vbuf[0:nbuf, :] = cv
        kbuf[nbuf:nbuf + lt, :] = kn
        vbuf[nbuf:nbuf + lt, :] = vn
        ko_ref[bi, 0:nbuf - lt, :] = ck[lt:, :]
        vo_ref[bi, 0:nbuf - lt, :] = cv[lt:, :]
        ko_ref[bi, nbuf - lt:nbuf, :] = kn
        vo_ref[bi, nbuf - lt:nbuf, :] = vn
        qbig = jnp.concatenate([q_s[rows, h * wide:(h + 1) * wide] for h in range(ATTN_HEADS)], axis=0)
        logits = _dot_nt(qbig.astype(BF16), kbuf[...].astype(BF16)) + bias_ref[...]
        p = _softmax_sink(logits, sink_ref[:, 0:1]).astype(BF16)
        res = _dot(p, vbuf[...].astype(BF16))
        att_s[rows, :] = jnp.concatenate([res[h * lt:(h + 1) * lt, :] for h in range(ATTN_HEADS)], axis=1)
        return carry

    lax.fori_loop(0, bt, seq, 0)

    out = (_dot(att_s[...].astype(BF16), wo_ref[...]) + bo_ref[...]).reshape(bt, lt, d)
    o_ref[...] = x + gt_ref[...] * _rms(out, gpost_ref[...])


def _swa_sample(x, mod, gpre, gpost, w, bias, cache_k, cache_v, bt):
    b, lt, d = x.shape
    nbuf = cache_k.shape[1]
    m = bt * lt
    wide = ATTN_KV_HEADS * ATTN_HEAD_DIM
    big = ATTN_HEADS * wide
    keys = 2 * WINDOW
    xspec = pl.BlockSpec((bt, lt, d), lambda i: (i, 0, 0))
    cache_spec = pl.BlockSpec((bt, nbuf, ATTN_KV_DIM), lambda i: (i, 0, 0))
    bias_s = bias[:, :lt, :].reshape(ATTN_HEADS * lt, keys)
    sink_col = jnp.broadcast_to(jnp.repeat(w["sinks"], lt)[:, None], (ATTN_HEADS * lt, LANES))
    mods = [pl.BlockSpec((bt, None, 1, D_MODEL), functools.partial(lambda k, i: (i, k, 0, 0), 3 + k))
            for k in range(3)]
    pre = [w["wq_big"], w["wk"], w["wv"], w["bq_big"], w["bk"], w["bv"]]
    post = [bias_s, sink_col, w["wo_big"], w["bo"]]
    return pl.pallas_call(
        _swa_sample_kernel,
        out_shape=(jax.ShapeDtypeStruct(x.shape, F32),
                   jax.ShapeDtypeStruct(cache_k.shape, F32),
                   jax.ShapeDtypeStruct(cache_v.shape, F32)),
        grid=(b // bt,),
        in_specs=[xspec] + mods + [_const_spec((1, 1, d)), _const_spec((1, 1, d))]
        + [_const_spec(a.shape, a.dtype == BF16) for a in pre] + [cache_spec, cache_spec]
        + [_const_spec(a.shape, a.dtype == BF16) for a in post],
        out_specs=(xspec, cache_spec, cache_spec),
        scratch_shapes=[
            pltpu.VMEM((m, big), F32),
            pltpu.VMEM((m, ATTN_KV_DIM), F32),
            pltpu.VMEM((m, ATTN_KV_DIM), F32),
            pltpu.VMEM((keys, ATTN_KV_DIM), F32),
            pltpu.VMEM((keys, ATTN_KV_DIM), F32),
            pltpu.VMEM((m, big), F32),
        ],
        compiler_params=_params(1),
        name="swa_sample_sublayer",
    )(x, mod, mod, mod, gpre, gpost, *pre, cache_k, cache_v, *post)


def _swa_weights(qkv_w, qkv_b, sinks, o_w, o_b):
    d = qkv_w.shape[0]
    wq = qkv_w[:, :ATTN_Q_DIM]
    wk = qkv_w[:, ATTN_Q_DIM:ATTN_Q_DIM + ATTN_KV_DIM]
    wv = qkv_w[:, ATTN_Q_DIM + ATTN_KV_DIM:]
    bq = qkv_b[:ATTN_Q_DIM]
    bk = qkv_b[ATTN_Q_DIM:ATTN_Q_DIM + ATTN_KV_DIM]
    bv = qkv_b[ATTN_Q_DIM + ATTN_KV_DIM:]

    def dup(a):
        a4 = a.reshape(a.shape[:-1] + (ATTN_KV_HEADS, 1, ATTN_HEAD_DIM))
        a4 = jnp.broadcast_to(a4, a.shape[:-1] + (ATTN_KV_HEADS, 2, ATTN_HEAD_DIM))
        return a4.reshape(a.shape[:-1] + (ATTN_KV_HEADS * LANES,))

    own = jnp.asarray((np.arange(ATTN_HEADS)[:, None] // ATTN_REP == np.arange(ATTN_KV_HEADS)[None, :])
                      .astype(np.float32))
    wq_big = (wq.reshape(d, ATTN_HEADS, 1, ATTN_HEAD_DIM) * own[None, :, :, None]).reshape(d, -1)
    bq_big = (bq.reshape(ATTN_HEADS, 1, ATTN_HEAD_DIM) * own[:, :, None]).reshape(1, -1)
    wo_big = (o_w.reshape(ATTN_HEADS, 1, ATTN_HEAD_DIM, d) * own[:, :, None, None]).reshape(-1, d)
    return dict(
        sinks=sinks,
        wq=wq.astype(BF16), wk_dup=dup(wk).astype(BF16), wv_dup=dup(wv).astype(BF16),
        bq=bq.reshape(1, -1), bk_dup=dup(bk).reshape(1, -1), bv_dup=dup(bv).reshape(1, -1),
        wo=o_w.astype(BF16), bo=o_b.reshape(1, d),
        wq_big=wq_big.astype(BF16), bq_big=bq_big, wk=wk.astype(BF16), wv=wv.astype(BF16),
        bk=bk.reshape(1, -1), bv=bv.reshape(1, -1), wo_big=wo_big.astype(BF16),
    )


def _trunk(x, mod_all, state_ssm, state_conv, cache_k, cache_v, norm_pre, norm_post, ffn_in, ffn_out,
           ssm_w, swa_w, bias, tiles):
    bt, lt, n_sub = tiles["ffn"]
    b, l, d = x.shape
    sample = state_ssm is not None
    outs = {}
    for i in range(2):
        mod = mod_all[i].reshape(b, N_SUB * 3, 1, d)
        gpre = norm_pre[i].reshape(N_SUB, 1, 1, d)
        gpost = norm_post[i].reshape(N_SUB, 1, 1, d)
        x = _ffn(x, mod, 0, gpre[0], gpost[0], ffn_in, ffn_out, i, 0, bt, lt, n_sub)
        if i == 0:
            sbt, slt = tiles["ssm"]
            x, conv_new, ssm_new = _ssm(x, mod, gpre[1], gpost[1], ssm_w,
                                        state_conv[0] if sample else None,
                                        state_ssm[0] if sample else None, sbt, slt)
            outs["conv"] = conv_new[None]
            outs["ssm"] = ssm_new[None]
        else:
            if sample:
                x, k_new, v_new = _swa_sample(x, mod, gpre[1], gpost[1], swa_w, bias,
                                              cache_k[0].reshape(b, -1, ATTN_KV_DIM),
                                              cache_v[0].reshape(b, -1, ATTN_KV_DIM), tiles["swa"])
            else:
                x, k_new, v_new = _swa_prompt(x, mod, gpre[1], gpost[1], swa_w, bias, tiles["swa"])
            shape = (1, b, -1, ATTN_KV_HEADS, ATTN_HEAD_DIM)
            outs["k"] = k_new.reshape(shape)
            outs["v"] = v_new.reshape(shape)
        x = _ffn(x, mod, 2, gpre[2], gpost[2], ffn_in, ffn_out, i, 1, bt, lt, n_sub)
    return x, outs["ssm"], outs["conv"], outs["k"], outs["v"]


def kernel(x_prompt, x_sample, state_ssm, state_conv, cache_k, cache_v, c_prompt, c_sample, ada_w, ada_b, norm_pre, norm_post, ffn_w_in, ffn_w_out, ssm_in_w, ssm_conv_w, ssm_conv_b, ssm_dt_bias, ssm_a_log, ssm_d, ssm_norm_w, ssm_out_w, attn_qkv_w, attn_qkv_b, attn_sinks, attn_o_w, attn_o_b, rel_bias):
    nb = x_prompt.shape[0]
    mod_all = _ada(jnp.concatenate([c_prompt, c_sample], axis=0), ada_w, ada_b)
    bias = _bias_table(rel_bias)
    ffn_in = ffn_w_in.astype(BF16)
    ffn_out = ffn_w_out.astype(BF16)
    ssm_w = _ssm_weights(ssm_in_w[0], ssm_conv_w[0], ssm_conv_b[0], ssm_dt_bias[0], ssm_a_log[0], ssm_d[0],
                         ssm_norm_w[0], ssm_out_w[0])
    swa_w = _swa_weights(attn_qkv_w[0], attn_qkv_b[0], attn_sinks[0], attn_o_w[0], attn_o_b[0])
    common = (norm_pre, norm_post, ffn_in, ffn_out, ssm_w, swa_w, bias)

    p_tiles = dict(ffn=(1, 1024, 2), ssm=(1, 256), swa=512)
    y_p, ssm_p, conv_p, k_p, v_p = _trunk(x_prompt, mod_all[:, :nb], None, None, None, None, *common, p_tiles)
    ns, ls = x_sample.shape[:2]
    s_tiles = dict(ffn=(min(64, ns), ls, 1), ssm=(min(4, ns), ls), swa=min(16, ns))
    y_s, ssm_s, conv_s, k_s, v_s = _trunk(x_sample, mod_all[:, nb:], state_ssm, state_conv, cache_k, cache_v,
                                          *common, s_tiles)
    return (y_p, y_s, ssm_p, conv_p, k_p, v_p, ssm_s, conv_s, k_s, v_s)
```

```python
import functools
import math

import numpy as np
import jax
import jax.numpy as jnp
from jax import lax
from jax.experimental import pallas as pl
from jax.experimental.pallas import tpu as pltpu

F32 = jnp.float32
BF16 = jnp.bfloat16

D_MODEL = 1024
N_SUB = 3
RMS_EPS = 1e-6
FFN_RES = 0.5
D_FF = 2816

SSM_D_INNER = 2048
SSM_HEAD_DIM = 64
SSM_HEADS = 32
SSM_GROUPS = 4
SSM_HPG = 8
SSM_STATE = 128
SSM_CONV = 4
SSM_CHUNK = 128
SSM_GN = SSM_GROUPS * SSM_STATE
SSM_CONV_DIM = SSM_D_INNER + 2 * SSM_GN
SSM_GROUP_WIDTH = SSM_HPG * SSM_HEAD_DIM

ATTN_HEAD_DIM = 64
ATTN_HEADS = 16
ATTN_KV_HEADS = 4
ATTN_REP = 4
WINDOW = 128
REL_BUCKETS = 32
ATTN_Q_DIM = ATTN_HEADS * ATTN_HEAD_DIM
ATTN_KV_DIM = ATTN_KV_HEADS * ATTN_HEAD_DIM

LANES = 128
SUBLANES = 8
HALF = LANES // 2
VMEM_LIMIT_BYTES = 56 * 1024 * 1024
CONV_PAD = SUBLANES


def _dot(a, b):
    return jnp.dot(a, b, preferred_element_type=F32)


def _dot_nt(a, b):
    return lax.dot_general(a, b, (((1,), (1,)), ((), ())), preferred_element_type=F32)


def _dot_tn(a, b):
    return lax.dot_general(a, b, (((0,), (0,)), ((), ())), preferred_element_type=F32)


def _silu(x):
    h = 0.5 * x
    return h * jnp.tanh(h) + h


def _rms(x, g):
    return x * lax.rsqrt(jnp.mean(x * x, axis=-1, keepdims=True) + RMS_EPS) * g


def _mod_norm(x, g, scale, shift):
    return _rms(x, g) * (1.0 + scale) + shift


def _split3(x):
    hi = x.astype(BF16)
    r1 = x - hi.astype(F32)
    mid = r1.astype(BF16)
    lo = (r1 - mid.astype(F32)).astype(BF16)
    return hi, mid, lo


def _const_spec(shape, single_buffer=False):
    nd = len(shape)
    kw = {"pipeline_mode": pl.Buffered(1)} if single_buffer else {}
    return pl.BlockSpec(shape, lambda *_: (0,) * nd, **kw)


def _params(n_grid):
    return pltpu.CompilerParams(
        dimension_semantics=("arbitrary",) * n_grid,
        vmem_limit_bytes=VMEM_LIMIT_BYTES,
    )


def _ada_kernel(c_ref, w_ref, b_ref, o_ref):
    cs = _silu(c_ref[...]).astype(BF16)
    o_ref[0] = _dot(cs, w_ref[0].astype(BF16)) + b_ref[0]


def _ada(c_all, ada_w, ada_b, tn=1152):
    depth, d, n = ada_w.shape
    bc = c_all.shape[0]
    return pl.pallas_call(
        _ada_kernel,
        out_shape=jax.ShapeDtypeStruct((depth, bc, n), F32),
        grid=(depth, n // tn),
        in_specs=[
            pl.BlockSpec((bc, d), lambda l, j: (0, 0)),
            pl.BlockSpec((1, d, tn), lambda l, j: (l, 0, j)),
            pl.BlockSpec((1, 1, tn), lambda l, j: (l, 0, j)),
        ],
        out_specs=pl.BlockSpec((1, bc, tn), lambda l, j: (l, 0, j)),
        compiler_params=_params(2),
        name="ada_mod",
    )(c_all, ada_w, ada_b.reshape(depth, 1, n))


def _t5_bucket_table():
    i = np.arange(WINDOW)[:, None]
    j = np.arange(2 * WINDOW)[None, :]
    dist = i + WINDOW - j
    exact = REL_BUCKETS // 2
    df = np.maximum(dist, 1).astype(np.float32)
    large = exact + (np.log(df / np.float32(exact)) / np.float32(math.log(WINDOW / exact))
                     * np.float32(REL_BUCKETS - exact)).astype(np.int32)
    large = np.minimum(large, REL_BUCKETS - 1)
    bucket = np.where(dist < exact, dist, large)
    valid = (dist >= 0) & (dist <= WINDOW)
    return np.where(valid, bucket, -1).astype(np.int32)


def _bias_kernel(rb_ref, idx_ref, o_ref):
    h = pl.program_id(0)
    idx = idx_ref[...]
    acc = jnp.full(idx.shape, -jnp.inf, F32)
    for b in range(REL_BUCKETS):
        acc = jnp.where(idx == b, rb_ref[b, h], acc)
    o_ref[0] = acc


def _bias_table(rel_bias):
    idx = jnp.asarray(_t5_bucket_table())
    return pl.pallas_call(
        _bias_kernel,
        out_shape=jax.ShapeDtypeStruct((ATTN_HEADS, WINDOW, 2 * WINDOW), F32),
        grid=(ATTN_HEADS,),
        in_specs=[
            pl.BlockSpec(memory_space=pltpu.SMEM),
            pl.BlockSpec((WINDOW, 2 * WINDOW), lambda h: (0, 0)),
        ],
        out_specs=pl.BlockSpec((1, WINDOW, 2 * WINDOW), lambda h: (h, 0, 0)),
        compiler_params=_params(1),
        name="rel_bias_table",
    )(rel_bias, idx)


MXU_TILE = 256
FF_CHUNKS = ((0, 6 * MXU_TILE), (6 * MXU_TILE, D_FF))


def _ffn_kernel(x_ref, sh_ref, sc_ref, gt_ref, gpre_ref, gpost_ref, win_ref, wout_ref, o_ref, *, ff_chunks, n_sub):
    bt, lt, d = x_ref.shape

    def sub_slices(s):
        if bt == 1:
            return slice(None), slice(s * (lt // n_sub), (s + 1) * (lt // n_sub))
        return slice(s * (bt // n_sub), (s + 1) * (bt // n_sub)), slice(None)

    def pre(s):
        bs, ls = sub_slices(s)
        x = x_ref[bs, ls, :]
        h = _mod_norm(x, gpre_ref[...], sc_ref[bs], sh_ref[bs])
        return h.reshape(x.shape[0] * x.shape[1], d).astype(BF16)

    def post(s, acc):
        bs, ls = sub_slices(s)
        x = x_ref[bs, ls, :]
        o_ref[bs, ls, :] = x + FFN_RES * gt_ref[bs] * _rms(acc.reshape(x.shape), gpost_ref[...])

    hb = pre(0)
    prev = None
    for s in range(n_sub):
        acc = None
        hb_next = None
        for c, (lo, hi) in enumerate(ff_chunks):
            g = _dot(hb, win_ref[:, lo:hi])
            u = _dot(hb, win_ref[:, D_FF + lo:D_FF + hi])
            a = (_silu(g) * u).astype(BF16)
            part = _dot(a, wout_ref[lo:hi, :])
            acc = part if acc is None else acc + part
            if c == 0:
                if prev is not None:
                    post(s - 1, prev)
                if s + 1 < n_sub:
                    hb_next = pre(s + 1)
        prev = acc
        hb = hb_next
    post(n_sub - 1, prev)


def _mod_specs(bt, sub):
    return [pl.BlockSpec((bt, None, 1, D_MODEL), functools.partial(lambda k, i, j: (i, k, 0, 0), sub * 3 + k))
            for k in range(3)]


def _ffn(x, mod, sub, gpre, gpost, w_in, w_out, layer, which, bt, lt, n_sub):
    b, l, d = x.shape
    xspec = pl.BlockSpec((bt, lt, d), lambda i, j: (i, j, 0))
    wspec = [pl.BlockSpec((None, None) + w.shape[2:], lambda i, j: (layer, which, 0, 0),
                          pipeline_mode=pl.Buffered(1)) for w in (w_in, w_out)]
    return pl.pallas_call(
        functools.partial(_ffn_kernel, ff_chunks=FF_CHUNKS, n_sub=n_sub),
        out_shape=jax.ShapeDtypeStruct(x.shape, F32),
        grid=(b // bt, l // lt),
        in_specs=[xspec] + _mod_specs(bt, sub) + [
            _const_spec((1, 1, d)), _const_spec((1, 1, d))] + wspec,
        out_specs=xspec,
        compiler_params=_params(2),
        name="ffn_sublayer",
    )(x, mod, mod, mod, gpre, gpost, w_in, w_out)


def _chunk_consts(qc, seg):
    r = lax.broadcasted_iota(jnp.int32, (qc, qc), 0)
    c = lax.broadcasted_iota(jnp.int32, (qc, qc), 1)
    seg_shift = seg.bit_length() - 1
    same = jnp.right_shift(r, seg_shift) == jnp.right_shift(c, seg_shift)
    causal = same & (r >= c)
    tri = jnp.where(causal, 1.0, 0.0)
    upper = jnp.where(same & (c > r), 1.0, 0.0)
    tu = jnp.concatenate([tri, upper], axis=0).astype(BF16)
    er = lax.broadcasted_iota(jnp.int32, (LANES, LANES), 0)
    ec = lax.broadcasted_iota(jnp.int32, (LANES, LANES), 1)
    eye = jnp.where(er == ec, 1.0, 0.0).astype(BF16)
    lane_lo = lax.broadcasted_iota(jnp.int32, (qc, LANES), 1) < HALF
    return tu, eye, causal, lane_lo


def _ssm_project(hb, wz_ref, wxbc_ref, wdt_ref, dtb_ref, alog_ref):
    z = _dot(hb, wz_ref[...])
    xbc_raw = _dot(hb, wxbc_ref[...])
    dt_raw = _dot(hb, wdt_ref[...]) + dtb_ref[...]
    dt = jnp.maximum(dt_raw, 0.0) + jnp.log1p(jnp.exp(-jnp.abs(dt_raw)))
    return z, xbc_raw, dt, dt * (-jnp.exp(alog_ref[...]))


def _gate_norm(y, xs, z, dx, nw):
    yg = (y + xs * dx) * _silu(z)
    parts = []
    for g in range(SSM_GROUPS):
        v = yg[:, g * SSM_GROUP_WIDTH:(g + 1) * SSM_GROUP_WIDTH]
        parts.append(v * lax.rsqrt(jnp.mean(v * v, axis=-1, keepdims=True) + RMS_EPS))
    return (jnp.concatenate(parts, axis=1) * nw).astype(BF16)


def _cols(ref, rows, c0, c1):
    if len(ref.shape) == 2:
        return ref[rows, c0:c1]
    return jnp.concatenate([ref[c, rows, :] for c in range(c0 // LANES, c1 // LANES)], axis=1)


def _ssd_chunk(r0, qc, seg, s_first, *, xbc_s, dt_s, da_s, y_s, e2_ref, h_in, h_out, consts, reset=None):
    tu, eye, causal, lane_lo = consts
    rows = pl.ds(r0, qc)
    hi, mid, lo = _split3(da_s[rows, :])
    cs2 = _dot(tu, hi) + _dot(tu, mid) + _dot(tu, lo)
    a_cs = cs2[:qc]
    ea = jnp.exp(a_cs)
    dte = jnp.exp(cs2[qc:])

    stack = jnp.concatenate([dt_s[rows, :], ea, dte], axis=0)
    s_hi = stack.astype(BF16)
    s_lo = (stack - s_hi.astype(F32)).astype(BF16)
    sx = _dot(jnp.concatenate([s_hi, s_lo], axis=1), e2_ref[...])
    dt_x, ea_x, dte_x = sx[:qc], sx[qc:2 * qc], sx[2 * qc:]

    xdt = _cols(xbc_s, rows, 0, SSM_D_INNER) * dt_x
    xdt_b = xdt.astype(BF16)
    xd_b = (xdt * dte_x).astype(BF16)
    bm = _cols(xbc_s, rows, SSM_D_INNER, SSM_D_INNER + SSM_GN).astype(BF16)
    cm = _cols(xbc_s, rows, SSM_D_INNER + SSM_GN, SSM_CONV_DIM).astype(BF16)

    a_hi, a_mid, a_lo = _split3(a_cs)
    a_cs_t = _dot_nt(eye, a_hi) + _dot_nt(eye, a_mid) + _dot_nt(eye, a_lo)

    zero_b = jnp.zeros((), BF16)
    for g in range(SSM_GROUPS):
        gsl = slice(g * SSM_STATE, (g + 1) * SSM_STATE)
        cb = _dot_nt(cm[:, gsl], bm[:, gsl])
        for pr in range(SSM_HPG // 2):
            h0 = g * SSM_HPG + 2 * pr
            psl = slice((h0 // 2) * LANES, (h0 // 2 + 1) * LANES)
            xp = xdt_b[:, psl]
            acc = None
            for half in range(2):
                h = h0 + half
                seg_sum = a_cs[:, h:h + 1] - a_cs_t[h:h + 1, :]
                decay = jnp.exp(jnp.where(causal, seg_sum, -jnp.inf))
                w = (decay * cb).astype(BF16)
                xh = jnp.where(lane_lo if half == 0 else jnp.logical_not(lane_lo), xp, zero_b)
                o = _dot(w, xh)
                acc = o if acc is None else acc + o
            y_s[rows, psl] = acc

    for t in range(qc // seg):
        tr = slice(t * seg, (t + 1) * seg)
        trows = pl.ds(r0 + t * seg, seg)
        last = t * seg + seg - 1
        for g in range(SSM_GROUPS):
            gsl = slice(g * SSM_STATE, (g + 1) * SSM_STATE)
            csl = slice(g * SSM_GROUP_WIDTH, (g + 1) * SSM_GROUP_WIDTH)
            hsl = slice(g * SSM_HPG, (g + 1) * SSM_HPG)
            hg = h_in[s_first + t, hsl].reshape(SSM_GROUP_WIDTH, SSM_STATE)
            if reset is not None:
                hg = jnp.where(reset, 0.0, hg)
            y_off = _dot_nt(cm[tr, gsl], hg.astype(BF16)) * ea_x[tr, csl]
            y_s[trows, csl] = y_s[trows, csl] + y_off
            upd = _dot_tn(xd_b[tr, csl], bm[tr, gsl])
            cdec = jnp.concatenate(
                [jnp.broadcast_to(ea[last:last + 1, g * SSM_HPG + r:g * SSM_HPG + r + 1],
                                  (SSM_HEAD_DIM, SSM_STATE)) for r in range(SSM_HPG)], axis=0)
            h_out[s_first + t, hsl] = (hg * cdec + upd).reshape(SSM_HPG, SSM_HEAD_DIM, SSM_STATE)


HIST = SSM_CONV - 1
_SSM_WEIGHT_NAMES = ("wz", "wxbc", "wdt", "conv_w", "conv_b", "dt_bias", "a_log", "d_x", "norm_w", "wo", "e2")
_STATE_SHAPE = (SSM_HEADS, SSM_HEAD_DIM, SSM_STATE)


def _conv_silu(xpad, cw_ref, cb_ref, lt):
    conv = cb_ref[...]
    for k in range(SSM_CONV):
        conv = conv + xpad[:, CONV_PAD - HIST + k:CONV_PAD - HIST + k + lt, :] * cw_ref[k:k + 1, :]
    return _silu(conv)


def _ssm_sample_kernel(x_ref, sh_ref, sc_ref, gt_ref, gpre_ref, gpost_ref, wz_ref, wxbc_ref, wdt_ref, cw_ref,
                       cb_ref, dtb_ref, alog_ref, dx_ref, nw_ref, wo_ref, e2_ref, conv_in_ref, h0_ref,
                       o_ref, conv_out_ref, h_out_ref, xpad, xbc_s, dt_s, da_s, y_s):
    x = x_ref[...]
    bt, lt, d = x.shape
    m = bt * lt
    hb = _mod_norm(x, gpre_ref[...], sc_ref[...], sh_ref[...]).reshape(m, d).astype(BF16)
    z, xbc_raw, dt, da = _ssm_project(hb, wz_ref, wxbc_ref, wdt_ref, dtb_ref, alog_ref)
    dt_s[...] = dt
    da_s[...] = da
    xpad[:, CONV_PAD:CONV_PAD + lt, :] = xbc_raw.reshape(bt, lt, SSM_CONV_DIM)
    xpad[:, CONV_PAD - HIST:CONV_PAD, :] = conv_in_ref[...]
    xbc_s[...] = _conv_silu(xpad, cw_ref, cb_ref, lt).reshape(m, SSM_CONV_DIM)
    conv_out_ref[...] = xpad[:, CONV_PAD + lt - HIST:CONV_PAD + lt, :]
    _ssd_chunk(0, m, lt, 0, xbc_s=xbc_s, dt_s=dt_s, da_s=da_s, y_s=y_s, e2_ref=e2_ref,
               h_in=h0_ref, h_out=h_out_ref, consts=_chunk_consts(m, lt))
    yn = _gate_norm(y_s[...], xbc_s[:, 0:SSM_D_INNER], z, dx_ref[...], nw_ref[...])
    out = _dot(yn, wo_ref[...]).reshape(bt, lt, d)
    o_ref[...] = x + gt_ref[...] * _rms(out, gpost_ref[...])


def _ssm_sample(x, mod, gpre, gpost, w, conv_in, h0, bt):
    b, lt, d = x.shape
    m = bt * lt
    xspec = pl.BlockSpec((bt, lt, d), lambda i: (i, 0, 0))
    conv_spec = pl.BlockSpec((bt, HIST, SSM_CONV_DIM), lambda i: (i, 0, 0))
    state_spec = pl.BlockSpec((bt,) + _STATE_SHAPE, lambda i: (i, 0, 0, 0))
    mods = [pl.BlockSpec((bt, None, 1, d), functools.partial(lambda k, i: (i, k, 0, 0), 3 + k)) for k in range(3)]
    weights = [w[n] for n in _SSM_WEIGHT_NAMES]
    return pl.pallas_call(
        _ssm_sample_kernel,
        out_shape=(jax.ShapeDtypeStruct(x.shape, F32),
                   jax.ShapeDtypeStruct((b, HIST, SSM_CONV_DIM), F32),
                   jax.ShapeDtypeStruct((b,) + _STATE_SHAPE, F32)),
        grid=(b // bt,),
        in_specs=[xspec] + mods + [_const_spec((1, 1, d)), _const_spec((1, 1, d))]
        + [_const_spec(a.shape, a.dtype == BF16) for a in weights] + [conv_spec, state_spec],
        out_specs=(xspec, conv_spec, state_spec),
        scratch_shapes=[
            pltpu.VMEM((bt, CONV_PAD + lt, SSM_CONV_DIM), F32),
            pltpu.VMEM((m, SSM_CONV_DIM), F32),
            pltpu.VMEM((m, LANES), F32),
            pltpu.VMEM((m, LANES), F32),
            pltpu.VMEM((m, SSM_D_INNER), F32),
        ],
        compiler_params=_params(1),
        name="ssm_sublayer_state",
    )(x, mod, mod, mod, gpre, gpost, *weights, conv_in, h0)


def _ssm_prompt_kernel(xa_ref, xc_ref, sh_ref, sc_ref, gt_ref, gpre_ref, gpost_ref, wz_ref, wxbc_ref, wdt_ref,
                       cw_ref, cb_ref, dtb_ref, alog_ref, dx_ref, nw_ref, wo_ref, e2_ref,
                       o_ref, conv_out_ref, h_out_ref,
                       xpad, z_s, xbc_s, dt_s, da_s, zc_s, xbcc_s, dtc_s, dac_s, y_s, yn_a, yn_b, h_s,
                       *, steps_per_seq):
    s = pl.program_id(0)
    q = SSM_CHUNK
    lt = 2 * q
    seq_start = (s % steps_per_seq) == 0

    @pl.when(s == 0)
    def _():
        for ref in (z_s, xbc_s, dt_s, da_s, yn_a, h_s):
            ref[...] = jnp.zeros(ref.shape, ref.dtype)

    n_slabs = SSM_CONV_DIM // LANES

    @pl.when(seq_start)
    def _():
        xpad[:, CONV_PAD - HIST:CONV_PAD, :] = jnp.zeros((n_slabs, HIST, LANES), F32)

    zc_s[...] = z_s[q:lt, :]
    xbcc_s[...] = xbc_s[:, q:lt, :]
    dtc_s[...] = dt_s[q:lt, :]
    dac_s[...] = da_s[q:lt, :]
    consts = _chunk_consts(q, q)

    def finish(yn_ref, rows):
        out = _dot(yn_ref[...], wo_ref[...])
        o_ref[0, rows, :] = xc_ref[0, rows, :] + gt_ref[0] * _rms(out, gpost_ref[0])

    def scan(xbc_v, z_v, dt_v, da_v, yn_ref, reset):
        _ssd_chunk(0, q, q, 0, xbc_s=xbc_v, dt_s=dt_v, da_s=da_v, y_s=y_s, e2_ref=e2_ref,
                   h_in=h_s, h_out=h_s, consts=consts, reset=reset)
        xs = _cols(xbc_v, slice(None), 0, SSM_D_INNER)
        yn_ref[...] = _gate_norm(y_s[...], xs, z_v[...], dx_ref[...], nw_ref[...])

    hb = _mod_norm(xa_ref[0], gpre_ref[0], sc_ref[0], sh_ref[0]).astype(BF16)
    z, xbc_raw, dt, da = _ssm_project(hb, wz_ref, wxbc_ref, wdt_ref, dtb_ref, alog_ref)
    z_s[...] = z
    dt_s[...] = dt
    da_s[...] = da
    for c in range(n_slabs):
        xpad[c, CONV_PAD:CONV_PAD + lt, :] = xbc_raw[:, c * LANES:(c + 1) * LANES]

    finish(yn_a, slice(0, q))
    scan(xbcc_s, zc_s, dtc_s, dac_s, yn_b, None)
    h_out_ref[...] = h_s[...]

    half = lt // 2
    for c in range(n_slabs):
        csl = slice(c * LANES, (c + 1) * LANES)
        for par in range(2):
            acc = cb_ref[:, csl]
            for k in range(SSM_CONV):
                rows = pl.ds(CONV_PAD - HIST + par + k, half, stride=2)
                acc = acc + xpad[c, rows, :] * cw_ref[k:k + 1, csl]
            xbc_s[c, pl.ds(par, half, stride=2), :] = _silu(acc)
        new_hist = xpad[c, CONV_PAD + lt - HIST:CONV_PAD + lt, :]
        conv_out_ref[0, :, csl] = new_hist
        xpad[c, CONV_PAD - HIST:CONV_PAD, :] = new_hist

    scan(xbc_s.at[:, 0:q, :], z_s.at[0:q], dt_s.at[0:q], da_s.at[0:q], yn_a, seq_start)
    finish(yn_b, slice(q, lt))


def _ssm_prompt(x, mod, gpre, gpost, w):
    b, l, d = x.shape
    q = SSM_CHUNK
    lt = 2 * q
    spq = l // lt
    n_blocks = b * spq

    def cur(s):
        blk = jnp.minimum(s, n_blocks - 1)
        return blk // spq, blk % spq

    def prev(s):
        blk = jnp.maximum(s - 1, 0)
        return blk // spq, blk % spq

    xa_spec = pl.BlockSpec((1, lt, d), lambda s: cur(s) + (0,))
    xc_spec = pl.BlockSpec((1, lt, d), lambda s: prev(s) + (0,))
    mod_a = [pl.BlockSpec((1, None, 1, d), functools.partial(lambda k, s: (cur(s)[0], k, 0, 0), 3 + k))
             for k in range(2)]
    mod_c = pl.BlockSpec((1, None, 1, d), lambda s: (prev(s)[0], 5, 0, 0))
    conv_spec = pl.BlockSpec((1, HIST, SSM_CONV_DIM), lambda s: (cur(s)[0], 0, 0))
    state_spec = pl.BlockSpec((1,) + _STATE_SHAPE, lambda s: (prev(s)[0], 0, 0, 0))
    weights = [w[n] for n in _SSM_WEIGHT_NAMES]
    return pl.pallas_call(
        functools.partial(_ssm_prompt_kernel, steps_per_seq=spq),
        out_shape=(jax.ShapeDtypeStruct(x.shape, F32),
                   jax.ShapeDtypeStruct((b, HIST, SSM_CONV_DIM), F32),
                   jax.ShapeDtypeStruct((b,) + _STATE_SHAPE, F32)),
        grid=(n_blocks + 1,),
        in_specs=[xa_spec, xc_spec] + mod_a + [mod_c, _const_spec((1, 1, d)), _const_spec((1, 1, d))]
        + [_const_spec(a.shape, a.dtype == BF16) for a in weights],
        out_specs=(xc_spec, conv_spec, state_spec),
        scratch_shapes=[
            pltpu.VMEM((SSM_CONV_DIM // LANES, CONV_PAD + lt, LANES), F32),
            pltpu.VMEM((lt, SSM_D_INNER), F32),
            pltpu.VMEM((SSM_CONV_DIM // LANES, lt, LANES), F32),
            pltpu.VMEM((lt, LANES), F32),
            pltpu.VMEM((lt, LANES), F32),
            pltpu.VMEM((q, SSM_D_INNER), F32),
            pltpu.VMEM((SSM_CONV_DIM // LANES, q, LANES), F32),
            pltpu.VMEM((q, LANES), F32),
            pltpu.VMEM((q, LANES), F32),
            pltpu.VMEM((q, SSM_D_INNER), F32),
            pltpu.VMEM((q, SSM_D_INNER), BF16),
            pltpu.VMEM((q, SSM_D_INNER), BF16),
            pltpu.VMEM((1,) + _STATE_SHAPE, F32),
        ],
        compiler_params=_params(1),
        name="ssm_sublayer",
    )(x, x, mod, mod, mod, gpre, gpost, *weights)


def _ssm_weights(in_w, conv_w, conv_b, dt_bias, a_log, d_skip, norm_w, out_w):
    pad = LANES - SSM_HEADS
    heads = np.arange(SSM_D_INNER) // SSM_HEAD_DIM
    e = (np.arange(LANES)[:, None] == heads[None, :]).astype(np.float32)
    return dict(
        wz=in_w[:, :SSM_D_INNER].astype(BF16),
        wxbc=in_w[:, SSM_D_INNER:SSM_D_INNER + SSM_CONV_DIM].astype(BF16),
        wdt=jnp.pad(in_w[:, SSM_D_INNER + SSM_CONV_DIM:], ((0, 0), (0, pad))).astype(BF16),
        conv_w=conv_w,
        conv_b=conv_b.reshape(1, SSM_CONV_DIM),
        dt_bias=jnp.pad(dt_bias, (0, pad)).reshape(1, LANES),
        a_log=jnp.pad(a_log, (0, pad)).reshape(1, LANES),
        d_x=jnp.repeat(d_skip, SSM_HEAD_DIM).reshape(1, SSM_D_INNER),
        norm_w=norm_w.reshape(1, SSM_D_INNER),
        wo=out_w.astype(BF16),
        e2=jnp.asarray(np.concatenate([e, e], axis=0), BF16),
    )


SOFTMAX_ROWS = 64


def _softmax_sink(logits, sink):
    mx = jnp.maximum(jnp.max(logits, axis=-1, keepdims=True), sink)
    e = jnp.exp(logits - mx)
    denom = jnp.sum(e, axis=-1, keepdims=True) + jnp.exp(sink - mx)
    return e * (1.0 / denom)


def _dedup(t):
    lane_lo = lax.broadcasted_iota(jnp.int32, (t.shape[0], LANES), 1) < HALF
    tiles = [jnp.where(lane_lo, t[:, (2 * i) * LANES:(2 * i + 1) * LANES],
                       t[:, (2 * i + 1) * LANES:(2 * i + 2) * LANES]) for i in range(ATTN_KV_HEADS // 2)]
    return jnp.concatenate(tiles, axis=1)


def _swa_prompt_kernel(sinks_ref, x_ref, sh_ref, sc_ref, gt_ref, gpre_ref, gpost_ref, wq_ref, wk_ref, wv_ref,
                       bq_ref, bk_ref, bv_ref, bias_ref, wo_ref, bo_ref,
                       o_ref, kc_ref, vc_ref, q_s, kbuf, vbuf, att_s, lg_s, p_s):
    j = pl.program_id(1)
    x = x_ref[0]
    tq = x.shape[0]
    kvw = ATTN_KV_HEADS * LANES

    @pl.when(j == 0)
    def _():
        kbuf[0:WINDOW, :] = jnp.zeros((WINDOW, kvw), BF16)
        vbuf[0:WINDOW, :] = jnp.zeros((WINDOW, kvw), BF16)

    hb = _mod_norm(x, gpre_ref[0], sc_ref[0], sh_ref[0]).astype(BF16)
    q_s[...] = ((_dot(hb, wq_ref[...]) + bq_ref[...]) * (ATTN_HEAD_DIM ** -0.5)).astype(BF16)
    k = _dot(hb, wk_ref[...]) + bk_ref[...]
    v = _dot(hb, wv_ref[...]) + bv_ref[...]
    kbuf[WINDOW:WINDOW + tq, :] = k.astype(BF16)
    vbuf[WINDOW:WINDOW + tq, :] = v.astype(BF16)
    kc_ref[0] = _dedup(k[tq - WINDOW:, :])
    vc_ref[0] = _dedup(v[tq - WINDOW:, :])

    lane_lo = lax.broadcasted_iota(jnp.int32, (2 * WINDOW, LANES), 1) < HALF
    col = lax.broadcasted_iota(jnp.int32, (2 * WINDOW, 2 * WINDOW), 1)
    zero_b = jnp.zeros((), BF16)
    pairs = ATTN_REP // 2

    def block(bi, carry):
        r0 = pl.multiple_of(bi * WINDOW, WINDOW)
        qrows = pl.ds(r0, WINDOW)
        no_prev = jnp.logical_and(j == 0, bi == 0)
        prev_mask = jnp.where(jnp.logical_and(no_prev, col < WINDOW), -jnp.inf, 0.0)
        for g in range(ATTN_KV_HEADS):
            kd = kbuf[pl.ds(r0, 2 * WINDOW), g * LANES:(g + 1) * LANES]
            q2 = jnp.concatenate([q_s[qrows, (g * pairs + pr) * LANES:(g * pairs + pr + 1) * LANES]
                                  for pr in range(pairs)], axis=0)
            for half in range(2):
                kh = jnp.where(lane_lo, kd, zero_b) if half == 0 else jnp.where(lane_lo, zero_b, kd)
                t = 2 * g + half
                lg_s[t] = _dot_nt(q2, kh) + bias_ref[t] + prev_mask
        for t in range(2 * ATTN_KV_HEADS):
            for rc in range(2 * WINDOW // SOFTMAX_ROWS):
                rs = slice(rc * SOFTMAX_ROWS, (rc + 1) * SOFTMAX_ROWS)
                h = 4 * (t // 2) + (t % 2) + 2 * ((rc * SOFTMAX_ROWS) // WINDOW)
                p_s[t, rs, :] = _softmax_sink(lg_s[t, rs, :], sinks_ref[h]).astype(BF16)
        for g in range(ATTN_KV_HEADS):
            vd = vbuf[pl.ds(r0, 2 * WINDOW), g * LANES:(g + 1) * LANES]
            acc = (_dot(p_s[2 * g], jnp.where(lane_lo, vd, zero_b))
                   + _dot(p_s[2 * g + 1], jnp.where(lane_lo, zero_b, vd)))
            for pr in range(pairs):
                pair = g * pairs + pr
                att_s[qrows, pair * LANES:(pair + 1) * LANES] = acc[pr * WINDOW:(pr + 1) * WINDOW].astype(BF16)
        return carry

    lax.fori_loop(0, tq // WINDOW, block, 0)

    out = _dot(att_s[...], wo_ref[...]) + bo_ref[...]
    o_ref[0] = x + gt_ref[0] * _rms(out, gpost_ref[0])
    kbuf[0:WINDOW, :] = kbuf[tq:tq + WINDOW, :]
    vbuf[0:WINDOW, :] = vbuf[tq:tq + WINDOW, :]


def _swa_prompt(x, mod, gpre, gpost, w, bias, tq):
    b, l, d = x.shape
    kvw = ATTN_KV_HEADS * LANES
    pairs = ATTN_REP // 2
    bias = bias.reshape(ATTN_KV_HEADS, pairs, 2, WINDOW, 2 * WINDOW).transpose(0, 2, 1, 3, 4)
    bias = bias.reshape(2 * ATTN_KV_HEADS, pairs * WINDOW, 2 * WINDOW)
    xspec = pl.BlockSpec((1, tq, d), lambda i, j: (i, j, 0))
    cache_spec = pl.BlockSpec((1, WINDOW, ATTN_KV_DIM), lambda i, j: (i, 0, 0))
    consts = [w["wq"], w["wk_dup"], w["wv_dup"], w["bq"], w["bk_dup"], w["bv_dup"], bias, w["wo"], w["bo"]]
    return pl.pallas_call(
        _swa_prompt_kernel,
        out_shape=(jax.ShapeDtypeStruct(x.shape, F32),
                   jax.ShapeDtypeStruct((b, WINDOW, ATTN_KV_DIM), F32),
                   jax.ShapeDtypeStruct((b, WINDOW, ATTN_KV_DIM), F32)),
        grid=(b // 1, l // tq),
        in_specs=[pl.BlockSpec(memory_space=pltpu.SMEM), xspec] + _mod_specs(1, 1)
        + [_const_spec((1, 1, d)), _const_spec((1, 1, d))]
        + [_const_spec(a.shape, a.dtype == BF16) for a in consts],
        out_specs=(xspec, cache_spec, cache_spec),
        scratch_shapes=[
            pltpu.VMEM((tq, ATTN_Q_DIM), BF16),
            pltpu.VMEM((WINDOW + tq, kvw), BF16),
            pltpu.VMEM((WINDOW + tq, kvw), BF16),
            pltpu.VMEM((tq, ATTN_Q_DIM), BF16),
            pltpu.VMEM((2 * ATTN_KV_HEADS, pairs * WINDOW, 2 * WINDOW), F32),
            pltpu.VMEM((2 * ATTN_KV_HEADS, pairs * WINDOW, 2 * WINDOW), BF16),
        ],
        compiler_params=_params(2),
        name="swa_prompt_sublayer",
    )(w["sinks"], x, mod, mod, mod, gpre, gpost, *consts)


def _swa_sample_kernel(x_ref, sh_ref, sc_ref, gt_ref, gpre_ref, gpost_ref, wq_ref, wk_ref, wv_ref,
                       bq_ref, bk_ref, bv_ref, ck_ref, cv_ref, bias_ref, sink_ref, wo_ref, bo_ref,
                       o_ref, ko_ref, vo_ref, q_s, kn_s, vn_s, kbuf, vbuf, att_s):
    x = x_ref[...]
    bt, lt, d = x.shape
    m = bt * lt
    nbuf = ck_ref.shape[1]
    wide = ATTN_KV_HEADS * ATTN_HEAD_DIM

    hb = _mod_norm(x, gpre_ref[...], sc_ref[...], sh_ref[...]).reshape(m, d).astype(BF16)
    q_s[...] = (_dot(hb, wq_ref[...]) + bq_ref[...]) * (ATTN_HEAD_DIM ** -0.5)
    kn_s[...] = _dot(hb, wk_ref[...]) + bk_ref[...]
    vn_s[...] = _dot(hb, wv_ref[...]) + bv_ref[...]
    pad_rows = kbuf.shape[0] - nbuf - lt
    kbuf[nbuf + lt:, :] = jnp.zeros((pad_rows, ATTN_KV_DIM), F32)
    vbuf[nbuf + lt:, :] = jnp.zeros((pad_rows, ATTN_KV_DIM), F32)

    def seq(bi, carry):
        rows = pl.ds(pl.multiple_of(bi * lt, lt), lt)
        kn = kn_s[rows, :]
        vn = vn_s[rows, :]
        ck = ck_ref[bi]
        cv = cv_ref[bi]
        kbuf[0:nbuf, :] = ck
        vbuf[0:nbuf, :] = cv
        kbuf[nbuf:nbuf + lt, :] = kn
        vbuf[nbuf:nbuf + lt, :] = vn
        ko_ref[bi, 0:nbuf - lt, :] = ck[lt:, :]
        vo_ref[bi, 0:nbuf - lt, :] = cv[lt:, :]
        ko_ref[bi, nbuf - lt:nbuf, :] = kn
        vo_ref[bi, nbuf - lt:nbuf, :] = vn
        qbig = jnp.concatenate([q_s[rows, h * wide:(h + 1) * wide] for h in range(ATTN_HEADS)], axis=0)
        logits = _dot_nt(qbig.astype(BF16), kbuf[...].astype(BF16)) + bias_ref[...]
        p = _softmax_sink(logits, sink_ref[:, 0:1]).astype(BF16)
        res = _dot(p, vbuf[...].astype(BF16))
        att_s[rows, :] = jnp.concatenate([res[h * lt:(h + 1) * lt, :] for h in range(ATTN_HEADS)], axis=1)
        return carry

    lax.fori_loop(0, bt, seq, 0)

    out = (_dot(att_s[...].astype(BF16), wo_ref[...]) + bo_ref[...]).reshape(bt, lt, d)
    o_ref[...] = x + gt_ref[...] * _rms(out, gpost_ref[...])


def _swa_sample(x, mod, gpre, gpost, w, bias, cache_k, cache_v, bt):
    b, lt, d = x.shape
    nbuf = cache_k.shape[1]
    m = bt * lt
    wide = ATTN_KV_HEADS * ATTN_HEAD_DIM
    big = ATTN_HEADS * wide
    keys = 2 * WINDOW
    xspec = pl.BlockSpec((bt, lt, d), lambda i: (i, 0, 0))
    cache_spec = pl.BlockSpec((bt, nbuf, ATTN_KV_DIM), lambda i: (i, 0, 0))
    bias_s = bias[:, :lt, :].reshape(ATTN_HEADS * lt, keys)
    sink_col = jnp.broadcast_to(jnp.repeat(w["sinks"], lt)[:, None], (ATTN_HEADS * lt, LANES))
    mods = [pl.BlockSpec((bt, None, 1, D_MODEL), functools.partial(lambda k, i: (i, k, 0, 0), 3 + k))
            for k in range(3)]
    pre = [w["wq_big"], w["wk"], w["wv"], w["bq_big"], w["bk"], w["bv"]]
    post = [bias_s, sink_col, w["wo_big"], w["bo"]]
    return pl.pallas_call(
        _swa_sample_kernel,
        out_shape=(jax.ShapeDtypeStruct(x.shape, F32),
                   jax.ShapeDtypeStruct(cache_k.shape, F32),
                   jax.ShapeDtypeStruct(cache_v.shape, F32)),
        grid=(b // bt,),
        in_specs=[xspec] + mods + [_const_spec((1, 1, d)), _const_spec((1, 1, d))]
        + [_const_spec(a.shape, a.dtype == BF16) for a in pre] + [cache_spec, cache_spec]
        + [_const_spec(a.shape, a.dtype == BF16) for a in post],
        out_specs=(xspec, cache_spec, cache_spec),
        scratch_shapes=[
            pltpu.VMEM((m, big), F32),
            pltpu.VMEM((m, ATTN_KV_DIM), F32),
            pltpu.VMEM((m, ATTN_KV_DIM), F32),
            pltpu.VMEM((keys, ATTN_KV_DIM), F32),
            pltpu.VMEM((keys, ATTN_KV_DIM), F32),
            pltpu.VMEM((m, big), F32),
        ],
        compiler_params=_params(1),
        name="swa_sample_sublayer",
    )(x, mod, mod, mod, gpre, gpost, *pre, cache_k, cache_v, *post)


def _swa_weights(qkv_w, qkv_b, sinks, o_w, o_b):
    d = qkv_w.shape[0]
    wq = qkv_w[:, :ATTN_Q_DIM]
    wk = qkv_w[:, ATTN_Q_DIM:ATTN_Q_DIM + ATTN_KV_DIM]
    wv = qkv_w[:, ATTN_Q_DIM + ATTN_KV_DIM:]
    bq = qkv_b[:ATTN_Q_DIM]
    bk = qkv_b[ATTN_Q_DIM:ATTN_Q_DIM + ATTN_KV_DIM]
    bv = qkv_b[ATTN_Q_DIM + ATTN_KV_DIM:]

    def dup(a):
        a4 = a.reshape(a.shape[:-1] + (ATTN_KV_HEADS, 1, ATTN_HEAD_DIM))
        a4 = jnp.broadcast_to(a4, a.shape[:-1] + (ATTN_KV_HEADS, 2, ATTN_HEAD_DIM))
        return a4.reshape(a.shape[:-1] + (ATTN_KV_HEADS * LANES,))

    own = jnp.asarray((np.arange(ATTN_HEADS)[:, None] // ATTN_REP == np.arange(ATTN_KV_HEADS)[None, :])
                      .astype(np.float32))
    wq_big = (wq.reshape(d, ATTN_HEADS, 1, ATTN_HEAD_DIM) * own[None, :, :, None]).reshape(d, -1)
    bq_big = (bq.reshape(ATTN_HEADS, 1, ATTN_HEAD_DIM) * own[:, :, None]).reshape(1, -1)
    wo_big = (o_w.reshape(ATTN_HEADS, 1, ATTN_HEAD_DIM, d) * own[:, :, None, None]).reshape(-1, d)
    return dict(
        sinks=sinks,
        wq=wq.astype(BF16), wk_dup=dup(wk).astype(BF16), wv_dup=dup(wv).astype(BF16),
        bq=bq.reshape(1, -1), bk_dup=dup(bk).reshape(1, -1), bv_dup=dup(bv).reshape(1, -1),
        wo=o_w.astype(BF16), bo=o_b.reshape(1, d),
        wq_big=wq_big.astype(BF16), bq_big=bq_big, wk=wk.astype(BF16), wv=wv.astype(BF16),
        bk=bk.reshape(1, -1), bv=bv.reshape(1, -1), wo_big=wo_big.astype(BF16),
    )


def _trunk(x, mod_all, state_ssm, state_conv, cache_k, cache_v, norm_pre, norm_post, ffn_in, ffn_out,
           ssm_w, swa_w, bias, tiles):
    bt, lt, n_sub = tiles["ffn"]
    b, l, d = x.shape
    sample = state_ssm is not None
    outs = {}
    for i in range(2):
        mod = mod_all[i].reshape(b, N_SUB * 3, 1, d)
        gpre = norm_pre[i].reshape(N_SUB, 1, 1, d)
        gpost = norm_post[i].reshape(N_SUB, 1, 1, d)
        x = _ffn(x, mod, 0, gpre[0], gpost[0], ffn_in, ffn_out, i, 0, bt, lt, n_sub)
        if i == 0:
            if sample:
                x, conv_new, ssm_new = _ssm_sample(x, mod, gpre[1], gpost[1], ssm_w, state_conv[0], state_ssm[0],
                                                   tiles["ssm"])
            else:
                x, conv_new, ssm_new = _ssm_prompt(x, mod, gpre[1], gpost[1], ssm_w)
            outs["conv"] = conv_new[None]
            outs["ssm"] = ssm_new[None]
        else:
            if sample:
                x, k_new, v_new = _swa_sample(x, mod, gpre[1], gpost[1], swa_w, bias,
                                              cache_k[0].reshape(b, -1, ATTN_KV_DIM),
                                              cache_v[0].reshape(b, -1, ATTN_KV_DIM), tiles["swa"])
            else:
                x, k_new, v_new = _swa_prompt(x, mod, gpre[1], gpost[1], swa_w, bias, tiles["swa"])
            shape = (1, b, -1, ATTN_KV_HEADS, ATTN_HEAD_DIM)
            outs["k"] = k_new.reshape(shape)
            outs["v"] = v_new.reshape(shape)
        x = _ffn(x, mod, 2, gpre[2], gpost[2], ffn_in, ffn_out, i, 1, bt, lt, n_sub)
    return x, outs["ssm"], outs["conv"], outs["k"], outs["v"]


def kernel(x_prompt, x_sample, state_ssm, state_conv, cache_k, cache_v, c_prompt, c_sample, ada_w, ada_b, norm_pre, norm_post, ffn_w_in, ffn_w_out, ssm_in_w, ssm_conv_w, ssm_conv_b, ssm_dt_bias, ssm_a_log, ssm_d, ssm_norm_w, ssm_out_w, attn_qkv_w, attn_qkv_b, attn_sinks, attn_o_w, attn_o_b, rel_bias):
    nb = x_prompt.shape[0]
    mod_all = _ada(jnp.concatenate([c_prompt, c_sample], axis=0), ada_w, ada_b)
    bias = _bias_table(rel_bias)
    ffn_in = ffn_w_in.astype(BF16)
    ffn_out = ffn_w_out.astype(BF16)
    ssm_w = _ssm_weights(ssm_in_w[0], ssm_conv_w[0], ssm_conv_b[0], ssm_dt_bias[0], ssm_a_log[0], ssm_d[0],
                         ssm_norm_w[0], ssm_out_w[0])
    swa_w = _swa_weights(attn_qkv_w[0], attn_qkv_b[0], attn_sinks[0], attn_o_w[0], attn_o_b[0])
    common = (norm_pre, norm_post, ffn_in, ffn_out, ssm_w, swa_w, bias)

    p_tiles = dict(ffn=(1, 1024, 2), swa=512)
    y_p, ssm_p, conv_p, k_p, v_p = _trunk(x_prompt, mod_all[:, :nb], None, None, None, None, *common, p_tiles)
    ns, ls = x_sample.shape[:2]
    s_tiles = dict(ffn=(min(64, ns), ls, 1), ssm=min(4, ns), swa=min(16, ns))
    y_s, ssm_s, conv_s, k_s, v_s = _trunk(x_sample, mod_all[:, nb:], state_ssm, state_conv, cache_k, cache_v,
                                          *common, s_tiles)
    return (y_p, y_s, ssm_p, conv_p, k_p, v_p, ssm_s, conv_s, k_s, v_s)
```

```python
import functools
import math

import numpy as np
import jax
import jax.numpy as jnp
from jax import lax
from jax.experimental import pallas as pl
from jax.experimental.pallas import tpu as pltpu

F32 = jnp.float32
BF16 = jnp.bfloat16

D_MODEL = 1024
N_SUB = 3
RMS_EPS = 1e-6
FFN_RES = 0.5
D_FF = 2816

SSM_D_INNER = 2048
SSM_HEAD_DIM = 64
SSM_HEADS = 32
SSM_GROUPS = 4
SSM_HPG = 8
SSM_STATE = 128
SSM_CONV = 4
SSM_CHUNK = 128
SSM_GN = SSM_GROUPS * SSM_STATE
SSM_CONV_DIM = SSM_D_INNER + 2 * SSM_GN
SSM_GROUP_WIDTH = SSM_HPG * SSM_HEAD_DIM

ATTN_HEAD_DIM = 64
ATTN_HEADS = 16
ATTN_KV_HEADS = 4
ATTN_REP = 4
WINDOW = 128
REL_BUCKETS = 32
ATTN_Q_DIM = ATTN_HEADS * ATTN_HEAD_DIM
ATTN_KV_DIM = ATTN_KV_HEADS * ATTN_HEAD_DIM

LANES = 128
SUBLANES = 8
HALF = LANES // 2
VMEM_LIMIT_BYTES = 56 * 1024 * 1024
CONV_PAD = SUBLANES


def _dot(a, b):
    return jnp.dot(a, b, preferred_element_type=F32)


def _dot_nt(a, b):
    return lax.dot_general(a, b, (((1,), (1,)), ((), ())), preferred_element_type=F32)


def _dot_tn(a, b):
    return lax.dot_general(a, b, (((0,), (0,)), ((), ())), preferred_element_type=F32)


def _silu(x):
    h = 0.5 * x
    return h * jnp.tanh(h) + h


def _rms(x, g):
    return x * lax.rsqrt(jnp.mean(x * x, axis=-1, keepdims=True) + RMS_EPS) * g


def _mod_norm(x, g, scale, shift):
    return _rms(x, g) * (1.0 + scale) + shift


def _split3(x):
    hi = x.astype(BF16)
    r1 = x - hi.astype(F32)
    mid = r1.astype(BF16)
    lo = (r1 - mid.astype(F32)).astype(BF16)
    return hi, mid, lo


def _const_spec(shape, single_buffer=False):
    nd = len(shape)
    kw = {"pipeline_mode": pl.Buffered(1)} if single_buffer else {}
    return pl.BlockSpec(shape, lambda *_: (0,) * nd, **kw)


def _params(n_grid):
    return pltpu.CompilerParams(
        dimension_semantics=("arbitrary",) * n_grid,
        vmem_limit_bytes=VMEM_LIMIT_BYTES,
    )


def _ada_kernel(c_ref, w_ref, b_ref, o_ref):
    cs = _silu(c_ref[...]).astype(BF16)
    o_ref[0] = _dot(cs, w_ref[0].astype(BF16)) + b_ref[0]


def _ada(c_all, ada_w, ada_b, tn=1152):
    depth, d, n = ada_w.shape
    bc = c_all.shape[0]
    return pl.pallas_call(
        _ada_kernel,
        out_shape=jax.ShapeDtypeStruct((depth, bc, n), F32),
        grid=(depth, n // tn),
        in_specs=[
            pl.BlockSpec((bc, d), lambda l, j: (0, 0)),
            pl.BlockSpec((1, d, tn), lambda l, j: (l, 0, j)),
            pl.BlockSpec((1, 1, tn), lambda l, j: (l, 0, j)),
        ],
        out_specs=pl.BlockSpec((1, bc, tn), lambda l, j: (l, 0, j)),
        compiler_params=_params(2),
        name="ada_mod",
    )(c_all, ada_w, ada_b.reshape(depth, 1, n))


def _t5_bucket_table():
    i = np.arange(WINDOW)[:, None]
    j = np.arange(2 * WINDOW)[None, :]
    dist = i + WINDOW - j
    exact = REL_BUCKETS // 2
    df = np.maximum(dist, 1).astype(np.float32)
    large = exact + (np.log(df / np.float32(exact)) / np.float32(math.log(WINDOW / exact))
                     * np.float32(REL_BUCKETS - exact)).astype(np.int32)
    large = np.minimum(large, REL_BUCKETS - 1)
    bucket = np.where(dist < exact, dist, large)
    valid = (dist >= 0) & (dist <= WINDOW)
    return np.where(valid, bucket, -1).astype(np.int32)


def _bias_kernel(rb_ref, idx_ref, o_ref):
    h = pl.program_id(0)
    idx = idx_ref[...]
    acc = jnp.full(idx.shape, -jnp.inf, F32)
    for b in range(REL_BUCKETS):
        acc = jnp.where(idx == b, rb_ref[b, h], acc)
    o_ref[0, 0] = acc
    col = lax.broadcasted_iota(jnp.int32, idx.shape, 1)
    o_ref[1, 0] = jnp.where(col < WINDOW, -jnp.inf, acc)


def _bias_table(rel_bias):
    idx = jnp.asarray(_t5_bucket_table())
    return pl.pallas_call(
        _bias_kernel,
        out_shape=jax.ShapeDtypeStruct((2, ATTN_HEADS, WINDOW, 2 * WINDOW), F32),
        grid=(ATTN_HEADS,),
        in_specs=[
            pl.BlockSpec(memory_space=pltpu.SMEM),
            pl.BlockSpec((WINDOW, 2 * WINDOW), lambda h: (0, 0)),
        ],
        out_specs=pl.BlockSpec((2, 1, WINDOW, 2 * WINDOW), lambda h: (0, h, 0, 0)),
        compiler_params=_params(1),
        name="rel_bias_table",
    )(rel_bias, idx)


MXU_TILE = 256
FF_CHUNKS = ((0, D_FF),)


def _ffn_kernel(x_ref, sh_ref, sc_ref, gt_ref, gpre_ref, gpost_ref, win_ref, wout_ref, o_ref, *, ff_chunks, n_sub):
    bt, lt, d = x_ref.shape

    def sub_slices(s):
        if bt == 1:
            return slice(None), slice(s * (lt // n_sub), (s + 1) * (lt // n_sub))
        return slice(s * (bt // n_sub), (s + 1) * (bt // n_sub)), slice(None)

    def pre(s):
        bs, ls = sub_slices(s)
        x = x_ref[bs, ls, :]
        h = _mod_norm(x, gpre_ref[...], sc_ref[bs], sh_ref[bs])
        return h.reshape(x.shape[0] * x.shape[1], d).astype(BF16)

    def post(s, acc):
        bs, ls = sub_slices(s)
        x = x_ref[bs, ls, :]
        o_ref[bs, ls, :] = x + FFN_RES * gt_ref[bs] * _rms(acc.reshape(x.shape), gpost_ref[...])

    hb = pre(0)
    prev = None
    for s in range(n_sub):
        acc = None
        hb_next = None
        for c, (lo, hi) in enumerate(ff_chunks):
            g = _dot(hb, win_ref[:, lo:hi])
            u = _dot(hb, win_ref[:, D_FF + lo:D_FF + hi])
            a = (_silu(g) * u).astype(BF16)
            part = _dot(a, wout_ref[lo:hi, :])
            acc = part if acc is None else acc + part
            if c == 0:
                if prev is not None:
                    post(s - 1, prev)
                if s + 1 < n_sub:
                    hb_next = pre(s + 1)
        prev = acc
        hb = hb_next
    post(n_sub - 1, prev)


def _mod_specs(bt, sub):
    return [pl.BlockSpec((bt, None, 1, D_MODEL), functools.partial(lambda k, i, j: (i, k, 0, 0), sub * 3 + k))
            for k in range(3)]


def _ffn(x, mod, sub, gpre, gpost, w_in, w_out, layer, which, bt, lt, n_sub):
    b, l, d = x.shape
    xspec = pl.BlockSpec((bt, lt, d), lambda i, j: (i, j, 0))
    wspec = [pl.BlockSpec((None, None) + w.shape[2:], lambda i, j: (layer, which, 0, 0),
                          pipeline_mode=pl.Buffered(1)) for w in (w_in, w_out)]
    return pl.pallas_call(
        functools.partial(_ffn_kernel, ff_chunks=FF_CHUNKS, n_sub=n_sub),
        out_shape=jax.ShapeDtypeStruct(x.shape, F32),
        grid=(b // bt, l // lt),
        in_specs=[xspec] + _mod_specs(bt, sub) + [
            _const_spec((1, 1, d)), _const_spec((1, 1, d))] + wspec,
        out_specs=xspec,
        compiler_params=_params(2),
        name="ffn_sublayer",
    )(x, mod, mod, mod, gpre, gpost, w_in, w_out)


def _chunk_consts(qc, seg):
    r = lax.broadcasted_iota(jnp.int32, (qc, qc), 0)
    c = lax.broadcasted_iota(jnp.int32, (qc, qc), 1)
    seg_shift = seg.bit_length() - 1
    same = jnp.right_shift(r, seg_shift) == jnp.right_shift(c, seg_shift)
    causal = same & (r >= c)
    tri = jnp.where(causal, 1.0, 0.0)
    upper = jnp.where(same & (c > r), 1.0, 0.0)
    tu = jnp.concatenate([tri, upper], axis=0).astype(BF16)
    er = lax.broadcasted_iota(jnp.int32, (LANES, LANES), 0)
    ec = lax.broadcasted_iota(jnp.int32, (LANES, LANES), 1)
    eye = jnp.where(er == ec, 1.0, 0.0).astype(BF16)
    lane_lo = lax.broadcasted_iota(jnp.int32, (qc, LANES), 1) < HALF
    return tu, eye, causal, lane_lo


def _ssm_project(hb, wz_ref, wxbc_ref, wdt_ref, dtb_ref, alog_ref):
    z = _dot(hb, wz_ref[...])
    xbc_raw = _dot(hb, wxbc_ref[...])
    dt_raw = _dot(hb, wdt_ref[...]) + dtb_ref[...]
    dt = jnp.maximum(dt_raw, 0.0) + jnp.log1p(jnp.exp(-jnp.abs(dt_raw)))
    return z, xbc_raw, dt, dt * (-jnp.exp(alog_ref[...]))


def _gate_norm(y, xs, z, dx, nw):
    yg = (y + xs * dx) * _silu(z)
    parts = []
    for g in range(SSM_GROUPS):
        v = yg[:, g * SSM_GROUP_WIDTH:(g + 1) * SSM_GROUP_WIDTH]
        parts.append(v * lax.rsqrt(jnp.mean(v * v, axis=-1, keepdims=True) + RMS_EPS))
    return (jnp.concatenate(parts, axis=1) * nw).astype(BF16)


def _cols(ref, rows, c0, c1):
    if len(ref.shape) == 2:
        return ref[rows, c0:c1]
    return jnp.concatenate([ref[c, rows, :] for c in range(c0 // LANES, c1 // LANES)], axis=1)


def _ssd_chunk(r0, qc, seg, s_first, *, xbc_s, dt_s, da_s, y_s, e2_ref, h_in, h_out, consts, reset=None):
    tu, eye, causal, lane_lo = consts
    rows = pl.ds(r0, qc)
    hi, mid, lo = _split3(da_s[rows, :])
    cs2 = _dot(tu, hi) + _dot(tu, mid) + _dot(tu, lo)
    a_cs = cs2[:qc]
    ea = jnp.exp(a_cs)
    dte = jnp.exp(cs2[qc:])

    stack = jnp.concatenate([dt_s[rows, :], ea, dte], axis=0)
    s_hi = stack.astype(BF16)
    s_lo = (stack - s_hi.astype(F32)).astype(BF16)
    sx = _dot(jnp.concatenate([s_hi, s_lo], axis=1), e2_ref[...])
    dt_x, ea_x, dte_x = sx[:qc], sx[qc:2 * qc], sx[2 * qc:]

    xdt = _cols(xbc_s, rows, 0, SSM_D_INNER) * dt_x
    xdt_b = xdt.astype(BF16)
    xd_b = (xdt * dte_x).astype(BF16)
    bm = _cols(xbc_s, rows, SSM_D_INNER, SSM_D_INNER + SSM_GN).astype(BF16)
    cm = _cols(xbc_s, rows, SSM_D_INNER + SSM_GN, SSM_CONV_DIM).astype(BF16)

    a_hi, a_mid, a_lo = _split3(a_cs)
    a_cs_t = _dot_nt(eye, a_hi) + _dot_nt(eye, a_mid) + _dot_nt(eye, a_lo)

    zero_b = jnp.zeros((), BF16)
    for g in range(SSM_GROUPS):
        gsl = slice(g * SSM_STATE, (g + 1) * SSM_STATE)
        cb = _dot_nt(cm[:, gsl], bm[:, gsl])
        for pr in range(SSM_HPG // 2):
            h0 = g * SSM_HPG + 2 * pr
            psl = slice((h0 // 2) * LANES, (h0 // 2 + 1) * LANES)
            xp = xdt_b[:, psl]
            acc = None
            for half in range(2):
                h = h0 + half
                seg_sum = a_cs[:, h:h + 1] - a_cs_t[h:h + 1, :]
                decay = jnp.exp(jnp.where(causal, seg_sum, -jnp.inf))
                w = (decay * cb).astype(BF16)
                xh = jnp.where(lane_lo if half == 0 else jnp.logical_not(lane_lo), xp, zero_b)
                o = _dot(w, xh)
                acc = o if acc is None else acc + o
            y_s[rows, psl] = acc

    for t in range(qc // seg):
        tr = slice(t * seg, (t + 1) * seg)
        trows = pl.ds(r0 + t * seg, seg)
        last = t * seg + seg - 1
        for g in range(SSM_GROUPS):
            gsl = slice(g * SSM_STATE, (g + 1) * SSM_STATE)
            csl = slice(g * SSM_GROUP_WIDTH, (g + 1) * SSM_GROUP_WIDTH)
            hsl = slice(g * SSM_HPG, (g + 1) * SSM_HPG)
            hg = h_in[s_first + t, hsl].reshape(SSM_GROUP_WIDTH, SSM_STATE)
            if reset is not None:
                hg = jnp.where(reset, 0.0, hg)
            y_off = _dot_nt(cm[tr, gsl], hg.astype(BF16)) * ea_x[tr, csl]
            y_s[trows, csl] = y_s[trows, csl] + y_off
            upd = _dot_tn(xd_b[tr, csl], bm[tr, gsl])
            cdec = jnp.concatenate(
                [jnp.broadcast_to(ea[last:last + 1, g * SSM_HPG + r:g * SSM_HPG + r + 1],
                                  (SSM_HEAD_DIM, SSM_STATE)) for r in range(SSM_HPG)], axis=0)
            h_out[s_first + t, hsl] = (hg * cdec + upd).reshape(SSM_HPG, SSM_HEAD_DIM, SSM_STATE)


HIST = SSM_CONV - 1
_SSM_WEIGHT_NAMES = ("wz", "wxbc", "wdt", "conv_w", "conv_b", "dt_bias", "a_log", "d_x", "norm_w", "wo", "e2")
_STATE_SHAPE = (SSM_HEADS, SSM_HEAD_DIM, SSM_STATE)


def _conv_silu(xpad, cw_ref, cb_ref, lt):
    conv = cb_ref[...]
    for k in range(SSM_CONV):
        conv = conv + xpad[:, CONV_PAD - HIST + k:CONV_PAD - HIST + k + lt, :] * cw_ref[k:k + 1, :]
    return _silu(conv)


def _ssm_sample_kernel(x_ref, sh_ref, sc_ref, gt_ref, gpre_ref, gpost_ref, wz_ref, wxbc_ref, wdt_ref, cw_ref,
                       cb_ref, dtb_ref, alog_ref, dx_ref, nw_ref, wo_ref, e2_ref, conv_in_ref, h0_ref,
                       o_ref, conv_out_ref, h_out_ref, xpad, xbc_s, dt_s, da_s, y_s):
    x = x_ref[...]
    bt, lt, d = x.shape
    m = bt * lt
    hb = _mod_norm(x, gpre_ref[...], sc_ref[...], sh_ref[...]).reshape(m, d).astype(BF16)
    z, xbc_raw, dt, da = _ssm_project(hb, wz_ref, wxbc_ref, wdt_ref, dtb_ref, alog_ref)
    dt_s[...] = dt
    da_s[...] = da
    xpad[:, CONV_PAD:CONV_PAD + lt, :] = xbc_raw.reshape(bt, lt, SSM_CONV_DIM)
    xpad[:, CONV_PAD - HIST:CONV_PAD, :] = conv_in_ref[...]
    xbc_s[...] = _conv_silu(xpad, cw_ref, cb_ref, lt).reshape(m, SSM_CONV_DIM)
    conv_out_ref[...] = xpad[:, CONV_PAD + lt - HIST:CONV_PAD + lt, :]
    _ssd_chunk(0, m, lt, 0, xbc_s=xbc_s, dt_s=dt_s, da_s=da_s, y_s=y_s, e2_ref=e2_ref,
               h_in=h0_ref, h_out=h_out_ref, consts=_chunk_consts(m, lt))
    yn = _gate_norm(y_s[...], xbc_s[:, 0:SSM_D_INNER], z, dx_ref[...], nw_ref[...])
    out = _dot(yn, wo_ref[...]).reshape(bt, lt, d)
    o_ref[...] = x + gt_ref[...] * _rms(out, gpost_ref[...])


def _ssm_sample(x, mod, gpre, gpost, w, conv_in, h0, bt):
    b, lt, d = x.shape
    m = bt * lt
    xspec = pl.BlockSpec((bt, lt, d), lambda i: (i, 0, 0))
    conv_spec = pl.BlockSpec((bt, HIST, SSM_CONV_DIM), lambda i: (i, 0, 0))
    state_spec = pl.BlockSpec((bt,) + _STATE_SHAPE, lambda i: (i, 0, 0, 0))
    mods = [pl.BlockSpec((bt, None, 1, d), functools.partial(lambda k, i: (i, k, 0, 0), 3 + k)) for k in range(3)]
    weights = [w[n] for n in _SSM_WEIGHT_NAMES]
    return pl.pallas_call(
        _ssm_sample_kernel,
        out_shape=(jax.ShapeDtypeStruct(x.shape, F32),
                   jax.ShapeDtypeStruct((b, HIST, SSM_CONV_DIM), F32),
                   jax.ShapeDtypeStruct((b,) + _STATE_SHAPE, F32)),
        grid=(b // bt,),
        in_specs=[xspec] + mods + [_const_spec((1, 1, d)), _const_spec((1, 1, d))]
        + [_const_spec(a.shape, a.dtype == BF16) for a in weights] + [conv_spec, state_spec],
        out_specs=(xspec, conv_spec, state_spec),
        scratch_shapes=[
            pltpu.VMEM((bt, CONV_PAD + lt, SSM_CONV_DIM), F32),
            pltpu.VMEM((m, SSM_CONV_DIM), F32),
            pltpu.VMEM((m, LANES), F32),
            pltpu.VMEM((m, LANES), F32),
            pltpu.VMEM((m, SSM_D_INNER), F32),
        ],
        compiler_params=_params(1),
        name="ssm_sublayer_state",
    )(x, mod, mod, mod, gpre, gpost, *weights, conv_in, h0)


def _ssm_prompt_kernel(xa_ref, xc_ref, sh_ref, sc_ref, gt_ref, gpre_ref, gpost_ref, wz_ref, wxbc_ref, wdt_ref,
                       cw_ref, cb_ref, dtb_ref, alog_ref, dx_ref, nw_ref, wo_ref, e2_ref,
                       o_ref, conv_out_ref, h_out_ref,
                       xpad, z_s, xbc_s, dt_s, da_s, zc_s, xbcc_s, dtc_s, dac_s, y_s, yn_a, yn_b, h_s,
                       *, steps_per_seq):
    s = pl.program_id(0)
    q = SSM_CHUNK
    lt = 2 * q
    seq_start = (s % steps_per_seq) == 0
    n_slabs = SSM_CONV_DIM // LANES

    @pl.when(s == 0)
    def _():
        for ref in (z_s, xbc_s, dt_s, da_s, yn_a, h_s):
            ref[...] = jnp.zeros(ref.shape, ref.dtype)

    @pl.when(seq_start)
    def _():
        xpad[:, CONV_PAD - HIST:CONV_PAD, :] = jnp.zeros((n_slabs, HIST, LANES), F32)

    zc_s[...] = z_s[q:lt, :]
    xbcc_s[...] = xbc_s[:, q:lt, :]
    dtc_s[...] = dt_s[q:lt, :]
    dac_s[...] = da_s[q:lt, :]
    consts = _chunk_consts(q, q)

    def finish(yn_ref, rows):
        out = _dot(yn_ref[...], wo_ref[...])
        o_ref[0, rows, :] = xc_ref[0, rows, :] + gt_ref[0] * _rms(out, gpost_ref[0])

    def scan(xbc_v, z_v, dt_v, da_v, yn_ref, reset):
        _ssd_chunk(0, q, q, 0, xbc_s=xbc_v, dt_s=dt_v, da_s=da_v, y_s=y_s, e2_ref=e2_ref,
                   h_in=h_s, h_out=h_s, consts=consts, reset=reset)
        xs = _cols(xbc_v, slice(None), 0, SSM_D_INNER)
        yn_ref[...] = _gate_norm(y_s[...], xs, z_v[...], dx_ref[...], nw_ref[...])

    hb = _mod_norm(xa_ref[0], gpre_ref[0], sc_ref[0], sh_ref[0]).astype(BF16)
    z, xbc_raw, dt, da = _ssm_project(hb, wz_ref, wxbc_ref, wdt_ref, dtb_ref, alog_ref)
    z_s[...] = z
    dt_s[...] = dt
    da_s[...] = da
    for c in range(n_slabs):
        xpad[c, CONV_PAD:CONV_PAD + lt, :] = xbc_raw[:, c * LANES:(c + 1) * LANES]

    finish(yn_a, slice(0, q))
    scan(xbcc_s, zc_s, dtc_s, dac_s, yn_b, None)
    h_out_ref[...] = h_s[...]

    half = lt // 2
    for c in range(n_slabs):
        csl = slice(c * LANES, (c + 1) * LANES)
        for par in range(2):
            acc = cb_ref[:, csl]
            for k in range(SSM_CONV):
                rows = pl.ds(CONV_PAD - HIST + par + k, half, stride=2)
                acc = acc + xpad[c, rows, :] * cw_ref[k:k + 1, csl]
            xbc_s[c, pl.ds(par, half, stride=2), :] = _silu(acc)
        new_hist = xpad[c, CONV_PAD + lt - HIST:CONV_PAD + lt, :]
        conv_out_ref[0, :, csl] = new_hist
        xpad[c, CONV_PAD - HIST:CONV_PAD, :] = new_hist

    scan(xbc_s.at[:, 0:q, :], z_s.at[0:q], dt_s.at[0:q], da_s.at[0:q], yn_a, seq_start)
    finish(yn_b, slice(q, lt))


def _ssm_prompt(x, mod, gpre, gpost, w):
    b, l, d = x.shape
    q = SSM_CHUNK
    lt = 2 * q
    spq = l // lt
    n_blocks = b * spq

    def cur(s):
        blk = jnp.minimum(s, n_blocks - 1)
        return blk // spq, blk % spq

    def prev(s):
        blk = jnp.maximum(s - 1, 0)
        return blk // spq, blk % spq

    xa_spec = pl.BlockSpec((1, lt, d), lambda s: cur(s) + (0,))
    xc_spec = pl.BlockSpec((1, lt, d), lambda s: prev(s) + (0,))
    mod_a = [pl.BlockSpec((1, None, 1, d), functools.partial(lambda k, s: (cur(s)[0], k, 0, 0), 3 + k))
             for k in range(2)]
    mod_c = pl.BlockSpec((1, None, 1, d), lambda s: (prev(s)[0], 5, 0, 0))
    conv_spec = pl.BlockSpec((1, HIST, SSM_CONV_DIM), lambda s: (cur(s)[0], 0, 0))
    state_spec = pl.BlockSpec((1,) + _STATE_SHAPE, lambda s: (prev(s)[0], 0, 0, 0))
    weights = [w[n] for n in _SSM_WEIGHT_NAMES]
    return pl.pallas_call(
        functools.partial(_ssm_prompt_kernel, steps_per_seq=spq),
        out_shape=(jax.ShapeDtypeStruct(x.shape, F32),
                   jax.ShapeDtypeStruct((b, HIST, SSM_CONV_DIM), F32),
                   jax.ShapeDtypeStruct((b,) + _STATE_SHAPE, F32)),
        grid=(n_blocks + 1,),
        in_specs=[xa_spec, xc_spec] + mod_a + [mod_c, _const_spec((1, 1, d)), _const_spec((1, 1, d))]
        + [_const_spec(a.shape, a.dtype == BF16) for a in weights],
        out_specs=(xc_spec, conv_spec, state_spec),
        scratch_shapes=[
            pltpu.VMEM((SSM_CONV_DIM // LANES, CONV_PAD + lt, LANES), F32),
            pltpu.VMEM((lt, SSM_D_INNER), F32),
            pltpu.VMEM((SSM_CONV_DIM // LANES, lt, LANES), F32),
            pltpu.VMEM((lt, LANES), F32),
            pltpu.VMEM((lt, LANES), F32),
            pltpu.VMEM((q, SSM_D_INNER), F32),
            pltpu.VMEM((SSM_CONV_DIM // LANES, q, LANES), F32),
            pltpu.VMEM((q, LANES), F32),
            pltpu.VMEM((q, LANES), F32),
            pltpu.VMEM((q, SSM_D_INNER), F32),
            pltpu.VMEM((q, SSM_D_INNER), BF16),
            pltpu.VMEM((q, SSM_D_INNER), BF16),
            pltpu.VMEM((1,) + _STATE_SHAPE, F32),
        ],
        compiler_params=_params(1),
        name="ssm_sublayer",
    )(x, x, mod, mod, mod, gpre, gpost, *weights)


def _ssm_weights(in_w, conv_w, conv_b, dt_bias, a_log, d_skip, norm_w, out_w):
    pad = LANES - SSM_HEADS
    heads = np.arange(SSM_D_INNER) // SSM_HEAD_DIM
    e = (np.arange(LANES)[:, None] == heads[None, :]).astype(np.float32)
    return dict(
        wz=in_w[:, :SSM_D_INNER].astype(BF16),
        wxbc=in_w[:, SSM_D_INNER:SSM_D_INNER + SSM_CONV_DIM].astype(BF16),
        wdt=jnp.pad(in_w[:, SSM_D_INNER + SSM_CONV_DIM:], ((0, 0), (0, pad))).astype(BF16),
        conv_w=conv_w,
        conv_b=conv_b.reshape(1, SSM_CONV_DIM),
        dt_bias=jnp.pad(dt_bias, (0, pad)).reshape(1, LANES),
        a_log=jnp.pad(a_log, (0, pad)).reshape(1, LANES),
        d_x=jnp.repeat(d_skip, SSM_HEAD_DIM).reshape(1, SSM_D_INNER),
        norm_w=norm_w.reshape(1, SSM_D_INNER),
        wo=out_w.astype(BF16),
        e2=jnp.asarray(np.concatenate([e, e], axis=0), BF16),
    )


SOFTMAX_ROWS = 64


def _softmax_sink(logits, sink):
    mx = jnp.maximum(jnp.max(logits, axis=-1, keepdims=True), sink)
    e = jnp.exp(logits - mx)
    denom = jnp.sum(e, axis=-1, keepdims=True) + jnp.exp(sink - mx)
    return e * (1.0 / denom)


def _dedup(t):
    lane_lo = lax.broadcasted_iota(jnp.int32, (t.shape[0], LANES), 1) < HALF
    tiles = [jnp.where(lane_lo, t[:, (2 * i) * LANES:(2 * i + 1) * LANES],
                       t[:, (2 * i + 1) * LANES:(2 * i + 2) * LANES]) for i in range(ATTN_KV_HEADS // 2)]
    return jnp.concatenate(tiles, axis=1)


def _swa_prompt_kernel(sinks_ref, x_ref, sh_ref, sc_ref, gt_ref, gpre_ref, gpost_ref, wq_ref, wk_ref, wv_ref,
                       bq_ref, bk_ref, bv_ref, bias_ref, wo_ref, bo_ref,
                       o_ref, kc_ref, vc_ref, q_s, kbuf, vbuf, att_s, lg_s, p_s):
    j = pl.program_id(1)
    x = x_ref[0]
    tq = x.shape[0]
    kvw = ATTN_KV_HEADS * LANES

    @pl.when(j == 0)
    def _():
        kbuf[0:WINDOW, :] = jnp.zeros((WINDOW, kvw), BF16)
        vbuf[0:WINDOW, :] = jnp.zeros((WINDOW, kvw), BF16)

    hb = _mod_norm(x, gpre_ref[0], sc_ref[0], sh_ref[0]).astype(BF16)
    q_s[...] = ((_dot(hb, wq_ref[...]) + bq_ref[...]) * (ATTN_HEAD_DIM ** -0.5)).astype(BF16)
    k = _dot(hb, wk_ref[...]) + bk_ref[...]
    v = _dot(hb, wv_ref[...]) + bv_ref[...]
    kbuf[WINDOW:WINDOW + tq, :] = k.astype(BF16)
    vbuf[WINDOW:WINDOW + tq, :] = v.astype(BF16)
    kc_ref[0] = _dedup(k[tq - WINDOW:, :])
    vc_ref[0] = _dedup(v[tq - WINDOW:, :])

    lane_lo = lax.broadcasted_iota(jnp.int32, (2 * WINDOW, LANES), 1) < HALF
    zero_b = jnp.zeros((), BF16)
    pairs = ATTN_REP // 2

    def block(bi, carry):
        r0 = pl.multiple_of(bi * WINDOW, WINDOW)
        qrows = pl.ds(r0, WINDOW)
        first = jnp.where(jnp.logical_and(j == 0, bi == 0), 2 * ATTN_KV_HEADS, 0)
        for g in range(ATTN_KV_HEADS):
            kd = kbuf[pl.ds(r0, 2 * WINDOW), g * LANES:(g + 1) * LANES]
            q2 = jnp.concatenate([q_s[qrows, (g * pairs + pr) * LANES:(g * pairs + pr + 1) * LANES]
                                  for pr in range(pairs)], axis=0)
            for half in range(2):
                kh = jnp.where(lane_lo, kd, zero_b) if half == 0 else jnp.where(lane_lo, zero_b, kd)
                t = 2 * g + half
                lg_s[t] = _dot_nt(q2, kh) + bias_ref[first + t]
        for t in range(2 * ATTN_KV_HEADS):
            for rc in range(2 * WINDOW // SOFTMAX_ROWS):
                rs = slice(rc * SOFTMAX_ROWS, (rc + 1) * SOFTMAX_ROWS)
                h = 4 * (t // 2) + (t % 2) + 2 * ((rc * SOFTMAX_ROWS) // WINDOW)
                p_s[t, rs, :] = _softmax_sink(lg_s[t, rs, :], sinks_ref[h]).astype(BF16)
        for g in range(ATTN_KV_HEADS):
            vd = vbuf[pl.ds(r0, 2 * WINDOW), g * LANES:(g + 1) * LANES]
            acc = (_dot(p_s[2 * g], jnp.where(lane_lo, vd, zero_b))
                   + _dot(p_s[2 * g + 1], jnp.where(lane_lo, zero_b, vd)))
            for pr in range(pairs):
                pair = g * pairs + pr
                att_s[qrows, pair * LANES:(pair + 1) * LANES] = acc[pr * WINDOW:(pr + 1) * WINDOW].astype(BF16)
        return carry

    lax.fori_loop(0, tq // WINDOW, block, 0)

    out = _dot(att_s[...], wo_ref[...]) + bo_ref[...]
    o_ref[0] = x + gt_ref[0] * _rms(out, gpost_ref[0])
    kbuf[0:WINDOW, :] = kbuf[tq:tq + WINDOW, :]
    vbuf[0:WINDOW, :] = vbuf[tq:tq + WINDOW, :]


def _swa_prompt(x, mod, gpre, gpost, w, bias, tq):
    b, l, d = x.shape
    kvw = ATTN_KV_HEADS * LANES
    pairs = ATTN_REP // 2
    bias = bias.reshape(2, ATTN_KV_HEADS, pairs, 2, WINDOW, 2 * WINDOW).transpose(0, 1, 3, 2, 4, 5)
    bias = bias.reshape(2 * 2 * ATTN_KV_HEADS, pairs * WINDOW, 2 * WINDOW)
    xspec = pl.BlockSpec((1, tq, d), lambda i, j: (i, j, 0))
    cache_spec = pl.BlockSpec((1, WINDOW, ATTN_KV_DIM), lambda i, j: (i, 0, 0))
    consts = [w["wq"], w["wk_dup"], w["wv_dup"], w["bq"], w["bk_dup"], w["bv_dup"], bias, w["wo"], w["bo"]]
    return pl.pallas_call(
        _swa_prompt_kernel,
        out_shape=(jax.ShapeDtypeStruct(x.shape, F32),
                   jax.ShapeDtypeStruct((b, WINDOW, ATTN_KV_DIM), F32),
                   jax.ShapeDtypeStruct((b, WINDOW, ATTN_KV_DIM), F32)),
        grid=(b // 1, l // tq),
        in_specs=[pl.BlockSpec(memory_space=pltpu.SMEM), xspec] + _mod_specs(1, 1)
        + [_const_spec((1, 1, d)), _const_spec((1, 1, d))]
        + [_const_spec(a.shape, a.dtype == BF16) for a in consts],
        out_specs=(xspec, cache_spec, cache_spec),
        scratch_shapes=[
            pltpu.VMEM((tq, ATTN_Q_DIM), BF16),
            pltpu.VMEM((WINDOW + tq, kvw), BF16),
            pltpu.VMEM((WINDOW + tq, kvw), BF16),
            pltpu.VMEM((tq, ATTN_Q_DIM), BF16),
            pltpu.VMEM((2 * ATTN_KV_HEADS, pairs * WINDOW, 2 * WINDOW), F32),
            pltpu.VMEM((2 * ATTN_KV_HEADS, pairs * WINDOW, 2 * WINDOW), BF16),
        ],
        compiler_params=_params(2),
        name="swa_prompt_sublayer",
    )(w["sinks"], x, mod, mod, mod, gpre, gpost, *consts)


def _swa_sample_kernel(x_ref, sh_ref, sc_ref, gt_ref, gpre_ref, gpost_ref, wq_ref, wk_ref, wv_ref,
                       bq_ref, bk_ref, bv_ref, ck_ref, cv_ref, bias_ref, sink_ref, wo_ref, bo_ref,
                       o_ref, ko_ref, vo_ref, q_s, kn_s, vn_s, att_s):
    x = x_ref[...]
    bt, lt, d = x.shape
    m = bt * lt
    nbuf = ck_ref.shape[1]
    wide = ATTN_KV_HEADS * ATTN_HEAD_DIM

    hb = _mod_norm(x, gpre_ref[...], sc_ref[...], sh_ref[...]).reshape(m, d).astype(BF16)
    q_s[...] = (_dot(hb, wq_ref[...]) + bq_ref[...]) * (ATTN_HEAD_DIM ** -0.5)
    kn_s[...] = _dot(hb, wk_ref[...]) + bk_ref[...]
    vn_s[...] = _dot(hb, wv_ref[...]) + bv_ref[...]
    zpad = jnp.zeros((bias_ref.shape[1] - nbuf - lt, ATTN_KV_DIM), F32)

    def seq(bi, carry):
        rows = pl.ds(pl.multiple_of(bi * lt, lt), lt)
        kn = kn_s[rows, :]
        vn = vn_s[rows, :]
        ck = ck_ref[bi]
        cv = cv_ref[bi]
        kfull = jnp.concatenate([ck, kn, zpad], axis=0).astype(BF16)
        vfull = jnp.concatenate([cv, vn, zpad], axis=0).astype(BF16)
        ko_ref[bi, 0:nbuf - lt, :] = ck[lt:, :]
        vo_ref[bi, 0:nbuf - lt, :] = cv[lt:, :]
        ko_ref[bi, nbuf - lt:nbuf, :] = kn
        vo_ref[bi, nbuf - lt:nbuf, :] = vn
        qbig = jnp.concatenate([q_s[rows, h * wide:(h + 1) * wide] for h in range(ATTN_HEADS)], axis=0)
        logits = _dot_nt(qbig.astype(BF16), kfull) + bias_ref[...]
        p = _softmax_sink(logits, sink_ref[:, 0:1]).astype(BF16)
        res = _dot(p, vfull)
        att_s[rows, :] = jnp.concatenate([res[h * lt:(h + 1) * lt, :] for h in range(ATTN_HEADS)], axis=1)
        return carry

    lax.fori_loop(0, bt, seq, 0, unroll=8)

    out = (_dot(att_s[...].astype(BF16), wo_ref[...]) + bo_ref[...]).reshape(bt, lt, d)
    o_ref[...] = x + gt_ref[...] * _rms(out, gpost_ref[...])


def _swa_sample(x, mod, gpre, gpost, w, bias, cache_k, cache_v, bt):
    b, lt, d = x.shape
    nbuf = cache_k.shape[1]
    m = bt * lt
    wide = ATTN_KV_HEADS * ATTN_HEAD_DIM
    big = ATTN_HEADS * wide
    keys = 2 * WINDOW
    xspec = pl.BlockSpec((bt, lt, d), lambda i: (i, 0, 0))
    cache_spec = pl.BlockSpec((bt, nbuf, ATTN_KV_DIM), lambda i: (i, 0, 0))
    bias_s = bias[0, :, :lt, :].reshape(ATTN_HEADS * lt, keys)
    sink_col = jnp.broadcast_to(jnp.repeat(w["sinks"], lt)[:, None], (ATTN_HEADS * lt, LANES))
    mods = [pl.BlockSpec((bt, None, 1, D_MODEL), functools.partial(lambda k, i: (i, k, 0, 0), 3 + k))
            for k in range(3)]
    pre = [w["wq_big"], w["wk"], w["wv"], w["bq_big"], w["bk"], w["bv"]]
    post = [bias_s, sink_col, w["wo_big"], w["bo"]]
    return pl.pallas_call(
        _swa_sample_kernel,
        out_shape=(jax.ShapeDtypeStruct(x.shape, F32),
                   jax.ShapeDtypeStruct(cache_k.shape, F32),
                   jax.ShapeDtypeStruct(cache_v.shape, F32)),
        grid=(b // bt,),
        in_specs=[xspec] + mods + [_const_spec((1, 1, d)), _const_spec((1, 1, d))]
        + [_const_spec(a.shape, a.dtype == BF16) for a in pre] + [cache_spec, cache_spec]
        + [_const_spec(a.shape, a.dtype == BF16) for a in post],
        out_specs=(xspec, cache_spec, cache_spec),
        scratch_shapes=[
            pltpu.VMEM((m, big), F32),
            pltpu.VMEM((m, ATTN_KV_DIM), F32),
            pltpu.VMEM((m, ATTN_KV_DIM), F32),
            pltpu.VMEM((m, big), F32),
        ],
        compiler_params=_params(1),
        name="swa_sample_sublayer",
    )(x, mod, mod, mod, gpre, gpost, *pre, cache_k, cache_v, *post)


def _swa_weights(qkv_w, qkv_b, sinks, o_w, o_b):
    d = qkv_w.shape[0]
    wq = qkv_w[:, :ATTN_Q_DIM]
    wk = qkv_w[:, ATTN_Q_DIM:ATTN_Q_DIM + ATTN_KV_DIM]
    wv = qkv_w[:, ATTN_Q_DIM + ATTN_KV_DIM:]
    bq = qkv_b[:ATTN_Q_DIM]
    bk = qkv_b[ATTN_Q_DIM:ATTN_Q_DIM + ATTN_KV_DIM]
    bv = qkv_b[ATTN_Q_DIM + ATTN_KV_DIM:]

    def dup(a):
        a4 = a.reshape(a.shape[:-1] + (ATTN_KV_HEADS, 1, ATTN_HEAD_DIM))
        a4 = jnp.broadcast_to(a4, a.shape[:-1] + (ATTN_KV_HEADS, 2, ATTN_HEAD_DIM))
        return a4.reshape(a.shape[:-1] + (ATTN_KV_HEADS * LANES,))

    own = jnp.asarray((np.arange(ATTN_HEADS)[:, None] // ATTN_REP == np.arange(ATTN_KV_HEADS)[None, :])
                      .astype(np.float32))
    wq_big = (wq.reshape(d, ATTN_HEADS, 1, ATTN_HEAD_DIM) * own[None, :, :, None]).reshape(d, -1)
    bq_big = (bq.reshape(ATTN_HEADS, 1, ATTN_HEAD_DIM) * own[:, :, None]).reshape(1, -1)
    wo_big = (o_w.reshape(ATTN_HEADS, 1, ATTN_HEAD_DIM, d) * own[:, :, None, None]).reshape(-1, d)
    return dict(
        sinks=sinks,
        wq=wq.astype(BF16), wk_dup=dup(wk).astype(BF16), wv_dup=dup(wv).astype(BF16),
        bq=bq.reshape(1, -1), bk_dup=dup(bk).reshape(1, -1), bv_dup=dup(bv).reshape(1, -1),
        wo=o_w.astype(BF16), bo=o_b.reshape(1, d),
        wq_big=wq_big.astype(BF16), bq_big=bq_big, wk=wk.astype(BF16), wv=wv.astype(BF16),
        bk=bk.reshape(1, -1), bv=bv.reshape(1, -1), wo_big=wo_big.astype(BF16),
    )


def _trunk(x, mod_all, state_ssm, state_conv, cache_k, cache_v, norm_pre, norm_post, ffn_in, ffn_out,
           ssm_w, swa_w, bias, tiles):
    bt, lt, n_sub = tiles["ffn"]
    b, l, d = x.shape
    sample = state_ssm is not None
    outs = {}
    for i in range(2):
        mod = mod_all[i].reshape(b, N_SUB * 3, 1, d)
        gpre = norm_pre[i].reshape(N_SUB, 1, 1, d)
        gpost = norm_post[i].reshape(N_SUB, 1, 1, d)
        x = _ffn(x, mod, 0, gpre[0], gpost[0], ffn_in, ffn_out, i, 0, bt, lt, n_sub)
        if i == 0:
            if sample:
                x, conv_new, ssm_new = _ssm_sample(x, mod, gpre[1], gpost[1], ssm_w, state_conv[0], state_ssm[0],
                                                   tiles["ssm"])
            else:
                x, conv_new, ssm_new = _ssm_prompt(x, mod, gpre[1], gpost[1], ssm_w)
            outs["conv"] = conv_new[None]
            outs["ssm"] = ssm_new[None]
        else:
            if sample:
                x, k_new, v_new = _swa_sample(x, mod, gpre[1], gpost[1], swa_w, bias,
                                              cache_k[0].reshape(b, -1, ATTN_KV_DIM),
                                              cache_v[0].reshape(b, -1, ATTN_KV_DIM), tiles["swa"])
            else:
                x, k_new, v_new = _swa_prompt(x, mod, gpre[1], gpost[1], swa_w, bias, tiles["swa"])
            shape = (1, b, -1, ATTN_KV_HEADS, ATTN_HEAD_DIM)
            outs["k"] = k_new.reshape(shape)
            outs["v"] = v_new.reshape(shape)
        x = _ffn(x, mod, 2, gpre[2], gpost[2], ffn_in, ffn_out, i, 1, bt, lt, n_sub)
    return x, outs["ssm"], outs["conv"], outs["k"], outs["v"]


def kernel(x_prompt, x_sample, state_ssm, state_conv, cache_k, cache_v, c_prompt, c_sample, ada_w, ada_b, norm_pre, norm_post, ffn_w_in, ffn_w_out, ssm_in_w, ssm_conv_w, ssm_conv_b, ssm_dt_bias, ssm_a_log, ssm_d, ssm_norm_w, ssm_out_w, attn_qkv_w, attn_qkv_b, attn_sinks, attn_o_w, attn_o_b, rel_bias):
    nb = x_prompt.shape[0]
    mod_all = _ada(jnp.concatenate([c_prompt, c_sample], axis=0), ada_w, ada_b)
    bias = _bias_table(rel_bias)
    ffn_in = ffn_w_in.astype(BF16)
    ffn_out = ffn_w_out.astype(BF16)
    ssm_w = _ssm_weights(ssm_in_w[0], ssm_conv_w[0], ssm_conv_b[0], ssm_dt_bias[0], ssm_a_log[0], ssm_d[0],
                         ssm_norm_w[0], ssm_out_w[0])
    swa_w = _swa_weights(attn_qkv_w[0], attn_qkv_b[0], attn_sinks[0], attn_o_w[0], attn_o_b[0])
    common = (norm_pre, norm_post, ffn_in, ffn_out, ssm_w, swa_w, bias)

    p_tiles = dict(ffn=(1, 1024, 2), swa=512)
    y_p, ssm_p, conv_p, k_p, v_p = _trunk(x_prompt, mod_all[:, :nb], None, None, None, None, *common, p_tiles)
    ns, ls = x_sample.shape[:2]
    s_tiles = dict(ffn=(min(64, ns), ls, 1), ssm=min(4, ns), swa=min(16, ns))
    y_s, ssm_s, conv_s, k_s, v_s = _trunk(x_sample, mod_all[:, nb:], state_ssm, state_conv, cache_k, cache_v,
                                          *common, s_tiles)
    return (y_p, y_s, ssm_p, conv_p, k_p, v_p, ssm_s, conv_s, k_s, v_s)
```

```python
import functools
import math

import numpy as np
import jax
import jax.numpy as jnp
from jax import lax
from jax.experimental import pallas as pl
from jax.experimental.pallas import tpu as pltpu

F32 = jnp.float32
BF16 = jnp.bfloat16

D_MODEL = 1024
N_SUB = 3
RMS_EPS = 1e-6
FFN_RES = 0.5
D_FF = 2816

SSM_D_INNER = 2048
SSM_HEAD_DIM = 64
SSM_HEADS = 32
SSM_GROUPS = 4
SSM_HPG = 8
SSM_STATE = 128
SSM_CONV = 4
SSM_CHUNK = 128
SSM_GN = SSM_GROUPS * SSM_STATE
SSM_CONV_DIM = SSM_D_INNER + 2 * SSM_GN
SSM_GROUP_WIDTH = SSM_HPG * SSM_HEAD_DIM

ATTN_HEAD_DIM = 64
ATTN_HEADS = 16
ATTN_KV_HEADS = 4
ATTN_REP = 4
WINDOW = 128
REL_BUCKETS = 32
ATTN_Q_DIM = ATTN_HEADS * ATTN_HEAD_DIM
ATTN_KV_DIM = ATTN_KV_HEADS * ATTN_HEAD_DIM

LANES = 128
SUBLANES = 8
HALF = LANES // 2
VMEM_LIMIT_BYTES = 56 * 1024 * 1024
CONV_PAD = SUBLANES


def _dot(a, b):
    return jnp.dot(a, b, preferred_element_type=F32)


def _dot_nt(a, b):
    return lax.dot_general(a, b, (((1,), (1,)), ((), ())), preferred_element_type=F32)


def _dot_tn(a, b):
    return lax.dot_general(a, b, (((0,), (0,)), ((), ())), preferred_element_type=F32)


def _silu(x):
    h = 0.5 * x
    return h * jnp.tanh(h) + h


def _rms(x, g):
    return x * lax.rsqrt(jnp.mean(x * x, axis=-1, keepdims=True) + RMS_EPS) * g


def _mod_norm(x, g, scale, shift):
    return _rms(x, g) * (1.0 + scale) + shift


def _split3(x):
    hi = x.astype(BF16)
    r1 = x - hi.astype(F32)
    mid = r1.astype(BF16)
    lo = (r1 - mid.astype(F32)).astype(BF16)
    return hi, mid, lo


def _const_spec(shape, single_buffer=False):
    nd = len(shape)
    kw = {"pipeline_mode": pl.Buffered(1)} if single_buffer else {}
    return pl.BlockSpec(shape, lambda *_: (0,) * nd, **kw)


def _params(n_grid, flags=None):
    return pltpu.CompilerParams(
        dimension_semantics=("arbitrary",) * n_grid,
        vmem_limit_bytes=VMEM_LIMIT_BYTES,
        flags=flags,
    )


def _ada_kernel(c_ref, w_ref, b_ref, o_ref):
    cs = _silu(c_ref[...]).astype(BF16)
    o_ref[0] = _dot(cs, w_ref[0].astype(BF16)) + b_ref[0]


def _ada(c_all, ada_w, ada_b, tn=1152):
    depth, d, n = ada_w.shape
    bc = c_all.shape[0]
    return pl.pallas_call(
        _ada_kernel,
        out_shape=jax.ShapeDtypeStruct((depth, bc, n), F32),
        grid=(depth, n // tn),
        in_specs=[
            pl.BlockSpec((bc, d), lambda l, j: (0, 0)),
            pl.BlockSpec((1, d, tn), lambda l, j: (l, 0, j)),
            pl.BlockSpec((1, 1, tn), lambda l, j: (l, 0, j)),
        ],
        out_specs=pl.BlockSpec((1, bc, tn), lambda l, j: (l, 0, j)),
        compiler_params=_params(2),
        name="ada_mod",
    )(c_all, ada_w, ada_b.reshape(depth, 1, n))


def _t5_bucket_table():
    i = np.arange(WINDOW)[:, None]
    j = np.arange(2 * WINDOW)[None, :]
    dist = i + WINDOW - j
    exact = REL_BUCKETS // 2
    df = np.maximum(dist, 1).astype(np.float32)
    large = exact + (np.log(df / np.float32(exact)) / np.float32(math.log(WINDOW / exact))
                     * np.float32(REL_BUCKETS - exact)).astype(np.int32)
    large = np.minimum(large, REL_BUCKETS - 1)
    bucket = np.where(dist < exact, dist, large)
    valid = (dist >= 0) & (dist <= WINDOW)
    return np.where(valid, bucket, -1).astype(np.int32)


def _bias_kernel(rb_ref, idx_ref, o_ref):
    h = pl.program_id(0)
    idx = idx_ref[...]
    acc = jnp.full(idx.shape, -jnp.inf, F32)
    for b in range(REL_BUCKETS):
        acc = jnp.where(idx == b, rb_ref[b, h], acc)
    o_ref[0, 0] = acc
    col = lax.broadcasted_iota(jnp.int32, idx.shape, 1)
    o_ref[1, 0] = jnp.where(col < WINDOW, -jnp.inf, acc)


def _bias_table(rel_bias):
    idx = jnp.asarray(_t5_bucket_table())
    return pl.pallas_call(
        _bias_kernel,
        out_shape=jax.ShapeDtypeStruct((2, ATTN_HEADS, WINDOW, 2 * WINDOW), F32),
        grid=(ATTN_HEADS,),
        in_specs=[
            pl.BlockSpec(memory_space=pltpu.SMEM),
            pl.BlockSpec((WINDOW, 2 * WINDOW), lambda h: (0, 0)),
        ],
        out_specs=pl.BlockSpec((2, 1, WINDOW, 2 * WINDOW), lambda h: (0, h, 0, 0)),
        compiler_params=_params(1),
        name="rel_bias_table",
    )(rel_bias, idx)


MXU_TILE = 256
FF_CHUNKS = ((0, D_FF),)


def _ffn_kernel(x_ref, sh_ref, sc_ref, gt_ref, gpre_ref, gpost_ref, win_ref, wout_ref, o_ref, *, ff_chunks, n_sub):
    bt, lt, d = x_ref.shape

    def sub_slices(s):
        if bt == 1:
            return slice(None), slice(s * (lt // n_sub), (s + 1) * (lt // n_sub))
        return slice(s * (bt // n_sub), (s + 1) * (bt // n_sub)), slice(None)

    def pre(s):
        bs, ls = sub_slices(s)
        x = x_ref[bs, ls, :]
        h = _mod_norm(x, gpre_ref[...], sc_ref[bs], sh_ref[bs])
        return h.reshape(x.shape[0] * x.shape[1], d).astype(BF16)

    def post(s, acc):
        bs, ls = sub_slices(s)
        x = x_ref[bs, ls, :]
        o_ref[bs, ls, :] = x + FFN_RES * gt_ref[bs] * _rms(acc.reshape(x.shape), gpost_ref[...])

    hb = pre(0)
    prev = None
    for s in range(n_sub):
        acc = None
        hb_next = None
        for c, (lo, hi) in enumerate(ff_chunks):
            g = _dot(hb, win_ref[:, lo:hi])
            u = _dot(hb, win_ref[:, D_FF + lo:D_FF + hi])
            a = (_silu(g) * u).astype(BF16)
            part = _dot(a, wout_ref[lo:hi, :])
            acc = part if acc is None else acc + part
            if c == 0:
                if prev is not None:
                    post(s - 1, prev)
                if s + 1 < n_sub:
                    hb_next = pre(s + 1)
        prev = acc
        hb = hb_next
    post(n_sub - 1, prev)


def _mod_specs(bt, sub):
    return [pl.BlockSpec((bt, None, 1, D_MODEL), functools.partial(lambda k, i, j: (i, k, 0, 0), sub * 3 + k))
            for k in range(3)]


def _ffn(x, mod, sub, gpre, gpost, w_in, w_out, layer, which, bt, lt, n_sub):
    b, l, d = x.shape
    xspec = pl.BlockSpec((bt, lt, d), lambda i, j: (i, j, 0))
    wspec = [pl.BlockSpec((None, None) + w.shape[2:], lambda i, j: (layer, which, 0, 0),
                          pipeline_mode=pl.Buffered(1)) for w in (w_in, w_out)]
    return pl.pallas_call(
        functools.partial(_ffn_kernel, ff_chunks=FF_CHUNKS, n_sub=n_sub),
        out_shape=jax.ShapeDtypeStruct(x.shape, F32),
        grid=(b // bt, l // lt),
        in_specs=[xspec] + _mod_specs(bt, sub) + [
            _const_spec((1, 1, d)), _const_spec((1, 1, d))] + wspec,
        out_specs=xspec,
        compiler_params=_params(2),
        name="ffn_sublayer",
    )(x, mod, mod, mod, gpre, gpost, w_in, w_out)


def _chunk_consts(qc, seg):
    r = lax.broadcasted_iota(jnp.int32, (qc, qc), 0)
    c = lax.broadcasted_iota(jnp.int32, (qc, qc), 1)
    seg_shift = seg.bit_length() - 1
    same = jnp.right_shift(r, seg_shift) == jnp.right_shift(c, seg_shift)
    causal = same & (r >= c)
    tri = jnp.where(causal, 1.0, 0.0)
    upper = jnp.where(same & (c > r), 1.0, 0.0)
    tu = jnp.concatenate([tri, upper], axis=0).astype(BF16)
    er = lax.broadcasted_iota(jnp.int32, (LANES, LANES), 0)
    ec = lax.broadcasted_iota(jnp.int32, (LANES, LANES), 1)
    eye = jnp.where(er == ec, 1.0, 0.0).astype(BF16)
    lane_lo = lax.broadcasted_iota(jnp.int32, (qc, LANES), 1) < HALF
    return tu, eye, causal, lane_lo


def _ssm_project(hb, wz_ref, wxbc_ref, wdt_ref, dtb_ref, alog_ref):
    z = _dot(hb, wz_ref[...])
    xbc_raw = _dot(hb, wxbc_ref[...])
    dt_raw = _dot(hb, wdt_ref[...]) + dtb_ref[...]
    dt = jnp.maximum(dt_raw, 0.0) + jnp.log1p(jnp.exp(-jnp.abs(dt_raw)))
    return z, xbc_raw, dt, dt * (-jnp.exp(alog_ref[...]))


def _gate_norm(y, xs, z, dx, nw):
    yg = (y + xs * dx) * _silu(z)
    parts = []
    for g in range(SSM_GROUPS):
        v = yg[:, g * SSM_GROUP_WIDTH:(g + 1) * SSM_GROUP_WIDTH]
        parts.append(v * lax.rsqrt(jnp.mean(v * v, axis=-1, keepdims=True) + RMS_EPS))
    return (jnp.concatenate(parts, axis=1) * nw).astype(BF16)


def _cols(ref, rows, c0, c1):
    if len(ref.shape) == 2:
        return ref[rows, c0:c1]
    return jnp.concatenate([ref[c, rows, :] for c in range(c0 // LANES, c1 // LANES)], axis=1)


def _ssd_chunk(r0, qc, seg, s_first, *, xbc_s, dt_s, da_s, y_s, e2_ref, h_in, h_out, consts, reset=None):
    tu, eye, causal, lane_lo = consts
    rows = pl.ds(r0, qc)
    hi, mid, lo = _split3(da_s[rows, :])
    cs2 = _dot(tu, hi) + _dot(tu, mid) + _dot(tu, lo)
    a_cs = cs2[:qc]
    ea = jnp.exp(a_cs)
    dte = jnp.exp(cs2[qc:])

    stack = jnp.concatenate([dt_s[rows, :], ea, dte], axis=0)
    s_hi = stack.astype(BF16)
    s_lo = (stack - s_hi.astype(F32)).astype(BF16)
    sx = _dot(jnp.concatenate([s_hi, s_lo], axis=1), e2_ref[...])
    dt_x, ea_x, dte_x = sx[:qc], sx[qc:2 * qc], sx[2 * qc:]

    xdt = _cols(xbc_s, rows, 0, SSM_D_INNER) * dt_x
    xdt_b = xdt.astype(BF16)
    xd_b = (xdt * dte_x).astype(BF16)
    bm = _cols(xbc_s, rows, SSM_D_INNER, SSM_D_INNER + SSM_GN).astype(BF16)
    cm = _cols(xbc_s, rows, SSM_D_INNER + SSM_GN, SSM_CONV_DIM).astype(BF16)

    a_hi, a_mid, a_lo = _split3(a_cs)
    a_cs_t = _dot_nt(eye, a_hi) + _dot_nt(eye, a_mid) + _dot_nt(eye, a_lo)

    zero_b = jnp.zeros((), BF16)
    for g in range(SSM_GROUPS):
        gsl = slice(g * SSM_STATE, (g + 1) * SSM_STATE)
        cb = _dot_nt(cm[:, gsl], bm[:, gsl])
        for pr in range(SSM_HPG // 2):
            h0 = g * SSM_HPG + 2 * pr
            psl = slice((h0 // 2) * LANES, (h0 // 2 + 1) * LANES)
            xp = xdt_b[:, psl]
            acc = None
            for half in range(2):
                h = h0 + half
                seg_sum = a_cs[:, h:h + 1] - a_cs_t[h:h + 1, :]
                decay = jnp.exp(jnp.where(causal, seg_sum, -jnp.inf))
                w = (decay * cb).astype(BF16)
                xh = jnp.where(lane_lo if half == 0 else jnp.logical_not(lane_lo), xp, zero_b)
                o = _dot(w, xh)
                acc = o if acc is None else acc + o
            y_s[rows, psl] = acc

    for t in range(qc // seg):
        tr = slice(t * seg, (t + 1) * seg)
        trows = pl.ds(r0 + t * seg, seg)
        last = t * seg + seg - 1
        for g in range(SSM_GROUPS):
            gsl = slice(g * SSM_STATE, (g + 1) * SSM_STATE)
            csl = slice(g * SSM_GROUP_WIDTH, (g + 1) * SSM_GROUP_WIDTH)
            hsl = slice(g * SSM_HPG, (g + 1) * SSM_HPG)
            hg = h_in[s_first + t, hsl].reshape(SSM_GROUP_WIDTH, SSM_STATE)
            if reset is not None:
                hg = jnp.where(reset, 0.0, hg)
            y_off = _dot_nt(cm[tr, gsl], hg.astype(BF16)) * ea_x[tr, csl]
            y_s[trows, csl] = y_s[trows, csl] + y_off
            upd = _dot_tn(xd_b[tr, csl], bm[tr, gsl])
            cdec = jnp.concatenate(
                [jnp.broadcast_to(ea[last:last + 1, g * SSM_HPG + r:g * SSM_HPG + r + 1],
                                  (SSM_HEAD_DIM, SSM_STATE)) for r in range(SSM_HPG)], axis=0)
            h_out[s_first + t, hsl] = (hg * cdec + upd).reshape(SSM_HPG, SSM_HEAD_DIM, SSM_STATE)


HIST = SSM_CONV - 1
_SSM_WEIGHT_NAMES = ("wz", "wxbc", "wdt", "conv_w", "conv_b", "dt_bias", "a_log", "d_x", "norm_w", "wo", "e2")
_STATE_SHAPE = (SSM_HEADS, SSM_HEAD_DIM, SSM_STATE)


def _conv_silu(xpad, cw_ref, cb_ref, lt):
    conv = cb_ref[...]
    for k in range(SSM_CONV):
        conv = conv + xpad[:, CONV_PAD - HIST + k:CONV_PAD - HIST + k + lt, :] * cw_ref[k:k + 1, :]
    return _silu(conv)


def _ssm_sample_kernel(x_ref, sh_ref, sc_ref, gt_ref, gpre_ref, gpost_ref, wz_ref, wxbc_ref, wdt_ref, cw_ref,
                       cb_ref, dtb_ref, alog_ref, dx_ref, nw_ref, wo_ref, e2_ref, conv_in_ref, h0_ref,
                       o_ref, conv_out_ref, h_out_ref, xpad, xbc_s, dt_s, da_s, y_s):
    x = x_ref[...]
    bt, lt, d = x.shape
    m = bt * lt
    hb = _mod_norm(x, gpre_ref[...], sc_ref[...], sh_ref[...]).reshape(m, d).astype(BF16)
    z, xbc_raw, dt, da = _ssm_project(hb, wz_ref, wxbc_ref, wdt_ref, dtb_ref, alog_ref)
    dt_s[...] = dt
    da_s[...] = da
    xpad[:, CONV_PAD:CONV_PAD + lt, :] = xbc_raw.reshape(bt, lt, SSM_CONV_DIM)
    xpad[:, CONV_PAD - HIST:CONV_PAD, :] = conv_in_ref[...]
    xbc_s[...] = _conv_silu(xpad, cw_ref, cb_ref, lt).reshape(m, SSM_CONV_DIM)
    conv_out_ref[...] = xpad[:, CONV_PAD + lt - HIST:CONV_PAD + lt, :]
    _ssd_chunk(0, m, lt, 0, xbc_s=xbc_s, dt_s=dt_s, da_s=da_s, y_s=y_s, e2_ref=e2_ref,
               h_in=h0_ref, h_out=h_out_ref, consts=_chunk_consts(m, lt))
    yn = _gate_norm(y_s[...], xbc_s[:, 0:SSM_D_INNER], z, dx_ref[...], nw_ref[...])
    out = _dot(yn, wo_ref[...]).reshape(bt, lt, d)
    o_ref[...] = x + gt_ref[...] * _rms(out, gpost_ref[...])


def _ssm_sample(x, mod, gpre, gpost, w, conv_in, h0, bt):
    b, lt, d = x.shape
    m = bt * lt
    xspec = pl.BlockSpec((bt, lt, d), lambda i: (i, 0, 0))
    conv_spec = pl.BlockSpec((bt, HIST, SSM_CONV_DIM), lambda i: (i, 0, 0))
    state_spec = pl.BlockSpec((bt,) + _STATE_SHAPE, lambda i: (i, 0, 0, 0))
    mods = [pl.BlockSpec((bt, None, 1, d), functools.partial(lambda k, i: (i, k, 0, 0), 3 + k)) for k in range(3)]
    weights = [w[n] for n in _SSM_WEIGHT_NAMES]
    return pl.pallas_call(
        _ssm_sample_kernel,
        out_shape=(jax.ShapeDtypeStruct(x.shape, F32),
                   jax.ShapeDtypeStruct((b, HIST, SSM_CONV_DIM), F32),
                   jax.ShapeDtypeStruct((b,) + _STATE_SHAPE, F32)),
        grid=(b // bt,),
        in_specs=[xspec] + mods + [_const_spec((1, 1, d)), _const_spec((1, 1, d))]
        + [_const_spec(a.shape, a.dtype == BF16) for a in weights] + [conv_spec, state_spec],
        out_specs=(xspec, conv_spec, state_spec),
        scratch_shapes=[
            pltpu.VMEM((bt, CONV_PAD + lt, SSM_CONV_DIM), F32),
            pltpu.VMEM((m, SSM_CONV_DIM), F32),
            pltpu.VMEM((m, LANES), F32),
            pltpu.VMEM((m, LANES), F32),
            pltpu.VMEM((m, SSM_D_INNER), F32),
        ],
        compiler_params=_params(1),
        name="ssm_sublayer_state",
    )(x, mod, mod, mod, gpre, gpost, *weights, conv_in, h0)


def _ssm_prompt_kernel(xa_ref, xc_ref, sh_ref, sc_ref, gt_ref, gpre_ref, gpost_ref, wz_ref, wxbc_ref, wdt_ref,
                       cw_ref, cb_ref, dtb_ref, alog_ref, dx_ref, nw_ref, wo_ref, e2_ref,
                       o_ref, conv_out_ref, h_out_ref,
                       xpad, z_s, xbc_s, dt_s, da_s, zc_s, xbcc_s, dtc_s, dac_s, y_s, yn_a, yn_b, h_s,
                       *, steps_per_seq):
    s = pl.program_id(0)
    q = SSM_CHUNK
    lt = 2 * q
    seq_start = (s % steps_per_seq) == 0
    n_slabs = SSM_CONV_DIM // LANES

    @pl.when(s == 0)
    def _():
        for ref in (z_s, xbc_s, dt_s, da_s, yn_a, h_s):
            ref[...] = jnp.zeros(ref.shape, ref.dtype)

    @pl.when(seq_start)
    def _():
        xpad[:, CONV_PAD - HIST:CONV_PAD, :] = jnp.zeros((n_slabs, HIST, LANES), F32)

    zc_s[...] = z_s[q:lt, :]
    xbcc_s[...] = xbc_s[:, q:lt, :]
    dtc_s[...] = dt_s[q:lt, :]
    dac_s[...] = da_s[q:lt, :]
    consts = _chunk_consts(q, q)

    def finish(yn_ref, rows):
        out = _dot(yn_ref[...], wo_ref[...])
        o_ref[0, rows, :] = xc_ref[0, rows, :] + gt_ref[0] * _rms(out, gpost_ref[0])

    def scan(xbc_v, z_v, dt_v, da_v, yn_ref, reset):
        _ssd_chunk(0, q, q, 0, xbc_s=xbc_v, dt_s=dt_v, da_s=da_v, y_s=y_s, e2_ref=e2_ref,
                   h_in=h_s, h_out=h_s, consts=consts, reset=reset)
        xs = _cols(xbc_v, slice(None), 0, SSM_D_INNER)
        yn_ref[...] = _gate_norm(y_s[...], xs, z_v[...], dx_ref[...], nw_ref[...])

    hb = _mod_norm(xa_ref[0], gpre_ref[0], sc_ref[0], sh_ref[0]).astype(BF16)
    z, xbc_raw, dt, da = _ssm_project(hb, wz_ref, wxbc_ref, wdt_ref, dtb_ref, alog_ref)
    z_s[...] = z
    dt_s[...] = dt
    da_s[...] = da
    for c in range(n_slabs):
        xpad[c, CONV_PAD:CONV_PAD + lt, :] = xbc_raw[:, c * LANES:(c + 1) * LANES]

    finish(yn_a, slice(0, q))
    scan(xbcc_s, zc_s, dtc_s, dac_s, yn_b, None)
    h_out_ref[...] = h_s[...]

    half = lt // 2
    for c in range(n_slabs):
        csl = slice(c * LANES, (c + 1) * LANES)
        for par in range(2):
            acc = cb_ref[:, csl]
            for k in range(SSM_CONV):
                rows = pl.ds(CONV_PAD - HIST + par + k, half, stride=2)
                acc = acc + xpad[c, rows, :] * cw_ref[k:k + 1, csl]
            xbc_s[c, pl.ds(par, half, stride=2), :] = _silu(acc)
        new_hist = xpad[c, CONV_PAD + lt - HIST:CONV_PAD + lt, :]
        conv_out_ref[0, :, csl] = new_hist
        xpad[c, CONV_PAD - HIST:CONV_PAD, :] = new_hist

    scan(xbc_s.at[:, 0:q, :], z_s.at[0:q], dt_s.at[0:q], da_s.at[0:q], yn_a, seq_start)
    finish(yn_b, slice(q, lt))


def _ssm_prompt(x, mod, gpre, gpost, w):
    b, l, d = x.shape
    q = SSM_CHUNK
    lt = 2 * q
    spq = l // lt
    n_blocks = b * spq

    def cur(s):
        blk = jnp.minimum(s, n_blocks - 1)
        return blk // spq, blk % spq

    def prev(s):
        blk = jnp.maximum(s - 1, 0)
        return blk // spq, blk % spq

    xa_spec = pl.BlockSpec((1, lt, d), lambda s: cur(s) + (0,))
    xc_spec = pl.BlockSpec((1, lt, d), lambda s: prev(s) + (0,))
    mod_a = [pl.BlockSpec((1, None, 1, d), functools.partial(lambda k, s: (cur(s)[0], k, 0, 0), 3 + k))
             for k in range(2)]
    mod_c = pl.BlockSpec((1, None, 1, d), lambda s: (prev(s)[0], 5, 0, 0))
    conv_spec = pl.BlockSpec((1, HIST, SSM_CONV_DIM), lambda s: (cur(s)[0], 0, 0))
    state_spec = pl.BlockSpec((1,) + _STATE_SHAPE, lambda s: (prev(s)[0], 0, 0, 0))
    weights = [w[n] for n in _SSM_WEIGHT_NAMES]
    return pl.pallas_call(
        functools.partial(_ssm_prompt_kernel, steps_per_seq=spq),
        out_shape=(jax.ShapeDtypeStruct(x.shape, F32),
                   jax.ShapeDtypeStruct((b, HIST, SSM_CONV_DIM), F32),
                   jax.ShapeDtypeStruct((b,) + _STATE_SHAPE, F32)),
        grid=(n_blocks + 1,),
        in_specs=[xa_spec, xc_spec] + mod_a + [mod_c, _const_spec((1, 1, d)), _const_spec((1, 1, d))]
        + [_const_spec(a.shape, a.dtype == BF16) for a in weights],
        out_specs=(xc_spec, conv_spec, state_spec),
        scratch_shapes=[
            pltpu.VMEM((SSM_CONV_DIM // LANES, CONV_PAD + lt, LANES), F32),
            pltpu.VMEM((lt, SSM_D_INNER), F32),
            pltpu.VMEM((SSM_CONV_DIM // LANES, lt, LANES), F32),
            pltpu.VMEM((lt, LANES), F32),
            pltpu.VMEM((lt, LANES), F32),
            pltpu.VMEM((q, SSM_D_INNER), F32),
            pltpu.VMEM((SSM_CONV_DIM // LANES, q, LANES), F32),
            pltpu.VMEM((q, LANES), F32),
            pltpu.VMEM((q, LANES), F32),
            pltpu.VMEM((q, SSM_D_INNER), F32),
            pltpu.VMEM((q, SSM_D_INNER), BF16),
            pltpu.VMEM((q, SSM_D_INNER), BF16),
            pltpu.VMEM((1,) + _STATE_SHAPE, F32),
        ],
        compiler_params=_params(1),
        name="ssm_sublayer",
    )(x, x, mod, mod, mod, gpre, gpost, *weights)


def _ssm_weights(in_w, conv_w, conv_b, dt_bias, a_log, d_skip, norm_w, out_w):
    pad = LANES - SSM_HEADS
    heads = np.arange(SSM_D_INNER) // SSM_HEAD_DIM
    e = (np.arange(LANES)[:, None] == heads[None, :]).astype(np.float32)
    return dict(
        wz=in_w[:, :SSM_D_INNER].astype(BF16),
        wxbc=in_w[:, SSM_D_INNER:SSM_D_INNER + SSM_CONV_DIM].astype(BF16),
        wdt=jnp.pad(in_w[:, SSM_D_INNER + SSM_CONV_DIM:], ((0, 0), (0, pad))).astype(BF16),
        conv_w=conv_w,
        conv_b=conv_b.reshape(1, SSM_CONV_DIM),
        dt_bias=jnp.pad(dt_bias, (0, pad)).reshape(1, LANES),
        a_log=jnp.pad(a_log, (0, pad)).reshape(1, LANES),
        d_x=jnp.repeat(d_skip, SSM_HEAD_DIM).reshape(1, SSM_D_INNER),
        norm_w=norm_w.reshape(1, SSM_D_INNER),
        wo=out_w.astype(BF16),
        e2=jnp.asarray(np.concatenate([e, e], axis=0), BF16),
    )


SOFTMAX_ROWS = 64


def _softmax_sink(logits, sink):
    mx = jnp.maximum(jnp.max(logits, axis=-1, keepdims=True), sink)
    e = jnp.exp(logits - mx)
    denom = jnp.sum(e, axis=-1, keepdims=True) + jnp.exp(sink - mx)
    return e * (1.0 / denom)


def _dedup(t):
    lane_lo = lax.broadcasted_iota(jnp.int32, (t.shape[0], LANES), 1) < HALF
    tiles = [jnp.where(lane_lo, t[:, (2 * i) * LANES:(2 * i + 1) * LANES],
                       t[:, (2 * i + 1) * LANES:(2 * i + 2) * LANES]) for i in range(ATTN_KV_HEADS // 2)]
    return jnp.concatenate(tiles, axis=1)


def _swa_prompt_kernel(sinks_ref, xa_ref, xc_ref, sh_ref, sc_ref, gt_ref, gpre_ref, gpost_ref, wq_ref, wk_ref,
                       wv_ref, bq_ref, bk_ref, bv_ref, bias_ref, wo_ref, bo_ref,
                       o_ref, kc_ref, vc_ref, qn_s, kn_s, vn_s, q_s, kbuf, vbuf, att_s, lg_s, p_s,
                       *, steps_per_seq):
    s = pl.program_id(0)
    tq = xa_ref.shape[1]
    n_tiles = 2 * ATTN_KV_HEADS
    pairs = ATTN_REP // 2

    @pl.when(s == 0)
    def _():
        for ref in (qn_s, kn_s, vn_s, kbuf, vbuf):
            ref[...] = jnp.zeros(ref.shape, ref.dtype)

    kbuf[0:WINDOW, :] = kbuf[tq:tq + WINDOW, :]
    vbuf[0:WINDOW, :] = vbuf[tq:tq + WINDOW, :]
    kbuf[WINDOW:WINDOW + tq, :] = kn_s[...]
    vbuf[WINDOW:WINDOW + tq, :] = vn_s[...]
    q_s[...] = qn_s[...]

    hb = _mod_norm(xa_ref[0], gpre_ref[0], sc_ref[0], sh_ref[0]).astype(BF16)
    half_q = ATTN_Q_DIM // 2

    def project_q(lo):
        qn_s[:, lo:lo + half_q] = ((_dot(hb, wq_ref[:, lo:lo + half_q]) + bq_ref[:, lo:lo + half_q])
                                   * (ATTN_HEAD_DIM ** -0.5)).astype(BF16)

    def project_kv(w_ref, b_ref, nxt, cache_ref):
        kv = _dot(hb, w_ref[...]) + b_ref[...]
        nxt[...] = kv.astype(BF16)
        cache_ref[0] = _dedup(kv[tq - WINDOW:, :])

    parts = [functools.partial(project_q, 0), functools.partial(project_q, half_q),
             functools.partial(project_kv, wk_ref, bk_ref, kn_s, kc_ref),
             functools.partial(project_kv, wv_ref, bv_ref, vn_s, vc_ref)]

    lane_lo = lax.broadcasted_iota(jnp.int32, (2 * WINDOW, LANES), 1) < HALF
    zero_b = jnp.zeros((), BF16)
    seq_first = jnp.where(((s + steps_per_seq - 1) % steps_per_seq) == 0, n_tiles, 0)
    n_q = tq // WINDOW
    for bi in range(n_q):
        r0 = bi * WINDOW
        qrows = slice(r0, r0 + WINDOW)
        first = seq_first if bi == 0 else 0
        lg, pb = lg_s.at[bi], p_s.at[bi]
        for g in range(ATTN_KV_HEADS):
            kd = kbuf[r0:r0 + 2 * WINDOW, g * LANES:(g + 1) * LANES]
            q2 = jnp.concatenate([q_s[qrows, (g * pairs + pr) * LANES:(g * pairs + pr + 1) * LANES]
                                  for pr in range(pairs)], axis=0)
            for half in range(2):
                kh = jnp.where(lane_lo, kd, zero_b) if half == 0 else jnp.where(lane_lo, zero_b, kd)
                t = 2 * g + half
                lg[t] = _dot_nt(q2, kh) + bias_ref[first + t]
        for part in parts[bi * len(parts) // n_q:(bi + 1) * len(parts) // n_q]:
            part()
        for t in range(n_tiles):
            for rc in range(2 * WINDOW // SOFTMAX_ROWS):
                rs = slice(rc * SOFTMAX_ROWS, (rc + 1) * SOFTMAX_ROWS)
                h = 4 * (t // 2) + (t % 2) + 2 * ((rc * SOFTMAX_ROWS) // WINDOW)
                pb[t, rs, :] = _softmax_sink(lg[t, rs, :], sinks_ref[h]).astype(BF16)
        for g in range(ATTN_KV_HEADS):
            vd = vbuf[r0:r0 + 2 * WINDOW, g * LANES:(g + 1) * LANES]
            acc = (_dot(pb[2 * g], jnp.where(lane_lo, vd, zero_b))
                   + _dot(pb[2 * g + 1], jnp.where(lane_lo, zero_b, vd)))
            for pr in range(pairs):
                pair = g * pairs + pr
                att_s[qrows, pair * LANES:(pair + 1) * LANES] = acc[pr * WINDOW:(pr + 1) * WINDOW].astype(BF16)
        out = _dot(att_s[qrows, :], wo_ref[...]) + bo_ref[...]
        o_ref[0, qrows, :] = xc_ref[0, qrows, :] + gt_ref[0] * _rms(out, gpost_ref[0])


def _swa_prompt(x, mod, gpre, gpost, w, bias, tq):
    b, l, d = x.shape
    kvw = ATTN_KV_HEADS * LANES
    pairs = ATTN_REP // 2
    n_tiles = 2 * ATTN_KV_HEADS
    spq = l // tq
    n_blocks = b * spq
    bias = bias.reshape(2, ATTN_KV_HEADS, pairs, 2, WINDOW, 2 * WINDOW).transpose(0, 1, 3, 2, 4, 5)
    bias = bias.reshape(2 * n_tiles, pairs * WINDOW, 2 * WINDOW)

    def cur(s):
        blk = jnp.minimum(s, n_blocks - 1)
        return blk // spq, blk % spq

    def prev(s):
        blk = jnp.maximum(s - 1, 0)
        return blk // spq, blk % spq

    xa_spec = pl.BlockSpec((1, tq, d), lambda s: cur(s) + (0,))
    xc_spec = pl.BlockSpec((1, tq, d), lambda s: prev(s) + (0,))
    mod_a = [pl.BlockSpec((1, None, 1, d), functools.partial(lambda k, s: (cur(s)[0], k, 0, 0), 3 + k))
             for k in range(2)]
    mod_c = pl.BlockSpec((1, None, 1, d), lambda s: (prev(s)[0], 5, 0, 0))
    cache_spec = pl.BlockSpec((1, WINDOW, ATTN_KV_DIM), lambda s: (cur(s)[0], 0, 0))
    consts = [w["wq"], w["wk_dup"], w["wv_dup"], w["bq"], w["bk_dup"], w["bv_dup"], bias, w["wo"], w["bo"]]
    n_q = tq // WINDOW
    return pl.pallas_call(
        functools.partial(_swa_prompt_kernel, steps_per_seq=spq),
        out_shape=(jax.ShapeDtypeStruct(x.shape, F32),
                   jax.ShapeDtypeStruct((b, WINDOW, ATTN_KV_DIM), F32),
                   jax.ShapeDtypeStruct((b, WINDOW, ATTN_KV_DIM), F32)),
        grid=(n_blocks + 1,),
        in_specs=[pl.BlockSpec(memory_space=pltpu.SMEM), xa_spec, xc_spec] + mod_a
        + [mod_c, _const_spec((1, 1, d)), _const_spec((1, 1, d))]
        + [_const_spec(a.shape, a.dtype == BF16) for a in consts],
        out_specs=(xc_spec, cache_spec, cache_spec),
        scratch_shapes=[
            pltpu.VMEM((tq, ATTN_Q_DIM), BF16),
            pltpu.VMEM((tq, kvw), BF16),
            pltpu.VMEM((tq, kvw), BF16),
            pltpu.VMEM((tq, ATTN_Q_DIM), BF16),
            pltpu.VMEM((WINDOW + tq, kvw), BF16),
            pltpu.VMEM((WINDOW + tq, kvw), BF16),
            pltpu.VMEM((tq, ATTN_Q_DIM), BF16),
            pltpu.VMEM((n_q, n_tiles, pairs * WINDOW, 2 * WINDOW), F32),
            pltpu.VMEM((n_q, n_tiles, pairs * WINDOW, 2 * WINDOW), BF16),
        ],
        compiler_params=_params(1),
        name="swa_prompt_sublayer",
    )(w["sinks"], x, x, mod, mod, mod, gpre, gpost, *consts)


def _swa_sample_kernel(x_ref, sh_ref, sc_ref, gt_ref, gpre_ref, gpost_ref, wq_ref, wk_ref, wv_ref,
                       bq_ref, bk_ref, bv_ref, ck_ref, cv_ref, bias_ref, sink_ref, wo_ref, bo_ref,
                       o_ref, ko_ref, vo_ref, q_s, kn_s, vn_s, att_s):
    x = x_ref[...]
    bt, lt, d = x.shape
    m = bt * lt
    nbuf = ck_ref.shape[1]
    wide = ATTN_KV_HEADS * ATTN_HEAD_DIM

    hb = _mod_norm(x, gpre_ref[...], sc_ref[...], sh_ref[...]).reshape(m, d).astype(BF16)
    q_s[...] = (_dot(hb, wq_ref[...]) + bq_ref[...]) * (ATTN_HEAD_DIM ** -0.5)
    kn_s[...] = _dot(hb, wk_ref[...]) + bk_ref[...]
    vn_s[...] = _dot(hb, wv_ref[...]) + bv_ref[...]
    zpad = jnp.zeros((bias_ref.shape[1] - nbuf - lt, ATTN_KV_DIM), F32)

    def seq(bi, carry):
        rows = pl.ds(pl.multiple_of(bi * lt, lt), lt)
        kn = kn_s[rows, :]
        vn = vn_s[rows, :]
        ck = ck_ref[bi]
        cv = cv_ref[bi]
        kfull = jnp.concatenate([ck, kn, zpad], axis=0).astype(BF16)
        vfull = jnp.concatenate([cv, vn, zpad], axis=0).astype(BF16)
        ko_ref[bi, 0:nbuf - lt, :] = ck[lt:, :]
        vo_ref[bi, 0:nbuf - lt, :] = cv[lt:, :]
        ko_ref[bi, nbuf - lt:nbuf, :] = kn
        vo_ref[bi, nbuf - lt:nbuf, :] = vn
        qbig = jnp.concatenate([q_s[rows, h * wide:(h + 1) * wide] for h in range(ATTN_HEADS)], axis=0)
        logits = _dot_nt(qbig.astype(BF16), kfull) + bias_ref[...]
        p = _softmax_sink(logits, sink_ref[:, 0:1]).astype(BF16)
        res = _dot(p, vfull)
        att_s[rows, :] = jnp.concatenate([res[h * lt:(h + 1) * lt, :] for h in range(ATTN_HEADS)], axis=1)
        return carry

    lax.fori_loop(0, bt, seq, 0, unroll=8)

    out = (_dot(att_s[...].astype(BF16), wo_ref[...]) + bo_ref[...]).reshape(bt, lt, d)
    o_ref[...] = x + gt_ref[...] * _rms(out, gpost_ref[...])


def _swa_sample(x, mod, gpre, gpost, w, bias, cache_k, cache_v, bt):
    b, lt, d = x.shape
    nbuf = cache_k.shape[1]
    m = bt * lt
    wide = ATTN_KV_HEADS * ATTN_HEAD_DIM
    big = ATTN_HEADS * wide
    keys = 2 * WINDOW
    xspec = pl.BlockSpec((bt, lt, d), lambda i: (i, 0, 0))
    cache_spec = pl.BlockSpec((bt, nbuf, ATTN_KV_DIM), lambda i: (i, 0, 0))
    bias_s = bias[0, :, :lt, :].reshape(ATTN_HEADS * lt, keys)
    sink_col = jnp.broadcast_to(jnp.repeat(w["sinks"], lt)[:, None], (ATTN_HEADS * lt, LANES))
    mods = [pl.BlockSpec((bt, None, 1, D_MODEL), functools.partial(lambda k, i: (i, k, 0, 0), 3 + k))
            for k in range(3)]
    pre = [w["wq_big"], w["wk"], w["wv"], w["bq_big"], w["bk"], w["bv"]]
    post = [bias_s, sink_col, w["wo_big"], w["bo"]]
    return pl.pallas_call(
        _swa_sample_kernel,
        out_shape=(jax.ShapeDtypeStruct(x.shape, F32),
                   jax.ShapeDtypeStruct(cache_k.shape, F32),
                   jax.ShapeDtypeStruct(cache_v.shape, F32)),
        grid=(b // bt,),
        in_specs=[xspec] + mods + [_const_spec((1, 1, d)), _const_spec((1, 1, d))]
        + [_const_spec(a.shape, a.dtype == BF16) for a in pre] + [cache_spec, cache_spec]
        + [_const_spec(a.shape, a.dtype == BF16) for a in post],
        out_specs=(xspec, cache_spec, cache_spec),
        scratch_shapes=[
            pltpu.VMEM((m, big), F32),
            pltpu.VMEM((m, ATTN_KV_DIM), F32),
            pltpu.VMEM((m, ATTN_KV_DIM), F32),
            pltpu.VMEM((m, big), F32),
        ],
        compiler_params=_params(1),
        name="swa_sample_sublayer",
    )(x, mod, mod, mod, gpre, gpost, *pre, cache_k, cache_v, *post)


def _swa_weights(qkv_w, qkv_b, sinks, o_w, o_b):
    d = qkv_w.shape[0]
    wq = qkv_w[:, :ATTN_Q_DIM]
    wk = qkv_w[:, ATTN_Q_DIM:ATTN_Q_DIM + ATTN_KV_DIM]
    wv = qkv_w[:, ATTN_Q_DIM + ATTN_KV_DIM:]
    bq = qkv_b[:ATTN_Q_DIM]
    bk = qkv_b[ATTN_Q_DIM:ATTN_Q_DIM + ATTN_KV_DIM]
    bv = qkv_b[ATTN_Q_DIM + ATTN_KV_DIM:]

    def dup(a):
        a4 = a.reshape(a.shape[:-1] + (ATTN_KV_HEADS, 1, ATTN_HEAD_DIM))
        a4 = jnp.broadcast_to(a4, a.shape[:-1] + (ATTN_KV_HEADS, 2, ATTN_HEAD_DIM))
        return a4.reshape(a.shape[:-1] + (ATTN_KV_HEADS * LANES,))

    own = jnp.asarray((np.arange(ATTN_HEADS)[:, None] // ATTN_REP == np.arange(ATTN_KV_HEADS)[None, :])
                      .astype(np.float32))
    wq_big = (wq.reshape(d, ATTN_HEADS, 1, ATTN_HEAD_DIM) * own[None, :, :, None]).reshape(d, -1)
    bq_big = (bq.reshape(ATTN_HEADS, 1, ATTN_HEAD_DIM) * own[:, :, None]).reshape(1, -1)
    wo_big = (o_w.reshape(ATTN_HEADS, 1, ATTN_HEAD_DIM, d) * own[:, :, None, None]).reshape(-1, d)
    return dict(
        sinks=sinks,
        wq=wq.astype(BF16), wk_dup=dup(wk).astype(BF16), wv_dup=dup(wv).astype(BF16),
        bq=bq.reshape(1, -1), bk_dup=dup(bk).reshape(1, -1), bv_dup=dup(bv).reshape(1, -1),
        wo=o_w.astype(BF16), bo=o_b.reshape(1, d),
        wq_big=wq_big.astype(BF16), bq_big=bq_big, wk=wk.astype(BF16), wv=wv.astype(BF16),
        bk=bk.reshape(1, -1), bv=bv.reshape(1, -1), wo_big=wo_big.astype(BF16),
    )


def _trunk(x, mod_all, state_ssm, state_conv, cache_k, cache_v, norm_pre, norm_post, ffn_in, ffn_out,
           ssm_w, swa_w, bias, tiles):
    bt, lt, n_sub = tiles["ffn"]
    b, l, d = x.shape
    sample = state_ssm is not None
    outs = {}
    for i in range(2):
        mod = mod_all[i].reshape(b, N_SUB * 3, 1, d)
        gpre = norm_pre[i].reshape(N_SUB, 1, 1, d)
        gpost = norm_post[i].reshape(N_SUB, 1, 1, d)
        x = _ffn(x, mod, 0, gpre[0], gpost[0], ffn_in, ffn_out, i, 0, bt, lt, n_sub)
        if i == 0:
            if sample:
                x, conv_new, ssm_new = _ssm_sample(x, mod, gpre[1], gpost[1], ssm_w, state_conv[0], state_ssm[0],
                                                   tiles["ssm"])
            else:
                x, conv_new, ssm_new = _ssm_prompt(x, mod, gpre[1], gpost[1], ssm_w)
            outs["conv"] = conv_new[None]
            outs["ssm"] = ssm_new[None]
        else:
            if sample:
                x, k_new, v_new = _swa_sample(x, mod, gpre[1], gpost[1], swa_w, bias,
                                              cache_k[0].reshape(b, -1, ATTN_KV_DIM),
                                              cache_v[0].reshape(b, -1, ATTN_KV_DIM), tiles["swa"])
            else:
                x, k_new, v_new = _swa_prompt(x, mod, gpre[1], gpost[1], swa_w, bias, tiles["swa"])
            shape = (1, b, -1, ATTN_KV_HEADS, ATTN_HEAD_DIM)
            outs["k"] = k_new.reshape(shape)
            outs["v"] = v_new.reshape(shape)
        x = _ffn(x, mod, 2, gpre[2], gpost[2], ffn_in, ffn_out, i, 1, bt, lt, n_sub)
    return x, outs["ssm"], outs["conv"], outs["k"], outs["v"]


def kernel(x_prompt, x_sample, state_ssm, state_conv, cache_k, cache_v, c_prompt, c_sample, ada_w, ada_b, norm_pre, norm_post, ffn_w_in, ffn_w_out, ssm_in_w, ssm_conv_w, ssm_conv_b, ssm_dt_bias, ssm_a_log, ssm_d, ssm_norm_w, ssm_out_w, attn_qkv_w, attn_qkv_b, attn_sinks, attn_o_w, attn_o_b, rel_bias):
    nb = x_prompt.shape[0]
    mod_all = _ada(jnp.concatenate([c_prompt, c_sample], axis=0), ada_w, ada_b)
    bias = _bias_table(rel_bias)
    ffn_in = ffn_w_in.astype(BF16)
    ffn_out = ffn_w_out.astype(BF16)
    ssm_w = _ssm_weights(ssm_in_w[0], ssm_conv_w[0], ssm_conv_b[0], ssm_dt_bias[0], ssm_a_log[0], ssm_d[0],
                         ssm_norm_w[0], ssm_out_w[0])
    swa_w = _swa_weights(attn_qkv_w[0], attn_qkv_b[0], attn_sinks[0], attn_o_w[0], attn_o_b[0])
    common = (norm_pre, norm_post, ffn_in, ffn_out, ssm_w, swa_w, bias)

    p_tiles = dict(ffn=(1, 1024, 2), swa=512)
    y_p, ssm_p, conv_p, k_p, v_p = _trunk(x_prompt, mod_all[:, :nb], None, None, None, None, *common, p_tiles)
    ns, ls = x_sample.shape[:2]
    s_tiles = dict(ffn=(min(64, ns), ls, 1), ssm=min(4, ns), swa=min(16, ns))
    y_s, ssm_s, conv_s, k_s, v_s = _trunk(x_sample, mod_all[:, nb:], state_ssm, state_conv, cache_k, cache_v,
                                          *common, s_tiles)
    return (y_p, y_s, ssm_p, conv_p, k_p, v_p, ssm_s, conv_s, k_s, v_s)
```

```python
import functools
import math

import numpy as np
import jax
import jax.numpy as jnp
from jax import lax
from jax.experimental import pallas as pl
from jax.experimental.pallas import tpu as pltpu

F32 = jnp.float32
BF16 = jnp.bfloat16

D_MODEL = 1024
N_SUB = 3
RMS_EPS = 1e-6
FFN_RES = 0.5
D_FF = 2816

SSM_D_INNER = 2048
SSM_HEAD_DIM = 64
SSM_HEADS = 32
SSM_GROUPS = 4
SSM_HPG = 8
SSM_STATE = 128
SSM_CONV = 4
SSM_CHUNK = 128
SSM_GN = SSM_GROUPS * SSM_STATE
SSM_CONV_DIM = SSM_D_INNER + 2 * SSM_GN
SSM_GROUP_WIDTH = SSM_HPG * SSM_HEAD_DIM

ATTN_HEAD_DIM = 64
ATTN_HEADS = 16
ATTN_KV_HEADS = 4
ATTN_REP = 4
WINDOW = 128
REL_BUCKETS = 32
ATTN_Q_DIM = ATTN_HEADS * ATTN_HEAD_DIM
ATTN_KV_DIM = ATTN_KV_HEADS * ATTN_HEAD_DIM

LANES = 128
SUBLANES = 8
HALF = LANES // 2
VMEM_LIMIT_BYTES = 56 * 1024 * 1024
CONV_PAD = SUBLANES


def _dot(a, b):
    return jnp.dot(a, b, preferred_element_type=F32)


def _dot_nt(a, b):
    return lax.dot_general(a, b, (((1,), (1,)), ((), ())), preferred_element_type=F32)


def _dot_tn(a, b):
    return lax.dot_general(a, b, (((0,), (0,)), ((), ())), preferred_element_type=F32)


def _silu(x):
    h = 0.5 * x
    return h * jnp.tanh(h) + h


def _rms(x, g):
    return x * lax.rsqrt(jnp.mean(x * x, axis=-1, keepdims=True) + RMS_EPS) * g


def _mod_norm(x, g, scale, shift):
    return _rms(x, g) * (1.0 + scale) + shift


def _split3(x):
    hi = x.astype(BF16)
    r1 = x - hi.astype(F32)
    mid = r1.astype(BF16)
    lo = (r1 - mid.astype(F32)).astype(BF16)
    return hi, mid, lo


def _const_spec(shape, single_buffer=False):
    nd = len(shape)
    kw = {"pipeline_mode": pl.Buffered(1)} if single_buffer else {}
    return pl.BlockSpec(shape, lambda *_: (0,) * nd, **kw)


def _params(n_grid, flags=None):
    return pltpu.CompilerParams(
        dimension_semantics=("arbitrary",) * n_grid,
        vmem_limit_bytes=VMEM_LIMIT_BYTES,
        flags=flags,
    )


def _ada_kernel(c_ref, w_ref, b_ref, o_ref):
    cs = _silu(c_ref[...]).astype(BF16)
    o_ref[0] = _dot(cs, w_ref[0].astype(BF16)) + b_ref[0]


def _ada(c_all, ada_w, ada_b, tn=1152):
    depth, d, n = ada_w.shape
    bc = c_all.shape[0]
    return pl.pallas_call(
        _ada_kernel,
        out_shape=jax.ShapeDtypeStruct((depth, bc, n), F32),
        grid=(depth, n // tn),
        in_specs=[
            pl.BlockSpec((bc, d), lambda l, j: (0, 0)),
            pl.BlockSpec((1, d, tn), lambda l, j: (l, 0, j)),
            pl.BlockSpec((1, 1, tn), lambda l, j: (l, 0, j)),
        ],
        out_specs=pl.BlockSpec((1, bc, tn), lambda l, j: (l, 0, j)),
        compiler_params=_params(2),
        name="ada_mod",
    )(c_all, ada_w, ada_b.reshape(depth, 1, n))


def _t5_bucket_table():
    i = np.arange(WINDOW)[:, None]
    j = np.arange(2 * WINDOW)[None, :]
    dist = i + WINDOW - j
    exact = REL_BUCKETS // 2
    df = np.maximum(dist, 1).astype(np.float32)
    large = exact + (np.log(df / np.float32(exact)) / np.float32(math.log(WINDOW / exact))
                     * np.float32(REL_BUCKETS - exact)).astype(np.int32)
    large = np.minimum(large, REL_BUCKETS - 1)
    bucket = np.where(dist < exact, dist, large)
    valid = (dist >= 0) & (dist <= WINDOW)
    return np.where(valid, bucket, -1).astype(np.int32)


def _bias_kernel(rb_ref, idx_ref, o_ref):
    h = pl.program_id(0)
    idx = idx_ref[...]
    acc = jnp.full(idx.shape, -jnp.inf, F32)
    for b in range(REL_BUCKETS):
        acc = jnp.where(idx == b, rb_ref[b, h], acc)
    o_ref[0, 0] = acc
    col = lax.broadcasted_iota(jnp.int32, idx.shape, 1)
    o_ref[1, 0] = jnp.where(col < WINDOW, -jnp.inf, acc)


def _bias_table(rel_bias):
    idx = jnp.asarray(_t5_bucket_table())
    return pl.pallas_call(
        _bias_kernel,
        out_shape=jax.ShapeDtypeStruct((2, ATTN_HEADS, WINDOW, 2 * WINDOW), F32),
        grid=(ATTN_HEADS,),
        in_specs=[
            pl.BlockSpec(memory_space=pltpu.SMEM),
            pl.BlockSpec((WINDOW, 2 * WINDOW), lambda h: (0, 0)),
        ],
        out_specs=pl.BlockSpec((2, 1, WINDOW, 2 * WINDOW), lambda h: (0, h, 0, 0)),
        compiler_params=_params(1),
        name="rel_bias_table",
    )(rel_bias, idx)


MXU_TILE = 256
FF_CHUNKS = ((0, D_FF),)


def _ffn_kernel(x_ref, sh_ref, sc_ref, gt_ref, gpre_ref, gpost_ref, win_ref, wout_ref, o_ref, *, ff_chunks, n_sub):
    bt, lt, d = x_ref.shape

    def sub_slices(s):
        if bt == 1:
            return slice(None), slice(s * (lt // n_sub), (s + 1) * (lt // n_sub))
        return slice(s * (bt // n_sub), (s + 1) * (bt // n_sub)), slice(None)

    def pre(s):
        bs, ls = sub_slices(s)
        x = x_ref[bs, ls, :]
        h = _mod_norm(x, gpre_ref[...], sc_ref[bs], sh_ref[bs])
        return h.reshape(x.shape[0] * x.shape[1], d).astype(BF16)

    def post(s, acc):
        bs, ls = sub_slices(s)
        x = x_ref[bs, ls, :]
        o_ref[bs, ls, :] = x + FFN_RES * gt_ref[bs] * _rms(acc.reshape(x.shape), gpost_ref[...])

    hb = pre(0)
    prev = None
    for s in range(n_sub):
        acc = None
        hb_next = None
        for c, (lo, hi) in enumerate(ff_chunks):
            g = _dot(hb, win_ref[:, lo:hi])
            u = _dot(hb, win_ref[:, D_FF + lo:D_FF + hi])
            a = (_silu(g) * u).astype(BF16)
            part = _dot(a, wout_ref[lo:hi, :])
            acc = part if acc is None else acc + part
            if c == 0:
                if prev is not None:
                    post(s - 1, prev)
                if s + 1 < n_sub:
                    hb_next = pre(s + 1)
        prev = acc
        hb = hb_next
    post(n_sub - 1, prev)


def _mod_specs(bt, sub):
    return [pl.BlockSpec((bt, None, 1, D_MODEL), functools.partial(lambda k, i, j: (i, k, 0, 0), sub * 3 + k))
            for k in range(3)]


def _ffn(x, mod, sub, gpre, gpost, w_in, w_out, layer, which, bt, lt, n_sub):
    b, l, d = x.shape
    xspec = pl.BlockSpec((bt, lt, d), lambda i, j: (i, j, 0))
    wspec = [pl.BlockSpec((None, None) + w.shape[2:], lambda i, j: (layer, which, 0, 0),
                          pipeline_mode=pl.Buffered(1)) for w in (w_in, w_out)]
    return pl.pallas_call(
        functools.partial(_ffn_kernel, ff_chunks=FF_CHUNKS, n_sub=n_sub),
        out_shape=jax.ShapeDtypeStruct(x.shape, F32),
        grid=(b // bt, l // lt),
        in_specs=[xspec] + _mod_specs(bt, sub) + [
            _const_spec((1, 1, d)), _const_spec((1, 1, d))] + wspec,
        out_specs=xspec,
        compiler_params=_params(2),
        name="ffn_sublayer",
    )(x, mod, mod, mod, gpre, gpost, w_in, w_out)


def _chunk_consts(qc, seg):
    r = lax.broadcasted_iota(jnp.int32, (qc, qc), 0)
    c = lax.broadcasted_iota(jnp.int32, (qc, qc), 1)
    seg_shift = seg.bit_length() - 1
    same = jnp.right_shift(r, seg_shift) == jnp.right_shift(c, seg_shift)
    causal = same & (r >= c)
    tri = jnp.where(causal, 1.0, 0.0)
    upper = jnp.where(same & (c > r), 1.0, 0.0)
    tu = jnp.concatenate([tri, upper], axis=0).astype(BF16)
    er = lax.broadcasted_iota(jnp.int32, (LANES, LANES), 0)
    ec = lax.broadcasted_iota(jnp.int32, (LANES, LANES), 1)
    eye = jnp.where(er == ec, 1.0, 0.0).astype(BF16)
    lane_lo = lax.broadcasted_iota(jnp.int32, (qc, LANES), 1) < HALF
    return tu, eye, causal, lane_lo


def _ssm_project(hb, wz_ref, wxbc_ref, wdt_ref, dtb_ref, alog_ref):
    z = _dot(hb, wz_ref[...])
    xbc_raw = _dot(hb, wxbc_ref[...])
    dt_raw = _dot(hb, wdt_ref[...]) + dtb_ref[...]
    dt = jnp.maximum(dt_raw, 0.0) + jnp.log1p(jnp.exp(-jnp.abs(dt_raw)))
    return z, xbc_raw, dt, dt * (-jnp.exp(alog_ref[...]))


def _gate_norm(y, xs, z, dx, nw):
    yg = (y + xs * dx) * _silu(z)
    parts = []
    for g in range(SSM_GROUPS):
        v = yg[:, g * SSM_GROUP_WIDTH:(g + 1) * SSM_GROUP_WIDTH]
        parts.append(v * lax.rsqrt(jnp.mean(v * v, axis=-1, keepdims=True) + RMS_EPS))
    return (jnp.concatenate(parts, axis=1) * nw).astype(BF16)


def _cols(ref, rows, c0, c1):
    if len(ref.shape) == 2:
        return ref[rows, c0:c1]
    return jnp.concatenate([ref[c, rows, :] for c in range(c0 // LANES, c1 // LANES)], axis=1)


def _ssd_chunk(r0, qc, seg, s_first, *, xbc_s, dt_s, da_s, y_s, e2_ref, h_in, h_out, consts, reset=None):
    tu, eye, causal, lane_lo = consts
    rows = pl.ds(r0, qc)
    hi, mid, lo = _split3(da_s[rows, :])
    cs2 = _dot(tu, hi) + _dot(tu, mid) + _dot(tu, lo)
    a_cs = cs2[:qc]
    ea = jnp.exp(a_cs)
    dte = jnp.exp(cs2[qc:])

    stack = jnp.concatenate([dt_s[rows, :], ea, dte], axis=0)
    s_hi = stack.astype(BF16)
    s_lo = (stack - s_hi.astype(F32)).astype(BF16)
    sx = _dot(jnp.concatenate([s_hi, s_lo], axis=1), e2_ref[...])
    dt_x, ea_x, dte_x = sx[:qc], sx[qc:2 * qc], sx[2 * qc:]

    xdt = _cols(xbc_s, rows, 0, SSM_D_INNER) * dt_x
    xdt_b = xdt.astype(BF16)
    xd_b = (xdt * dte_x).astype(BF16)
    bm = _cols(xbc_s, rows, SSM_D_INNER, SSM_D_INNER + SSM_GN).astype(BF16)
    cm = _cols(xbc_s, rows, SSM_D_INNER + SSM_GN, SSM_CONV_DIM).astype(BF16)

    a_hi, a_mid, a_lo = _split3(a_cs)
    a_cs_t = _dot_nt(eye, a_hi) + _dot_nt(eye, a_mid) + _dot_nt(eye, a_lo)

    zero_b = jnp.zeros((), BF16)
    for g in range(SSM_GROUPS):
        gsl = slice(g * SSM_STATE, (g + 1) * SSM_STATE)
        cb = _dot_nt(cm[:, gsl], bm[:, gsl])
        for pr in range(SSM_HPG // 2):
            h0 = g * SSM_HPG + 2 * pr
            psl = slice((h0 // 2) * LANES, (h0 // 2 + 1) * LANES)
            xp = xdt_b[:, psl]
            acc = None
            for half in range(2):
                h = h0 + half
                seg_sum = a_cs[:, h:h + 1] - a_cs_t[h:h + 1, :]
                decay = jnp.exp(jnp.where(causal, seg_sum, -jnp.inf))
                w = (decay * cb).astype(BF16)
                xh = jnp.where(lane_lo if half == 0 else jnp.logical_not(lane_lo), xp, zero_b)
                o = _dot(w, xh)
                acc = o if acc is None else acc + o
            y_s[rows, psl] = acc

    for t in range(qc // seg):
        tr = slice(t * seg, (t + 1) * seg)
        trows = pl.ds(r0 + t * seg, seg)
        last = t * seg + seg - 1
        for g in range(SSM_GROUPS):
            gsl = slice(g * SSM_STATE, (g + 1) * SSM_STATE)
            csl = slice(g * SSM_GROUP_WIDTH, (g + 1) * SSM_GROUP_WIDTH)
            hsl = slice(g * SSM_HPG, (g + 1) * SSM_HPG)
            hg = h_in[s_first + t, hsl].reshape(SSM_GROUP_WIDTH, SSM_STATE)
            if reset is not None:
                hg = jnp.where(reset, 0.0, hg)
            y_off = _dot_nt(cm[tr, gsl], hg.astype(BF16)) * ea_x[tr, csl]
            y_s[trows, csl] = y_s[trows, csl] + y_off
            upd = _dot_tn(xd_b[tr, csl], bm[tr, gsl])
            cdec = jnp.concatenate(
                [jnp.broadcast_to(ea[last:last + 1, g * SSM_HPG + r:g * SSM_HPG + r + 1],
                                  (SSM_HEAD_DIM, SSM_STATE)) for r in range(SSM_HPG)], axis=0)
            h_out[s_first + t, hsl] = (hg * cdec + upd).reshape(SSM_HPG, SSM_HEAD_DIM, SSM_STATE)


HIST = SSM_CONV - 1
_SSM_WEIGHT_NAMES = ("wz", "wxbc", "wdt", "conv_w", "conv_b", "dt_bias", "a_log", "d_x", "norm_w", "wo", "e2")
_STATE_SHAPE = (SSM_HEADS, SSM_HEAD_DIM, SSM_STATE)


def _conv_silu(xpad, cw_ref, cb_ref, lt):
    conv = cb_ref[...]
    for k in range(SSM_CONV):
        conv = conv + xpad[:, CONV_PAD - HIST + k:CONV_PAD - HIST + k + lt, :] * cw_ref[k:k + 1, :]
    return _silu(conv)


def _ssm_sample_kernel(x_ref, sh_ref, sc_ref, gt_ref, gpre_ref, gpost_ref, wz_ref, wxbc_ref, wdt_ref, cw_ref,
                       cb_ref, dtb_ref, alog_ref, dx_ref, nw_ref, wo_ref, e2_ref, conv_in_ref, h0_ref,
                       o_ref, conv_out_ref, h_out_ref, xpad, xbc_s, dt_s, da_s, y_s):
    x = x_ref[...]
    bt, lt, d = x.shape
    m = bt * lt
    hb = _mod_norm(x, gpre_ref[...], sc_ref[...], sh_ref[...]).reshape(m, d).astype(BF16)
    z, xbc_raw, dt, da = _ssm_project(hb, wz_ref, wxbc_ref, wdt_ref, dtb_ref, alog_ref)
    dt_s[...] = dt
    da_s[...] = da
    xpad[:, CONV_PAD:CONV_PAD + lt, :] = xbc_raw.reshape(bt, lt, SSM_CONV_DIM)
    xpad[:, CONV_PAD - HIST:CONV_PAD, :] = conv_in_ref[...]
    xbc_s[...] = _conv_silu(xpad, cw_ref, cb_ref, lt).reshape(m, SSM_CONV_DIM)
    conv_out_ref[...] = xpad[:, CONV_PAD + lt - HIST:CONV_PAD + lt, :]
    _ssd_chunk(0, m, lt, 0, xbc_s=xbc_s, dt_s=dt_s, da_s=da_s, y_s=y_s, e2_ref=e2_ref,
               h_in=h0_ref, h_out=h_out_ref, consts=_chunk_consts(m, lt))
    yn = _gate_norm(y_s[...], xbc_s[:, 0:SSM_D_INNER], z, dx_ref[...], nw_ref[...])
    out = _dot(yn, wo_ref[...]).reshape(bt, lt, d)
    o_ref[...] = x + gt_ref[...] * _rms(out, gpost_ref[...])


def _ssm_sample(x, mod, gpre, gpost, w, conv_in, h0, bt):
    b, lt, d = x.shape
    m = bt * lt
    xspec = pl.BlockSpec((bt, lt, d), lambda i: (i, 0, 0))
    conv_spec = pl.BlockSpec((bt, HIST, SSM_CONV_DIM), lambda i: (i, 0, 0))
    state_spec = pl.BlockSpec((bt,) + _STATE_SHAPE, lambda i: (i, 0, 0, 0))
    mods = [pl.BlockSpec((bt, None, 1, d), functools.partial(lambda k, i: (i, k, 0, 0), 3 + k)) for k in range(3)]
    weights = [w[n] for n in _SSM_WEIGHT_NAMES]
    return pl.pallas_call(
        _ssm_sample_kernel,
        out_shape=(jax.ShapeDtypeStruct(x.shape, F32),
                   jax.ShapeDtypeStruct((b, HIST, SSM_CONV_DIM), F32),
                   jax.ShapeDtypeStruct((b,) + _STATE_SHAPE, F32)),
        grid=(b // bt,),
        in_specs=[xspec] + mods + [_const_spec((1, 1, d)), _const_spec((1, 1, d))]
        + [_const_spec(a.shape, a.dtype == BF16) for a in weights] + [conv_spec, state_spec],
        out_specs=(xspec, conv_spec, state_spec),
        scratch_shapes=[
            pltpu.VMEM((bt, CONV_PAD + lt, SSM_CONV_DIM), F32),
            pltpu.VMEM((m, SSM_CONV_DIM), F32),
            pltpu.VMEM((m, LANES), F32),
            pltpu.VMEM((m, LANES), F32),
            pltpu.VMEM((m, SSM_D_INNER), F32),
        ],
        compiler_params=_params(1),
        name="ssm_sublayer_state",
    )(x, mod, mod, mod, gpre, gpost, *weights, conv_in, h0)


def _ssm_prompt_kernel(xa_ref, xc_ref, sh_ref, sc_ref, gt_ref, gpre_ref, gpost_ref, wz_ref, wxbc_ref, wdt_ref,
                       cw_ref, cb_ref, dtb_ref, alog_ref, dx_ref, nw_ref, wo_ref, e2_ref,
                       o_ref, conv_out_ref, h_out_ref,
                       xpad, z_s, xbc_s, dt_s, da_s, zc_s, xbcc_s, dtc_s, dac_s, y_s, yn_s, ynp_s, h_s,
                       *, steps_per_seq, n_blocks):
    s = pl.program_id(0)
    q = SSM_CHUNK
    lt = 2 * q
    seq_start = (s % steps_per_seq) == 0
    scan_start = ((s + steps_per_seq - 1) % steps_per_seq) == 0
    n_slabs = SSM_CONV_DIM // LANES

    @pl.when(s == 0)
    def _():
        for ref in (z_s, xbc_s, dt_s, da_s, yn_s, h_s):
            ref[...] = jnp.zeros(ref.shape, ref.dtype)

    @pl.when(seq_start)
    def _():
        xpad[:, CONV_PAD - HIST:CONV_PAD, :] = jnp.zeros((n_slabs, HIST, LANES), F32)

    ynp_s[...] = yn_s[...]
    zc_s[...] = z_s[...]
    xbcc_s[...] = xbc_s[...]
    dtc_s[...] = dt_s[...]
    dac_s[...] = da_s[...]
    consts = _chunk_consts(q, q)

    def scan(t, reset):
        rows = slice(t * q, (t + 1) * q)
        xbc_v = xbcc_s.at[:, rows, :]
        _ssd_chunk(0, q, q, 0, xbc_s=xbc_v, dt_s=dtc_s.at[rows], da_s=dac_s.at[rows], y_s=y_s, e2_ref=e2_ref,
                   h_in=h_s, h_out=h_s, consts=consts, reset=reset)
        xs = _cols(xbc_v, slice(None), 0, SSM_D_INNER)
        yn_s[rows, :] = _gate_norm(y_s[...], xs, zc_s[rows, :], dx_ref[...], nw_ref[...])

    out = _dot(ynp_s[...], wo_ref[...])
    o_ref[0] = xc_ref[0] + gt_ref[0] * _rms(out, gpost_ref[0])

    hb = _mod_norm(xa_ref[0], gpre_ref[0], sc_ref[0], sh_ref[0]).astype(BF16)
    z, xbc_raw, dt, da = _ssm_project(hb, wz_ref, wxbc_ref, wdt_ref, dtb_ref, alog_ref)
    z_s[...] = z
    dt_s[...] = dt
    da_s[...] = da
    for c in range(n_slabs):
        xpad[c, CONV_PAD:CONV_PAD + lt, :] = xbc_raw[:, c * LANES:(c + 1) * LANES]

    scan(0, scan_start)

    half = lt // 2
    for c in range(n_slabs):
        csl = slice(c * LANES, (c + 1) * LANES)
        for par in range(2):
            acc = cb_ref[:, csl]
            for k in range(SSM_CONV):
                rows = pl.ds(CONV_PAD - HIST + par + k, half, stride=2)
                acc = acc + xpad[c, rows, :] * cw_ref[k:k + 1, csl]
            xbc_s[c, pl.ds(par, half, stride=2), :] = _silu(acc)
        new_hist = xpad[c, CONV_PAD + lt - HIST:CONV_PAD + lt, :]
        conv_out_ref[0, :, csl] = new_hist
        xpad[c, CONV_PAD - HIST:CONV_PAD, :] = new_hist

    scan(1, None)

    @pl.when(s <= n_blocks)
    def _():
        h_out_ref[...] = h_s[...]


def _ssm_prompt(x, mod, gpre, gpost, w):
    b, l, d = x.shape
    q = SSM_CHUNK
    lt = 2 * q
    spq = l // lt
    n_blocks = b * spq

    def lagged(lag):
        def index(s):
            blk = jnp.clip(s - lag, 0, n_blocks - 1)
            return blk // spq, blk % spq
        return index

    cur, prev, prev2 = lagged(0), lagged(1), lagged(2)
    xa_spec = pl.BlockSpec((1, lt, d), lambda s: cur(s) + (0,))
    xc_spec = pl.BlockSpec((1, lt, d), lambda s: prev2(s) + (0,))
    mod_a = [pl.BlockSpec((1, None, 1, d), functools.partial(lambda k, s: (cur(s)[0], k, 0, 0), 3 + k))
             for k in range(2)]
    mod_c = pl.BlockSpec((1, None, 1, d), lambda s: (prev2(s)[0], 5, 0, 0))
    conv_spec = pl.BlockSpec((1, HIST, SSM_CONV_DIM), lambda s: (cur(s)[0], 0, 0))
    state_spec = pl.BlockSpec((1,) + _STATE_SHAPE, lambda s: (prev(s)[0], 0, 0, 0))
    weights = [w[n] for n in _SSM_WEIGHT_NAMES]
    return pl.pallas_call(
        functools.partial(_ssm_prompt_kernel, steps_per_seq=spq, n_blocks=n_blocks),
        out_shape=(jax.ShapeDtypeStruct(x.shape, F32),
                   jax.ShapeDtypeStruct((b, HIST, SSM_CONV_DIM), F32),
                   jax.ShapeDtypeStruct((b,) + _STATE_SHAPE, F32)),
        grid=(n_blocks + 2,),
        in_specs=[xa_spec, xc_spec] + mod_a + [mod_c, _const_spec((1, 1, d)), _const_spec((1, 1, d))]
        + [_const_spec(a.shape, a.dtype == BF16) for a in weights],
        out_specs=(xc_spec, conv_spec, state_spec),
        scratch_shapes=[
            pltpu.VMEM((SSM_CONV_DIM // LANES, CONV_PAD + lt, LANES), F32),
            pltpu.VMEM((lt, SSM_D_INNER), F32),
            pltpu.VMEM((SSM_CONV_DIM // LANES, lt, LANES), F32),
            pltpu.VMEM((lt, LANES), F32),
            pltpu.VMEM((lt, LANES), F32),
            pltpu.VMEM((lt, SSM_D_INNER), F32),
            pltpu.VMEM((SSM_CONV_DIM // LANES, lt, LANES), F32),
            pltpu.VMEM((lt, LANES), F32),
            pltpu.VMEM((lt, LANES), F32),
            pltpu.VMEM((q, SSM_D_INNER), F32),
            pltpu.VMEM((lt, SSM_D_INNER), BF16),
            pltpu.VMEM((lt, SSM_D_INNER), BF16),
            pltpu.VMEM((1,) + _STATE_SHAPE, F32),
        ],
        compiler_params=_params(1),
        name="ssm_sublayer",
    )(x, x, mod, mod, mod, gpre, gpost, *weights)


def _ssm_weights(in_w, conv_w, conv_b, dt_bias, a_log, d_skip, norm_w, out_w):
    pad = LANES - SSM_HEADS
    heads = np.arange(SSM_D_INNER) // SSM_HEAD_DIM
    e = (np.arange(LANES)[:, None] == heads[None, :]).astype(np.float32)
    return dict(
        wz=in_w[:, :SSM_D_INNER].astype(BF16),
        wxbc=in_w[:, SSM_D_INNER:SSM_D_INNER + SSM_CONV_DIM].astype(BF16),
        wdt=jnp.pad(in_w[:, SSM_D_INNER + SSM_CONV_DIM:], ((0, 0), (0, pad))).astype(BF16),
        conv_w=conv_w,
        conv_b=conv_b.reshape(1, SSM_CONV_DIM),
        dt_bias=jnp.pad(dt_bias, (0, pad)).reshape(1, LANES),
        a_log=jnp.pad(a_log, (0, pad)).reshape(1, LANES),
        d_x=jnp.repeat(d_skip, SSM_HEAD_DIM).reshape(1, SSM_D_INNER),
        norm_w=norm_w.reshape(1, SSM_D_INNER),
        wo=out_w.astype(BF16),
        e2=jnp.asarray(np.concatenate([e, e], axis=0), BF16),
    )


SOFTMAX_ROWS = 64


def _softmax_sink(logits, sink):
    mx = jnp.maximum(jnp.max(logits, axis=-1, keepdims=True), sink)
    e = jnp.exp(logits - mx)
    denom = jnp.sum(e, axis=-1, keepdims=True) + jnp.exp(sink - mx)
    return e * (1.0 / denom)


def _dedup(t):
    lane_lo = lax.broadcasted_iota(jnp.int32, (t.shape[0], LANES), 1) < HALF
    tiles = [jnp.where(lane_lo, t[:, (2 * i) * LANES:(2 * i + 1) * LANES],
                       t[:, (2 * i + 1) * LANES:(2 * i + 2) * LANES]) for i in range(ATTN_KV_HEADS // 2)]
    return jnp.concatenate(tiles, axis=1)


def _swa_prompt_kernel(sinks_ref, xa_ref, xc_ref, sh_ref, sc_ref, gt_ref, gpre_ref, gpost_ref, wq_ref, wk_ref,
                       wv_ref, bq_ref, bk_ref, bv_ref, bias_ref, wo_ref, bo_ref,
                       o_ref, kc_ref, vc_ref, qn_s, kn_s, vn_s, q_s, kbuf, vbuf, att_s, lg_s, p_s,
                       *, steps_per_seq):
    s = pl.program_id(0)
    tq = xa_ref.shape[1]
    n_tiles = 2 * ATTN_KV_HEADS
    pairs = ATTN_REP // 2

    @pl.when(s == 0)
    def _():
        for ref in (qn_s, kn_s, vn_s, kbuf, vbuf):
            ref[...] = jnp.zeros(ref.shape, ref.dtype)

    kbuf[0:WINDOW, :] = kbuf[tq:tq + WINDOW, :]
    vbuf[0:WINDOW, :] = vbuf[tq:tq + WINDOW, :]
    kbuf[WINDOW:WINDOW + tq, :] = kn_s[...]
    vbuf[WINDOW:WINDOW + tq, :] = vn_s[...]
    q_s[...] = qn_s[...]

    hb = _mod_norm(xa_ref[0], gpre_ref[0], sc_ref[0], sh_ref[0]).astype(BF16)
    half_q = ATTN_Q_DIM // 2

    def project_q(lo):
        qn_s[:, lo:lo + half_q] = ((_dot(hb, wq_ref[:, lo:lo + half_q]) + bq_ref[:, lo:lo + half_q])
                                   * (ATTN_HEAD_DIM ** -0.5)).astype(BF16)

    def project_kv(w_ref, b_ref, nxt, cache_ref):
        kv = _dot(hb, w_ref[...]) + b_ref[...]
        nxt[...] = kv.astype(BF16)
        cache_ref[0] = _dedup(kv[tq - WINDOW:, :])

    parts = [functools.partial(project_q, 0), functools.partial(project_q, half_q),
             functools.partial(project_kv, wk_ref, bk_ref, kn_s, kc_ref),
             functools.partial(project_kv, wv_ref, bv_ref, vn_s, vc_ref)]

    lane_lo = lax.broadcasted_iota(jnp.int32, (2 * WINDOW, LANES), 1) < HALF
    zero_b = jnp.zeros((), BF16)
    seq_first = jnp.where(((s + steps_per_seq - 1) % steps_per_seq) == 0, n_tiles, 0)
    n_q = tq // WINDOW
    for bi in range(n_q):
        r0 = bi * WINDOW
        qrows = slice(r0, r0 + WINDOW)
        first = seq_first if bi == 0 else 0
        lg, pb = lg_s.at[bi], p_s.at[bi]
        for g in range(ATTN_KV_HEADS):
            kd = kbuf[r0:r0 + 2 * WINDOW, g * LANES:(g + 1) * LANES]
            q2 = jnp.concatenate([q_s[qrows, (g * pairs + pr) * LANES:(g * pairs + pr + 1) * LANES]
                                  for pr in range(pairs)], axis=0)
            for half in range(2):
                kh = jnp.where(lane_lo, kd, zero_b) if half == 0 else jnp.where(lane_lo, zero_b, kd)
                t = 2 * g + half
                lg[t] = _dot_nt(q2, kh) + bias_ref[first + t]
        for part in parts[bi * len(parts) // n_q:(bi + 1) * len(parts) // n_q]:
            part()
        for t in range(n_tiles):
            for rc in range(2 * WINDOW // SOFTMAX_ROWS):
                rs = slice(rc * SOFTMAX_ROWS, (rc + 1) * SOFTMAX_ROWS)
                h = 4 * (t // 2) + (t % 2) + 2 * ((rc * SOFTMAX_ROWS) // WINDOW)
                pb[t, rs, :] = _softmax_sink(lg[t, rs, :], sinks_ref[h]).astype(BF16)
        for g in range(ATTN_KV_HEADS):
            vd = vbuf[r0:r0 + 2 * WINDOW, g * LANES:(g + 1) * LANES]
            acc = (_dot(pb[2 * g], jnp.where(lane_lo, vd, zero_b))
                   + _dot(pb[2 * g + 1], jnp.where(lane_lo, zero_b, vd)))
            for pr in range(pairs):
                pair = g * pairs + pr
                att_s[qrows, pair * LANES:(pair + 1) * LANES] = acc[pr * WINDOW:(pr + 1) * WINDOW].astype(BF16)
        out = _dot(att_s[qrows, :], wo_ref[...]) + bo_ref[...]
        o_ref[0, qrows, :] = xc_ref[0, qrows, :] + gt_ref[0] * _rms(out, gpost_ref[0])


def _swa_prompt(x, mod, gpre, gpost, w, bias, tq):
    b, l, d = x.shape
    kvw = ATTN_KV_HEADS * LANES
    pairs = ATTN_REP // 2
    n_tiles = 2 * ATTN_KV_HEADS
    spq = l // tq
    n_blocks = b * spq
    bias = bias.reshape(2, ATTN_KV_HEADS, pairs, 2, WINDOW, 2 * WINDOW).transpose(0, 1, 3, 2, 4, 5)
    bias = bias.reshape(2 * n_tiles, pairs * WINDOW, 2 * WINDOW)

    def cur(s):
        blk = jnp.minimum(s, n_blocks - 1)
        return blk // spq, blk % spq

    def prev(s):
        blk = jnp.maximum(s - 1, 0)
        return blk // spq, blk % spq

    xa_spec = pl.BlockSpec((1, tq, d), lambda s: cur(s) + (0,))
    xc_spec = pl.BlockSpec((1, tq, d), lambda s: prev(s) + (0,))
    mod_a = [pl.BlockSpec((1, None, 1, d), functools.partial(lambda k, s: (cur(s)[0], k, 0, 0), 3 + k))
             for k in range(2)]
    mod_c = pl.BlockSpec((1, None, 1, d), lambda s: (prev(s)[0], 5, 0, 0))
    cache_spec = pl.BlockSpec((1, WINDOW, ATTN_KV_DIM), lambda s: (cur(s)[0], 0, 0))
    consts = [w["wq"], w["wk_dup"], w["wv_dup"], w["bq"], w["bk_dup"], w["bv_dup"], bias, w["wo"], w["bo"]]
    n_q = tq // WINDOW
    return pl.pallas_call(
        functools.partial(_swa_prompt_kernel, steps_per_seq=spq),
        out_shape=(jax.ShapeDtypeStruct(x.shape, F32),
                   jax.ShapeDtypeStruct((b, WINDOW, ATTN_KV_DIM), F32),
                   jax.ShapeDtypeStruct((b, WINDOW, ATTN_KV_DIM), F32)),
        grid=(n_blocks + 1,),
        in_specs=[pl.BlockSpec(memory_space=pltpu.SMEM), xa_spec, xc_spec] + mod_a
        + [mod_c, _const_spec((1, 1, d)), _const_spec((1, 1, d))]
        + [_const_spec(a.shape, a.dtype == BF16) for a in consts],
        out_specs=(xc_spec, cache_spec, cache_spec),
        scratch_shapes=[
            pltpu.VMEM((tq, ATTN_Q_DIM), BF16),
            pltpu.VMEM((tq, kvw), BF16),
            pltpu.VMEM((tq, kvw), BF16),
            pltpu.VMEM((tq, ATTN_Q_DIM), BF16),
            pltpu.VMEM((WINDOW + tq, kvw), BF16),
            pltpu.VMEM((WINDOW + tq, kvw), BF16),
            pltpu.VMEM((tq, ATTN_Q_DIM), BF16),
            pltpu.VMEM((n_q, n_tiles, pairs * WINDOW, 2 * WINDOW), F32),
            pltpu.VMEM((n_q, n_tiles, pairs * WINDOW, 2 * WINDOW), BF16),
        ],
        compiler_params=_params(1),
        name="swa_prompt_sublayer",
    )(w["sinks"], x, x, mod, mod, mod, gpre, gpost, *consts)


def _swa_sample_kernel(x_ref, sh_ref, sc_ref, gt_ref, gpre_ref, gpost_ref, wq_ref, wk_ref, wv_ref,
                       bq_ref, bk_ref, bv_ref, ck_ref, cv_ref, bias_ref, sink_ref, wo_ref, bo_ref,
                       o_ref, ko_ref, vo_ref, q_s, kn_s, vn_s, att_s):
    x = x_ref[...]
    bt, lt, d = x.shape
    m = bt * lt
    nbuf = ck_ref.shape[1]
    wide = ATTN_KV_HEADS * ATTN_HEAD_DIM

    hb = _mod_norm(x, gpre_ref[...], sc_ref[...], sh_ref[...]).reshape(m, d).astype(BF16)
    q_s[...] = (_dot(hb, wq_ref[...]) + bq_ref[...]) * (ATTN_HEAD_DIM ** -0.5)
    kn_s[...] = _dot(hb, wk_ref[...]) + bk_ref[...]
    vn_s[...] = _dot(hb, wv_ref[...]) + bv_ref[...]
    zpad = jnp.zeros((bias_ref.shape[1] - nbuf - lt, ATTN_KV_DIM), F32)

    def seq(bi, carry):
        rows = pl.ds(pl.multiple_of(bi * lt, lt), lt)
        kn = kn_s[rows, :]
        vn = vn_s[rows, :]
        ck = ck_ref[bi]
        cv = cv_ref[bi]
        kfull = jnp.concatenate([ck, kn, zpad], axis=0).astype(BF16)
        vfull = jnp.concatenate([cv, vn, zpad], axis=0).astype(BF16)
        ko_ref[bi, 0:nbuf - lt, :] = ck[lt:, :]
        vo_ref[bi, 0:nbuf - lt, :] = cv[lt:, :]
        ko_ref[bi, nbuf - lt:nbuf, :] = kn
        vo_ref[bi, nbuf - lt:nbuf, :] = vn
        qbig = jnp.concatenate([q_s[rows, h * wide:(h + 1) * wide] for h in range(ATTN_HEADS)], axis=0)
        logits = _dot_nt(qbig.astype(BF16), kfull) + bias_ref[...]
        p = _softmax_sink(logits, sink_ref[:, 0:1]).astype(BF16)
        res = _dot(p, vfull)
        att_s[rows, :] = jnp.concatenate([res[h * lt:(h + 1) * lt, :] for h in range(ATTN_HEADS)], axis=1)
        return carry

    lax.fori_loop(0, bt, seq, 0, unroll=8)

    out = (_dot(att_s[...].astype(BF16), wo_ref[...]) + bo_ref[...]).reshape(bt, lt, d)
    o_ref[...] = x + gt_ref[...] * _rms(out, gpost_ref[...])


def _swa_sample(x, mod, gpre, gpost, w, bias, cache_k, cache_v, bt):
    b, lt, d = x.shape
    nbuf = cache_k.shape[1]
    m = bt * lt
    wide = ATTN_KV_HEADS * ATTN_HEAD_DIM
    big = ATTN_HEADS * wide
    keys = 2 * WINDOW
    xspec = pl.BlockSpec((bt, lt, d), lambda i: (i, 0, 0))
    cache_spec = pl.BlockSpec((bt, nbuf, ATTN_KV_DIM), lambda i: (i, 0, 0))
    bias_s = bias[0, :, :lt, :].reshape(ATTN_HEADS * lt, keys)
    sink_col = jnp.broadcast_to(jnp.repeat(w["sinks"], lt)[:, None], (ATTN_HEADS * lt, LANES))
    mods = [pl.BlockSpec((bt, None, 1, D_MODEL), functools.partial(lambda k, i: (i, k, 0, 0), 3 + k))
            for k in range(3)]
    pre = [w["wq_big"], w["wk"], w["wv"], w["bq_big"], w["bk"], w["bv"]]
    post = [bias_s, sink_col, w["wo_big"], w["bo"]]
    return pl.pallas_call(
        _swa_sample_kernel,
        out_shape=(jax.ShapeDtypeStruct(x.shape, F32),
                   jax.ShapeDtypeStruct(cache_k.shape, F32),
                   jax.ShapeDtypeStruct(cache_v.shape, F32)),
        grid=(b // bt,),
        in_specs=[xspec] + mods + [_const_spec((1, 1, d)), _const_spec((1, 1, d))]
        + [_const_spec(a.shape, a.dtype == BF16) for a in pre] + [cache_spec, cache_spec]
        + [_const_spec(a.shape, a.dtype == BF16) for a in post],
        out_specs=(xspec, cache_spec, cache_spec),
        scratch_shapes=[
            pltpu.VMEM((m, big), F32),
            pltpu.VMEM((m, ATTN_KV_DIM), F32),
            pltpu.VMEM((m, ATTN_KV_DIM), F32),
            pltpu.VMEM((m, big), F32),
        ],
        compiler_params=_params(1),
        name="swa_sample_sublayer",
    )(x, mod, mod, mod, gpre, gpost, *pre, cache_k, cache_v, *post)


def _swa_weights(qkv_w, qkv_b, sinks, o_w, o_b):
    d = qkv_w.shape[0]
    wq = qkv_w[:, :ATTN_Q_DIM]
    wk = qkv_w[:, ATTN_Q_DIM:ATTN_Q_DIM + ATTN_KV_DIM]
    wv = qkv_w[:, ATTN_Q_DIM + ATTN_KV_DIM:]
    bq = qkv_b[:ATTN_Q_DIM]
    bk = qkv_b[ATTN_Q_DIM:ATTN_Q_DIM + ATTN_KV_DIM]
    bv = qkv_b[ATTN_Q_DIM + ATTN_KV_DIM:]

    def dup(a):
        a4 = a.reshape(a.shape[:-1] + (ATTN_KV_HEADS, 1, ATTN_HEAD_DIM))
        a4 = jnp.broadcast_to(a4, a.shape[:-1] + (ATTN_KV_HEADS, 2, ATTN_HEAD_DIM))
        return a4.reshape(a.shape[:-1] + (ATTN_KV_HEADS * LANES,))

    own = jnp.asarray((np.arange(ATTN_HEADS)[:, None] // ATTN_REP == np.arange(ATTN_KV_HEADS)[None, :])
                      .astype(np.float32))
    wq_big = (wq.reshape(d, ATTN_HEADS, 1, ATTN_HEAD_DIM) * own[None, :, :, None]).reshape(d, -1)
    bq_big = (bq.reshape(ATTN_HEADS, 1, ATTN_HEAD_DIM) * own[:, :, None]).reshape(1, -1)
    wo_big = (o_w.reshape(ATTN_HEADS, 1, ATTN_HEAD_DIM, d) * own[:, :, None, None]).reshape(-1, d)
    return dict(
        sinks=sinks,
        wq=wq.astype(BF16), wk_dup=dup(wk).astype(BF16), wv_dup=dup(wv).astype(BF16),
        bq=bq.reshape(1, -1), bk_dup=dup(bk).reshape(1, -1), bv_dup=dup(bv).reshape(1, -1),
        wo=o_w.astype(BF16), bo=o_b.reshape(1, d),
        wq_big=wq_big.astype(BF16), bq_big=bq_big, wk=wk.astype(BF16), wv=wv.astype(BF16),
        bk=bk.reshape(1, -1), bv=bv.reshape(1, -1), wo_big=wo_big.astype(BF16),
    )


def _trunk(x, mod_all, state_ssm, state_conv, cache_k, cache_v, norm_pre, norm_post, ffn_in, ffn_out,
           ssm_w, swa_w, bias, tiles):
    bt, lt, n_sub = tiles["ffn"]
    b, l, d = x.shape
    sample = state_ssm is not None
    outs = {}
    for i in range(2):
        mod = mod_all[i].reshape(b, N_SUB * 3, 1, d)
        gpre = norm_pre[i].reshape(N_SUB, 1, 1, d)
        gpost = norm_post[i].reshape(N_SUB, 1, 1, d)
        x = _ffn(x, mod, 0, gpre[0], gpost[0], ffn_in, ffn_out, i, 0, bt, lt, n_sub)
        if i == 0:
            if sample:
                x, conv_new, ssm_new = _ssm_sample(x, mod, gpre[1], gpost[1], ssm_w, state_conv[0], state_ssm[0],
                                                   tiles["ssm"])
            else:
                x, conv_new, ssm_new = _ssm_prompt(x, mod, gpre[1], gpost[1], ssm_w)
            outs["conv"] = conv_new[None]
            outs["ssm"] = ssm_new[None]
        else:
            if sample:
                x, k_new, v_new = _swa_sample(x, mod, gpre[1], gpost[1], swa_w, bias,
                                              cache_k[0].reshape(b, -1, ATTN_KV_DIM),
                                              cache_v[0].reshape(b, -1, ATTN_KV_DIM), tiles["swa"])
            else:
                x, k_new, v_new = _swa_prompt(x, mod, gpre[1], gpost[1], swa_w, bias, tiles["swa"])
            shape = (1, b, -1, ATTN_KV_HEADS, ATTN_HEAD_DIM)
            outs["k"] = k_new.reshape(shape)
            outs["v"] = v_new.reshape(shape)
        x = _ffn(x, mod, 2, gpre[2], gpost[2], ffn_in, ffn_out, i, 1, bt, lt, n_sub)
    return x, outs["ssm"], outs["conv"], outs["k"], outs["v"]


def kernel(x_prompt, x_sample, state_ssm, state_conv, cache_k, cache_v, c_prompt, c_sample, ada_w, ada_b, norm_pre, norm_post, ffn_w_in, ffn_w_out, ssm_in_w, ssm_conv_w, ssm_conv_b, ssm_dt_bias, ssm_a_log, ssm_d, ssm_norm_w, ssm_out_w, attn_qkv_w, attn_qkv_b, attn_sinks, attn_o_w, attn_o_b, rel_bias):
    nb = x_prompt.shape[0]
    mod_all = _ada(jnp.concatenate([c_prompt, c_sample], axis=0), ada_w, ada_b)
    bias = _bias_table(rel_bias)
    ffn_in = ffn_w_in.astype(BF16)
    ffn_out = ffn_w_out.astype(BF16)
    ssm_w = _ssm_weights(ssm_in_w[0], ssm_conv_w[0], ssm_conv_b[0], ssm_dt_bias[0], ssm_a_log[0], ssm_d[0],
                         ssm_norm_w[0], ssm_out_w[0])
    swa_w = _swa_weights(attn_qkv_w[0], attn_qkv_b[0], attn_sinks[0], attn_o_w[0], attn_o_b[0])
    common = (norm_pre, norm_post, ffn_in, ffn_out, ssm_w, swa_w, bias)

    p_tiles = dict(ffn=(1, 1024, 2), swa=512)
    y_p, ssm_p, conv_p, k_p, v_p = _trunk(x_prompt, mod_all[:, :nb], None, None, None, None, *common, p_tiles)
    ns, ls = x_sample.shape[:2]
    s_tiles = dict(ffn=(min(64, ns), ls, 1), ssm=min(4, ns), swa=min(16, ns))
    y_s, ssm_s, conv_s, k_s, v_s = _trunk(x_sample, mod_all[:, nb:], state_ssm, state_conv, cache_k, cache_v,
                                          *common, s_tiles)
    return (y_p, y_s, ssm_p, conv_p, k_p, v_p, ssm_s, conv_s, k_s, v_s)
```

```python
import functools
import math

import numpy as np
import jax
import jax.numpy as jnp
from jax import lax
from jax.experimental import pallas as pl
from jax.experimental.pallas import tpu as pltpu

F32 = jnp.float32
BF16 = jnp.bfloat16

D_MODEL = 1024
N_SUB = 3
RMS_EPS = 1e-6
FFN_RES = 0.5
D_FF = 2816

SSM_D_INNER = 2048
SSM_HEAD_DIM = 64
SSM_HEADS = 32
SSM_GROUPS = 4
SSM_HPG = 8
SSM_STATE = 128
SSM_CONV = 4
SSM_CHUNK = 128
SSM_GN = SSM_GROUPS * SSM_STATE
SSM_CONV_DIM = SSM_D_INNER + 2 * SSM_GN
SSM_GROUP_WIDTH = SSM_HPG * SSM_HEAD_DIM

ATTN_HEAD_DIM = 64
ATTN_HEADS = 16
ATTN_KV_HEADS = 4
ATTN_REP = 4
WINDOW = 128
REL_BUCKETS = 32
ATTN_Q_DIM = ATTN_HEADS * ATTN_HEAD_DIM
ATTN_KV_DIM = ATTN_KV_HEADS * ATTN_HEAD_DIM

LANES = 128
SUBLANES = 8
HALF = LANES // 2
VMEM_LIMIT_BYTES = 56 * 1024 * 1024
CONV_PAD = SUBLANES


def _dot(a, b):
    return jnp.dot(a, b, preferred_element_type=F32)


def _dot_nt(a, b):
    return lax.dot_general(a, b, (((1,), (1,)), ((), ())), preferred_element_type=F32)


def _dot_tn(a, b):
    return lax.dot_general(a, b, (((0,), (0,)), ((), ())), preferred_element_type=F32)


def _silu(x):
    h = 0.5 * x
    return h * jnp.tanh(h) + h


def _rms(x, g):
    return x * lax.rsqrt(jnp.mean(x * x, axis=-1, keepdims=True) + RMS_EPS) * g


def _mod_norm(x, g, scale, shift):
    return _rms(x, g) * (1.0 + scale) + shift


def _split3(x):
    hi = x.astype(BF16)
    r1 = x - hi.astype(F32)
    mid = r1.astype(BF16)
    lo = (r1 - mid.astype(F32)).astype(BF16)
    return hi, mid, lo


def _const_spec(shape, single_buffer=False):
    nd = len(shape)
    kw = {"pipeline_mode": pl.Buffered(1)} if single_buffer else {}
    return pl.BlockSpec(shape, lambda *_: (0,) * nd, **kw)


def _params(n_grid, flags=None):
    return pltpu.CompilerParams(
        dimension_semantics=("arbitrary",) * n_grid,
        vmem_limit_bytes=VMEM_LIMIT_BYTES,
        flags=flags,
    )


def _ada_kernel(c_ref, w_ref, b_ref, o_ref):
    cs = _silu(c_ref[...]).astype(BF16)
    o_ref[0] = _dot(cs, w_ref[0].astype(BF16)) + b_ref[0]


def _ada(c_all, ada_w, ada_b, tn=1152):
    depth, d, n = ada_w.shape
    bc = c_all.shape[0]
    return pl.pallas_call(
        _ada_kernel,
        out_shape=jax.ShapeDtypeStruct((depth, bc, n), F32),
        grid=(depth, n // tn),
        in_specs=[
            pl.BlockSpec((bc, d), lambda l, j: (0, 0)),
            pl.BlockSpec((1, d, tn), lambda l, j: (l, 0, j)),
            pl.BlockSpec((1, 1, tn), lambda l, j: (l, 0, j)),
        ],
        out_specs=pl.BlockSpec((1, bc, tn), lambda l, j: (l, 0, j)),
        compiler_params=_params(2),
        name="ada_mod",
    )(c_all, ada_w, ada_b.reshape(depth, 1, n))


def _t5_bucket_table():
    i = np.arange(WINDOW)[:, None]
    j = np.arange(2 * WINDOW)[None, :]
    dist = i + WINDOW - j
    exact = REL_BUCKETS // 2
    df = np.maximum(dist, 1).astype(np.float32)
    large = exact + (np.log(df / np.float32(exact)) / np.float32(math.log(WINDOW / exact))
                     * np.float32(REL_BUCKETS - exact)).astype(np.int32)
    large = np.minimum(large, REL_BUCKETS - 1)
    bucket = np.where(dist < exact, dist, large)
    valid = (dist >= 0) & (dist <= WINDOW)
    return np.where(valid, bucket, -1).astype(np.int32)


def _bias_kernel(rb_ref, idx_ref, o_ref):
    h = pl.program_id(0)
    idx = idx_ref[...]
    acc = jnp.full(idx.shape, -jnp.inf, F32)
    for b in range(REL_BUCKETS):
        acc = jnp.where(idx == b, rb_ref[b, h], acc)
    o_ref[0, 0] = acc
    col = lax.broadcasted_iota(jnp.int32, idx.shape, 1)
    o_ref[1, 0] = jnp.where(col < WINDOW, -jnp.inf, acc)


def _bias_table(rel_bias):
    idx = jnp.asarray(_t5_bucket_table())
    return pl.pallas_call(
        _bias_kernel,
        out_shape=jax.ShapeDtypeStruct((2, ATTN_HEADS, WINDOW, 2 * WINDOW), F32),
        grid=(ATTN_HEADS,),
        in_specs=[
            pl.BlockSpec(memory_space=pltpu.SMEM),
            pl.BlockSpec((WINDOW, 2 * WINDOW), lambda h: (0, 0)),
        ],
        out_specs=pl.BlockSpec((2, 1, WINDOW, 2 * WINDOW), lambda h: (0, h, 0, 0)),
        compiler_params=_params(1),
        name="rel_bias_table",
    )(rel_bias, idx)


MXU_TILE = 256
FF_CHUNKS = ((0, D_FF),)


def _ffn_kernel(x_ref, sh_ref, sc_ref, gt_ref, gpre_ref, gpost_ref, win_ref, wout_ref, o_ref, *, ff_chunks, n_sub):
    bt, lt, d = x_ref.shape

    def sub_slices(s):
        if bt == 1:
            return slice(None), slice(s * (lt // n_sub), (s + 1) * (lt // n_sub))
        return slice(s * (bt // n_sub), (s + 1) * (bt // n_sub)), slice(None)

    def pre(s):
        bs, ls = sub_slices(s)
        x = x_ref[bs, ls, :]
        h = _mod_norm(x, gpre_ref[...], sc_ref[bs], sh_ref[bs])
        return h.reshape(x.shape[0] * x.shape[1], d).astype(BF16)

    def post(s, acc):
        bs, ls = sub_slices(s)
        x = x_ref[bs, ls, :]
        o_ref[bs, ls, :] = x + FFN_RES * gt_ref[bs] * _rms(acc.reshape(x.shape), gpost_ref[...])

    hb = pre(0)
    prev = None
    for s in range(n_sub):
        acc = None
        hb_next = None
        for c, (lo, hi) in enumerate(ff_chunks):
            g = _dot(hb, win_ref[:, lo:hi])
            u = _dot(hb, win_ref[:, D_FF + lo:D_FF + hi])
            a = (_silu(g) * u).astype(BF16)
            part = _dot(a, wout_ref[lo:hi, :])
            acc = part if acc is None else acc + part
            if c == 0:
                if prev is not None:
                    post(s - 1, prev)
                if s + 1 < n_sub:
                    hb_next = pre(s + 1)
        prev = acc
        hb = hb_next
    post(n_sub - 1, prev)


def _mod_specs(bt, sub):
    return [pl.BlockSpec((bt, None, 1, D_MODEL), functools.partial(lambda k, i, j: (i, k, 0, 0), sub * 3 + k))
            for k in range(3)]


def _ffn(x, mod, sub, gpre, gpost, w_in, w_out, layer, which, bt, lt, n_sub):
    b, l, d = x.shape
    xspec = pl.BlockSpec((bt, lt, d), lambda i, j: (i, j, 0))
    wspec = [pl.BlockSpec((None, None) + w.shape[2:], lambda i, j: (layer, which, 0, 0),
                          pipeline_mode=pl.Buffered(1)) for w in (w_in, w_out)]
    return pl.pallas_call(
        functools.partial(_ffn_kernel, ff_chunks=FF_CHUNKS, n_sub=n_sub),
        out_shape=jax.ShapeDtypeStruct(x.shape, F32),
        grid=(b // bt, l // lt),
        in_specs=[xspec] + _mod_specs(bt, sub) + [
            _const_spec((1, 1, d)), _const_spec((1, 1, d))] + wspec,
        out_specs=xspec,
        compiler_params=_params(2),
        name="ffn_sublayer",
    )(x, mod, mod, mod, gpre, gpost, w_in, w_out)


def _chunk_consts(qc, seg):
    r = lax.broadcasted_iota(jnp.int32, (qc, qc), 0)
    c = lax.broadcasted_iota(jnp.int32, (qc, qc), 1)
    seg_shift = seg.bit_length() - 1
    same = jnp.right_shift(r, seg_shift) == jnp.right_shift(c, seg_shift)
    causal = same & (r >= c)
    tri = jnp.where(causal, 1.0, 0.0)
    upper = jnp.where(same & (c > r), 1.0, 0.0)
    tu = jnp.concatenate([tri, upper], axis=0).astype(BF16)
    er = lax.broadcasted_iota(jnp.int32, (LANES, LANES), 0)
    ec = lax.broadcasted_iota(jnp.int32, (LANES, LANES), 1)
    eye = jnp.where(er == ec, 1.0, 0.0).astype(BF16)
    lane_lo = lax.broadcasted_iota(jnp.int32, (qc, LANES), 1) < HALF
    return tu, eye, causal, lane_lo


def _ssm_project(hb, wz_ref, wxbc_ref, wdt_ref, dtb_ref, alog_ref):
    z = _dot(hb, wz_ref[...])
    xbc_raw = _dot(hb, wxbc_ref[...])
    dt_raw = _dot(hb, wdt_ref[...]) + dtb_ref[...]
    dt = jnp.maximum(dt_raw, 0.0) + jnp.log1p(jnp.exp(-jnp.abs(dt_raw)))
    return z, xbc_raw, dt, dt * (-jnp.exp(alog_ref[...]))


def _gate_norm(y, xs, z, dx, nw):
    yg = (y + xs * dx) * _silu(z)
    parts = []
    for g in range(SSM_GROUPS):
        v = yg[:, g * SSM_GROUP_WIDTH:(g + 1) * SSM_GROUP_WIDTH]
        parts.append(v * lax.rsqrt(jnp.mean(v * v, axis=-1, keepdims=True) + RMS_EPS))
    return (jnp.concatenate(parts, axis=1) * nw).astype(BF16)


def _cols(ref, rows, c0, c1):
    if len(ref.shape) == 2:
        return ref[rows, c0:c1]
    return jnp.concatenate([ref[c, rows, :] for c in range(c0 // LANES, c1 // LANES)], axis=1)


def _ssd_chunk(r0, qc, seg, s_first, *, xbc_s, dt_s, da_s, y_s, e2_ref, h_in, h_out, consts, reset=None):
    tu, eye, causal, lane_lo = consts
    rows = pl.ds(r0, qc)
    hi, mid, lo = _split3(da_s[rows, :])
    cs2 = _dot(tu, hi) + _dot(tu, mid) + _dot(tu, lo)
    a_cs = cs2[:qc]
    ea = jnp.exp(a_cs)
    dte = jnp.exp(cs2[qc:])

    stack = jnp.concatenate([dt_s[rows, :], ea, dte], axis=0)
    s_hi = stack.astype(BF16)
    s_lo = (stack - s_hi.astype(F32)).astype(BF16)
    sx = _dot(jnp.concatenate([s_hi, s_lo], axis=1), e2_ref[...])
    dt_x, ea_x, dte_x = sx[:qc], sx[qc:2 * qc], sx[2 * qc:]

    xdt = _cols(xbc_s, rows, 0, SSM_D_INNER) * dt_x
    xdt_b = xdt.astype(BF16)
    xd_b = (xdt * dte_x).astype(BF16)
    bm = _cols(xbc_s, rows, SSM_D_INNER, SSM_D_INNER + SSM_GN).astype(BF16)
    cm = _cols(xbc_s, rows, SSM_D_INNER + SSM_GN, SSM_CONV_DIM).astype(BF16)

    a_hi, a_mid, a_lo = _split3(a_cs)
    a_cs_t = _dot_nt(eye, a_hi) + _dot_nt(eye, a_mid) + _dot_nt(eye, a_lo)

    zero_b = jnp.zeros((), BF16)
    for g in range(SSM_GROUPS):
        gsl = slice(g * SSM_STATE, (g + 1) * SSM_STATE)
        cb = _dot_nt(cm[:, gsl], bm[:, gsl])
        for pr in range(SSM_HPG // 2):
            h0 = g * SSM_HPG + 2 * pr
            psl = slice((h0 // 2) * LANES, (h0 // 2 + 1) * LANES)
            xp = xdt_b[:, psl]
            acc = None
            for half in range(2):
                h = h0 + half
                seg_sum = a_cs[:, h:h + 1] - a_cs_t[h:h + 1, :]
                decay = jnp.exp(jnp.where(causal, seg_sum, -jnp.inf))
                w = (decay * cb).astype(BF16)
                xh = jnp.where(lane_lo if half == 0 else jnp.logical_not(lane_lo), xp, zero_b)
                o = _dot(w, xh)
                acc = o if acc is None else acc + o
            y_s[rows, psl] = acc

    for t in range(qc // seg):
        tr = slice(t * seg, (t + 1) * seg)
        trows = pl.ds(r0 + t * seg, seg)
        last = t * seg + seg - 1
        for g in range(SSM_GROUPS):
            gsl = slice(g * SSM_STATE, (g + 1) * SSM_STATE)
            csl = slice(g * SSM_GROUP_WIDTH, (g + 1) * SSM_GROUP_WIDTH)
            hsl = slice(g * SSM_HPG, (g + 1) * SSM_HPG)
            hg = h_in[s_first + t, hsl].reshape(SSM_GROUP_WIDTH, SSM_STATE)
            if reset is not None:
                hg = jnp.where(reset, 0.0, hg)
            y_off = _dot_nt(cm[tr, gsl], hg.astype(BF16)) * ea_x[tr, csl]
            y_s[trows, csl] = y_s[trows, csl] + y_off
            upd = _dot_tn(xd_b[tr, csl], bm[tr, gsl])
            cdec = jnp.concatenate(
                [jnp.broadcast_to(ea[last:last + 1, g * SSM_HPG + r:g * SSM_HPG + r + 1],
                                  (SSM_HEAD_DIM, SSM_STATE)) for r in range(SSM_HPG)], axis=0)
            h_out[s_first + t, hsl] = (hg * cdec + upd).reshape(SSM_HPG, SSM_HEAD_DIM, SSM_STATE)


HIST = SSM_CONV - 1
_SSM_WEIGHT_NAMES = ("wz", "wxbc", "wdt", "conv_w", "conv_b", "dt_bias", "a_log", "d_x", "norm_w", "wo", "e2")
_STATE_SHAPE = (SSM_HEADS, SSM_HEAD_DIM, SSM_STATE)


def _conv_silu(xpad, cw_ref, cb_ref, lt):
    conv = cb_ref[...]
    for k in range(SSM_CONV):
        conv = conv + xpad[:, CONV_PAD - HIST + k:CONV_PAD - HIST + k + lt, :] * cw_ref[k:k + 1, :]
    return _silu(conv)


def _ssm_sample_kernel(x_ref, sh_ref, sc_ref, gt_ref, gpre_ref, gpost_ref, wz_ref, wxbc_ref, wdt_ref, cw_ref,
                       cb_ref, dtb_ref, alog_ref, dx_ref, nw_ref, wo_ref, e2_ref, conv_in_ref, h0_ref,
                       o_ref, conv_out_ref, h_out_ref, xpad, xbc_s, dt_s, da_s, y_s):
    x = x_ref[...]
    bt, lt, d = x.shape
    m = bt * lt
    hb = _mod_norm(x, gpre_ref[...], sc_ref[...], sh_ref[...]).reshape(m, d).astype(BF16)
    z, xbc_raw, dt, da = _ssm_project(hb, wz_ref, wxbc_ref, wdt_ref, dtb_ref, alog_ref)
    dt_s[...] = dt
    da_s[...] = da
    xpad[:, CONV_PAD:CONV_PAD + lt, :] = xbc_raw.reshape(bt, lt, SSM_CONV_DIM)
    xpad[:, CONV_PAD - HIST:CONV_PAD, :] = conv_in_ref[...]
    xbc_s[...] = _conv_silu(xpad, cw_ref, cb_ref, lt).reshape(m, SSM_CONV_DIM)
    conv_out_ref[...] = xpad[:, CONV_PAD + lt - HIST:CONV_PAD + lt, :]
    _ssd_chunk(0, m, lt, 0, xbc_s=xbc_s, dt_s=dt_s, da_s=da_s, y_s=y_s, e2_ref=e2_ref,
               h_in=h0_ref, h_out=h_out_ref, consts=_chunk_consts(m, lt))
    yn = _gate_norm(y_s[...], xbc_s[:, 0:SSM_D_INNER], z, dx_ref[...], nw_ref[...])
    out = _dot(yn, wo_ref[...]).reshape(bt, lt, d)
    o_ref[...] = x + gt_ref[...] * _rms(out, gpost_ref[...])


def _ssm_sample(x, mod, gpre, gpost, w, conv_in, h0, bt):
    b, lt, d = x.shape
    m = bt * lt
    xspec = pl.BlockSpec((bt, lt, d), lambda i: (i, 0, 0))
    conv_spec = pl.BlockSpec((bt, HIST, SSM_CONV_DIM), lambda i: (i, 0, 0))
    state_spec = pl.BlockSpec((bt,) + _STATE_SHAPE, lambda i: (i, 0, 0, 0))
    mods = [pl.BlockSpec((bt, None, 1, d), functools.partial(lambda k, i: (i, k, 0, 0), 3 + k)) for k in range(3)]
    weights = [w[n] for n in _SSM_WEIGHT_NAMES]
    return pl.pallas_call(
        _ssm_sample_kernel,
        out_shape=(jax.ShapeDtypeStruct(x.shape, F32),
                   jax.ShapeDtypeStruct((b, HIST, SSM_CONV_DIM), F32),
                   jax.ShapeDtypeStruct((b,) + _STATE_SHAPE, F32)),
        grid=(b // bt,),
        in_specs=[xspec] + mods + [_const_spec((1, 1, d)), _const_spec((1, 1, d))]
        + [_const_spec(a.shape, a.dtype == BF16) for a in weights]
        + [conv_spec, pl.BlockSpec(state_spec.block_shape, state_spec.index_map, pipeline_mode=pl.Buffered(1))],
        out_specs=(xspec, conv_spec, state_spec),
        scratch_shapes=[
            pltpu.VMEM((bt, CONV_PAD + lt, SSM_CONV_DIM), F32),
            pltpu.VMEM((m, SSM_CONV_DIM), F32),
            pltpu.VMEM((m, LANES), F32),
            pltpu.VMEM((m, LANES), F32),
            pltpu.VMEM((m, SSM_D_INNER), F32),
        ],
        compiler_params=_params(1),
        name="ssm_sublayer_state",
    )(x, mod, mod, mod, gpre, gpost, *weights, conv_in, h0)


def _ssm_prompt_kernel(xa_ref, xc_ref, sh_ref, sc_ref, gt_ref, gpre_ref, gpost_ref, wz_ref, wxbc_ref, wdt_ref,
                       cw_ref, cb_ref, dtb_ref, alog_ref, dx_ref, nw_ref, wo_ref, e2_ref,
                       o_ref, conv_out_ref, h_out_ref,
                       xpad, z_s, xbc_s, dt_s, da_s, zc_s, xbcc_s, dtc_s, dac_s, y_s, yn_s, ynp_s, h_s,
                       *, steps_per_seq, n_blocks):
    s = pl.program_id(0)
    q = SSM_CHUNK
    lt = 2 * q
    seq_start = (s % steps_per_seq) == 0
    scan_start = ((s + steps_per_seq - 1) % steps_per_seq) == 0
    n_slabs = SSM_CONV_DIM // LANES

    @pl.when(s == 0)
    def _():
        for ref in (z_s, xbc_s, dt_s, da_s, yn_s, h_s):
            ref[...] = jnp.zeros(ref.shape, ref.dtype)

    @pl.when(seq_start)
    def _():
        xpad[:, CONV_PAD - HIST:CONV_PAD, :] = jnp.zeros((n_slabs, HIST, LANES), F32)

    ynp_s[...] = yn_s[...]
    zc_s[...] = z_s[...]
    xbcc_s[...] = xbc_s[...]
    dtc_s[...] = dt_s[...]
    dac_s[...] = da_s[...]
    consts = _chunk_consts(q, q)

    def scan(t, reset):
        rows = slice(t * q, (t + 1) * q)
        xbc_v = xbcc_s.at[:, rows, :]
        _ssd_chunk(0, q, q, 0, xbc_s=xbc_v, dt_s=dtc_s.at[rows], da_s=dac_s.at[rows], y_s=y_s, e2_ref=e2_ref,
                   h_in=h_s, h_out=h_s, consts=consts, reset=reset)
        xs = _cols(xbc_v, slice(None), 0, SSM_D_INNER)
        yn_s[rows, :] = _gate_norm(y_s[...], xs, zc_s[rows, :], dx_ref[...], nw_ref[...])

    out = _dot(ynp_s[...], wo_ref[...])
    o_ref[0] = xc_ref[0] + gt_ref[0] * _rms(out, gpost_ref[0])

    hb = _mod_norm(xa_ref[0], gpre_ref[0], sc_ref[0], sh_ref[0]).astype(BF16)
    z, xbc_raw, dt, da = _ssm_project(hb, wz_ref, wxbc_ref, wdt_ref, dtb_ref, alog_ref)
    z_s[...] = z
    dt_s[...] = dt
    da_s[...] = da
    for c in range(n_slabs):
        xpad[c, CONV_PAD:CONV_PAD + lt, :] = xbc_raw[:, c * LANES:(c + 1) * LANES]

    scan(0, scan_start)

    half = lt // 2
    for c in range(n_slabs):
        csl = slice(c * LANES, (c + 1) * LANES)
        for par in range(2):
            acc = cb_ref[:, csl]
            for k in range(SSM_CONV):
                rows = pl.ds(CONV_PAD - HIST + par + k, half, stride=2)
                acc = acc + xpad[c, rows, :] * cw_ref[k:k + 1, csl]
            xbc_s[c, pl.ds(par, half, stride=2), :] = _silu(acc)
        new_hist = xpad[c, CONV_PAD + lt - HIST:CONV_PAD + lt, :]
        conv_out_ref[0, :, csl] = new_hist
        xpad[c, CONV_PAD - HIST:CONV_PAD, :] = new_hist

    scan(1, None)

    @pl.when(s <= n_blocks)
    def _():
        h_out_ref[...] = h_s[...]


def _ssm_prompt(x, mod, gpre, gpost, w):
    b, l, d = x.shape
    q = SSM_CHUNK
    lt = 2 * q
    spq = l // lt
    n_blocks = b * spq

    def lagged(lag):
        def index(s):
            blk = jnp.clip(s - lag, 0, n_blocks - 1)
            return blk // spq, blk % spq
        return index

    cur, prev, prev2 = lagged(0), lagged(1), lagged(2)
    xa_spec = pl.BlockSpec((1, lt, d), lambda s: cur(s) + (0,))
    xc_spec = pl.BlockSpec((1, lt, d), lambda s: prev2(s) + (0,))
    mod_a = [pl.BlockSpec((1, None, 1, d), functools.partial(lambda k, s: (cur(s)[0], k, 0, 0), 3 + k))
             for k in range(2)]
    mod_c = pl.BlockSpec((1, None, 1, d), lambda s: (prev2(s)[0], 5, 0, 0))
    conv_spec = pl.BlockSpec((1, HIST, SSM_CONV_DIM), lambda s: (cur(s)[0], 0, 0))
    state_spec = pl.BlockSpec((1,) + _STATE_SHAPE, lambda s: (prev(s)[0], 0, 0, 0))
    weights = [w[n] for n in _SSM_WEIGHT_NAMES]
    return pl.pallas_call(
        functools.partial(_ssm_prompt_kernel, steps_per_seq=spq, n_blocks=n_blocks),
        out_shape=(jax.ShapeDtypeStruct(x.shape, F32),
                   jax.ShapeDtypeStruct((b, HIST, SSM_CONV_DIM), F32),
                   jax.ShapeDtypeStruct((b,) + _STATE_SHAPE, F32)),
        grid=(n_blocks + 2,),
        in_specs=[xa_spec, xc_spec] + mod_a + [mod_c, _const_spec((1, 1, d)), _const_spec((1, 1, d))]
        + [_const_spec(a.shape, a.dtype == BF16) for a in weights],
        out_specs=(xc_spec, conv_spec, state_spec),
        scratch_shapes=[
            pltpu.VMEM((SSM_CONV_DIM // LANES, CONV_PAD + lt, LANES), F32),
            pltpu.VMEM((lt, SSM_D_INNER), F32),
            pltpu.VMEM((SSM_CONV_DIM // LANES, lt, LANES), F32),
            pltpu.VMEM((lt, LANES), F32),
            pltpu.VMEM((lt, LANES), F32),
            pltpu.VMEM((lt, SSM_D_INNER), F32),
            pltpu.VMEM((SSM_CONV_DIM // LANES, lt, LANES), F32),
            pltpu.VMEM((lt, LANES), F32),
            pltpu.VMEM((lt, LANES), F32),
            pltpu.VMEM((q, SSM_D_INNER), F32),
            pltpu.VMEM((lt, SSM_D_INNER), BF16),
            pltpu.VMEM((lt, SSM_D_INNER), BF16),
            pltpu.VMEM((1,) + _STATE_SHAPE, F32),
        ],
        compiler_params=_params(1),
        name="ssm_sublayer",
    )(x, x, mod, mod, mod, gpre, gpost, *weights)


def _ssm_weights(in_w, conv_w, conv_b, dt_bias, a_log, d_skip, norm_w, out_w):
    pad = LANES - SSM_HEADS
    heads = np.arange(SSM_D_INNER) // SSM_HEAD_DIM
    e = (np.arange(LANES)[:, None] == heads[None, :]).astype(np.float32)
    return dict(
        wz=in_w[:, :SSM_D_INNER].astype(BF16),
        wxbc=in_w[:, SSM_D_INNER:SSM_D_INNER + SSM_CONV_DIM].astype(BF16),
        wdt=jnp.pad(in_w[:, SSM_D_INNER + SSM_CONV_DIM:], ((0, 0), (0, pad))).astype(BF16),
        conv_w=conv_w,
        conv_b=conv_b.reshape(1, SSM_CONV_DIM),
        dt_bias=jnp.pad(dt_bias, (0, pad)).reshape(1, LANES),
        a_log=jnp.pad(a_log, (0, pad)).reshape(1, LANES),
        d_x=jnp.repeat(d_skip, SSM_HEAD_DIM).reshape(1, SSM_D_INNER),
        norm_w=norm_w.reshape(1, SSM_D_INNER),
        wo=out_w.astype(BF16),
        e2=jnp.asarray(np.concatenate([e, e], axis=0), BF16),
    )


SOFTMAX_ROWS = 64


def _softmax_sink(logits, sink):
    mx = jnp.maximum(jnp.max(logits, axis=-1, keepdims=True), sink)
    e = jnp.exp(logits - mx)
    denom = jnp.sum(e, axis=-1, keepdims=True) + jnp.exp(sink - mx)
    return e * (1.0 / denom)


def _dedup(t):
    lane_lo = lax.broadcasted_iota(jnp.int32, (t.shape[0], LANES), 1) < HALF
    tiles = [jnp.where(lane_lo, t[:, (2 * i) * LANES:(2 * i + 1) * LANES],
                       t[:, (2 * i + 1) * LANES:(2 * i + 2) * LANES]) for i in range(ATTN_KV_HEADS // 2)]
    return jnp.concatenate(tiles, axis=1)


def _swa_prompt_kernel(sinks_ref, xa_ref, xc_ref, sh_ref, sc_ref, gt_ref, gpre_ref, gpost_ref, wq_ref, wk_ref,
                       wv_ref, bq_ref, bk_ref, bv_ref, bias_ref, wo_ref, bo_ref,
                       o_ref, kc_ref, vc_ref, qn_s, kn_s, vn_s, q_s, kbuf, vbuf, att_s, lg_s, p_s,
                       *, steps_per_seq):
    s = pl.program_id(0)
    tq = xa_ref.shape[1]
    n_tiles = 2 * ATTN_KV_HEADS
    pairs = ATTN_REP // 2

    @pl.when(s == 0)
    def _():
        for ref in (qn_s, kn_s, vn_s, kbuf, vbuf):
            ref[...] = jnp.zeros(ref.shape, ref.dtype)

    kbuf[0:WINDOW, :] = kbuf[tq:tq + WINDOW, :]
    vbuf[0:WINDOW, :] = vbuf[tq:tq + WINDOW, :]
    kbuf[WINDOW:WINDOW + tq, :] = kn_s[...]
    vbuf[WINDOW:WINDOW + tq, :] = vn_s[...]
    q_s[...] = qn_s[...]

    hb = _mod_norm(xa_ref[0], gpre_ref[0], sc_ref[0], sh_ref[0]).astype(BF16)
    half_q = ATTN_Q_DIM // 2

    def project_q(lo):
        qn_s[:, lo:lo + half_q] = ((_dot(hb, wq_ref[:, lo:lo + half_q]) + bq_ref[:, lo:lo + half_q])
                                   * (ATTN_HEAD_DIM ** -0.5)).astype(BF16)

    def project_kv(w_ref, b_ref, nxt, cache_ref):
        kv = _dot(hb, w_ref[...]) + b_ref[...]
        nxt[...] = kv.astype(BF16)
        cache_ref[0] = _dedup(kv[tq - WINDOW:, :])

    parts = [functools.partial(project_q, 0), functools.partial(project_q, half_q),
             functools.partial(project_kv, wk_ref, bk_ref, kn_s, kc_ref),
             functools.partial(project_kv, wv_ref, bv_ref, vn_s, vc_ref)]

    lane_lo = lax.broadcasted_iota(jnp.int32, (2 * WINDOW, LANES), 1) < HALF
    zero_b = jnp.zeros((), BF16)
    seq_first = jnp.where(((s + steps_per_seq - 1) % steps_per_seq) == 0, n_tiles, 0)
    n_q = tq // WINDOW
    for bi in range(n_q):
        r0 = bi * WINDOW
        qrows = slice(r0, r0 + WINDOW)
        first = seq_first if bi == 0 else 0
        lg, pb = lg_s.at[bi], p_s.at[bi]
        for g in range(ATTN_KV_HEADS):
            kd = kbuf[r0:r0 + 2 * WINDOW, g * LANES:(g + 1) * LANES]
            q2 = jnp.concatenate([q_s[qrows, (g * pairs + pr) * LANES:(g * pairs + pr + 1) * LANES]
                                  for pr in range(pairs)], axis=0)
            for half in range(2):
                kh = jnp.where(lane_lo, kd, zero_b) if half == 0 else jnp.where(lane_lo, zero_b, kd)
                t = 2 * g + half
                lg[t] = _dot_nt(q2, kh) + bias_ref[first + t]
        for part in parts[bi * len(parts) // n_q:(bi + 1) * len(parts) // n_q]:
            part()
        for t in range(n_tiles):
            for rc in range(2 * WINDOW // SOFTMAX_ROWS):
                rs = slice(rc * SOFTMAX_ROWS, (rc + 1) * SOFTMAX_ROWS)
                h = 4 * (t // 2) + (t % 2) + 2 * ((rc * SOFTMAX_ROWS) // WINDOW)
                pb[t, rs, :] = _softmax_sink(lg[t, rs, :], sinks_ref[h]).astype(BF16)
        for g in range(ATTN_KV_HEADS):
            vd = vbuf[r0:r0 + 2 * WINDOW, g * LANES:(g + 1) * LANES]
            acc = (_dot(pb[2 * g], jnp.where(lane_lo, vd, zero_b))
                   + _dot(pb[2 * g + 1], jnp.where(lane_lo, zero_b, vd)))
            for pr in range(pairs):
                pair = g * pairs + pr
                att_s[qrows, pair * LANES:(pair + 1) * LANES] = acc[pr * WINDOW:(pr + 1) * WINDOW].astype(BF16)
        out = _dot(att_s[qrows, :], wo_ref[...]) + bo_ref[...]
        o_ref[0, qrows, :] = xc_ref[0, qrows, :] + gt_ref[0] * _rms(out, gpost_ref[0])


def _swa_prompt(x, mod, gpre, gpost, w, bias, tq):
    b, l, d = x.shape
    kvw = ATTN_KV_HEADS * LANES
    pairs = ATTN_REP // 2
    n_tiles = 2 * ATTN_KV_HEADS
    spq = l // tq
    n_blocks = b * spq
    bias = bias.reshape(2, ATTN_KV_HEADS, pairs, 2, WINDOW, 2 * WINDOW).transpose(0, 1, 3, 2, 4, 5)
    bias = bias.reshape(2 * n_tiles, pairs * WINDOW, 2 * WINDOW)

    def cur(s):
        blk = jnp.minimum(s, n_blocks - 1)
        return blk // spq, blk % spq

    def prev(s):
        blk = jnp.maximum(s - 1, 0)
        return blk // spq, blk % spq

    xa_spec = pl.BlockSpec((1, tq, d), lambda s: cur(s) + (0,))
    xc_spec = pl.BlockSpec((1, tq, d), lambda s: prev(s) + (0,))
    mod_a = [pl.BlockSpec((1, None, 1, d), functools.partial(lambda k, s: (cur(s)[0], k, 0, 0), 3 + k))
             for k in range(2)]
    mod_c = pl.BlockSpec((1, None, 1, d), lambda s: (prev(s)[0], 5, 0, 0))
    cache_spec = pl.BlockSpec((1, WINDOW, ATTN_KV_DIM), lambda s: (cur(s)[0], 0, 0))
    consts = [w["wq"], w["wk_dup"], w["wv_dup"], w["bq"], w["bk_dup"], w["bv_dup"], bias, w["wo"], w["bo"]]
    n_q = tq // WINDOW
    return pl.pallas_call(
        functools.partial(_swa_prompt_kernel, steps_per_seq=spq),
        out_shape=(jax.ShapeDtypeStruct(x.shape, F32),
                   jax.ShapeDtypeStruct((b, WINDOW, ATTN_KV_DIM), F32),
                   jax.ShapeDtypeStruct((b, WINDOW, ATTN_KV_DIM), F32)),
        grid=(n_blocks + 1,),
        in_specs=[pl.BlockSpec(memory_space=pltpu.SMEM), xa_spec, xc_spec] + mod_a
        + [mod_c, _const_spec((1, 1, d)), _const_spec((1, 1, d))]
        + [_const_spec(a.shape, a.dtype == BF16) for a in consts],
        out_specs=(xc_spec, cache_spec, cache_spec),
        scratch_shapes=[
            pltpu.VMEM((tq, ATTN_Q_DIM), BF16),
            pltpu.VMEM((tq, kvw), BF16),
            pltpu.VMEM((tq, kvw), BF16),
            pltpu.VMEM((tq, ATTN_Q_DIM), BF16),
            pltpu.VMEM((WINDOW + tq, kvw), BF16),
            pltpu.VMEM((WINDOW + tq, kvw), BF16),
            pltpu.VMEM((tq, ATTN_Q_DIM), BF16),
            pltpu.VMEM((n_q, n_tiles, pairs * WINDOW, 2 * WINDOW), F32),
            pltpu.VMEM((n_q, n_tiles, pairs * WINDOW, 2 * WINDOW), BF16),
        ],
        compiler_params=_params(1),
        name="swa_prompt_sublayer",
    )(w["sinks"], x, x, mod, mod, mod, gpre, gpost, *consts)


def _swa_sample_kernel(x_ref, sh_ref, sc_ref, gt_ref, gpre_ref, gpost_ref, wq_ref, wk_ref, wv_ref,
                       bq_ref, bk_ref, bv_ref, ck_ref, cv_ref, bias_ref, sink_ref, wo_ref, bo_ref,
                       o_ref, ko_ref, vo_ref, q_s, kn_s, vn_s, att_s):
    x = x_ref[...]
    bt, lt, d = x.shape
    m = bt * lt
    nbuf = ck_ref.shape[1]
    wide = ATTN_KV_HEADS * ATTN_HEAD_DIM

    hb = _mod_norm(x, gpre_ref[...], sc_ref[...], sh_ref[...]).reshape(m, d).astype(BF16)
    q_s[...] = (_dot(hb, wq_ref[...]) + bq_ref[...]) * (ATTN_HEAD_DIM ** -0.5)
    kn_s[...] = _dot(hb, wk_ref[...]) + bk_ref[...]
    vn_s[...] = _dot(hb, wv_ref[...]) + bv_ref[...]
    zpad = jnp.zeros((bias_ref.shape[1] - nbuf - lt, ATTN_KV_DIM), F32)

    def seq(bi, carry):
        rows = pl.ds(pl.multiple_of(bi * lt, lt), lt)
        kn = kn_s[rows, :]
        vn = vn_s[rows, :]
        ck = ck_ref[bi]
        cv = cv_ref[bi]
        kfull = jnp.concatenate([ck, kn, zpad], axis=0).astype(BF16)
        vfull = jnp.concatenate([cv, vn, zpad], axis=0).astype(BF16)
        ko_ref[bi, 0:nbuf - lt, :] = ck[lt:, :]
        vo_ref[bi, 0:nbuf - lt, :] = cv[lt:, :]
        ko_ref[bi, nbuf - lt:nbuf, :] = kn
        vo_ref[bi, nbuf - lt:nbuf, :] = vn
        qbig = jnp.concatenate([q_s[rows, h * wide:(h + 1) * wide] for h in range(ATTN_HEADS)], axis=0)
        logits = _dot_nt(qbig.astype(BF16), kfull) + bias_ref[...]
        p = _softmax_sink(logits, sink_ref[:, 0:1]).astype(BF16)
        res = _dot(p, vfull)
        att_s[rows, :] = jnp.concatenate([res[h * lt:(h + 1) * lt, :] for h in range(ATTN_HEADS)], axis=1)
        return carry

    lax.fori_loop(0, bt, seq, 0, unroll=8)

    out = (_dot(att_s[...].astype(BF16), wo_ref[...]) + bo_ref[...]).reshape(bt, lt, d)
    o_ref[...] = x + gt_ref[...] * _rms(out, gpost_ref[...])


def _swa_sample(x, mod, gpre, gpost, w, bias, cache_k, cache_v, bt):
    b, lt, d = x.shape
    nbuf = cache_k.shape[1]
    m = bt * lt
    wide = ATTN_KV_HEADS * ATTN_HEAD_DIM
    big = ATTN_HEADS * wide
    keys = 2 * WINDOW
    xspec = pl.BlockSpec((bt, lt, d), lambda i: (i, 0, 0))
    cache_spec = pl.BlockSpec((bt, nbuf, ATTN_KV_DIM), lambda i: (i, 0, 0))
    bias_s = bias[0, :, :lt, :].reshape(ATTN_HEADS * lt, keys)
    sink_col = jnp.broadcast_to(jnp.repeat(w["sinks"], lt)[:, None], (ATTN_HEADS * lt, LANES))
    mods = [pl.BlockSpec((bt, None, 1, D_MODEL), functools.partial(lambda k, i: (i, k, 0, 0), 3 + k))
            for k in range(3)]
    pre = [w["wq_big"], w["wk"], w["wv"], w["bq_big"], w["bk"], w["bv"]]
    post = [bias_s, sink_col, w["wo_big"], w["bo"]]
    return pl.pallas_call(
        _swa_sample_kernel,
        out_shape=(jax.ShapeDtypeStruct(x.shape, F32),
                   jax.ShapeDtypeStruct(cache_k.shape, F32),
                   jax.ShapeDtypeStruct(cache_v.shape, F32)),
        grid=(b // bt,),
        in_specs=[xspec] + mods + [_const_spec((1, 1, d)), _const_spec((1, 1, d))]
        + [_const_spec(a.shape, a.dtype == BF16) for a in pre] + [cache_spec, cache_spec]
        + [_const_spec(a.shape, a.dtype == BF16) for a in post],
        out_specs=(xspec, cache_spec, cache_spec),
        scratch_shapes=[
            pltpu.VMEM((m, big), F32),
            pltpu.VMEM((m, ATTN_KV_DIM), F32),
            pltpu.VMEM((m, ATTN_KV_DIM), F32),
            pltpu.VMEM((m, big), F32),
        ],
        compiler_params=_params(1),
        name="swa_sample_sublayer",
    )(x, mod, mod, mod, gpre, gpost, *pre, cache_k, cache_v, *post)


def _swa_weights(qkv_w, qkv_b, sinks, o_w, o_b):
    d = qkv_w.shape[0]
    wq = qkv_w[:, :ATTN_Q_DIM]
    wk = qkv_w[:, ATTN_Q_DIM:ATTN_Q_DIM + ATTN_KV_DIM]
    wv = qkv_w[:, ATTN_Q_DIM + ATTN_KV_DIM:]
    bq = qkv_b[:ATTN_Q_DIM]
    bk = qkv_b[ATTN_Q_DIM:ATTN_Q_DIM + ATTN_KV_DIM]
    bv = qkv_b[ATTN_Q_DIM + ATTN_KV_DIM:]

    def dup(a):
        a4 = a.reshape(a.shape[:-1] + (ATTN_KV_HEADS, 1, ATTN_HEAD_DIM))
        a4 = jnp.broadcast_to(a4, a.shape[:-1] + (ATTN_KV_HEADS, 2, ATTN_HEAD_DIM))
        return a4.reshape(a.shape[:-1] + (ATTN_KV_HEADS * LANES,))

    own = jnp.asarray((np.arange(ATTN_HEADS)[:, None] // ATTN_REP == np.arange(ATTN_KV_HEADS)[None, :])
                      .astype(np.float32))
    wq_big = (wq.reshape(d, ATTN_HEADS, 1, ATTN_HEAD_DIM) * own[None, :, :, None]).reshape(d, -1)
    bq_big = (bq.reshape(ATTN_HEADS, 1, ATTN_HEAD_DIM) * own[:, :, None]).reshape(1, -1)
    wo_big = (o_w.reshape(ATTN_HEADS, 1, ATTN_HEAD_DIM, d) * own[:, :, None, None]).reshape(-1, d)
    return dict(
        sinks=sinks,
        wq=wq.astype(BF16), wk_dup=dup(wk).astype(BF16), wv_dup=dup(wv).astype(BF16),
        bq=bq.reshape(1, -1), bk_dup=dup(bk).reshape(1, -1), bv_dup=dup(bv).reshape(1, -1),
        wo=o_w.astype(BF16), bo=o_b.reshape(1, d),
        wq_big=wq_big.astype(BF16), bq_big=bq_big, wk=wk.astype(BF16), wv=wv.astype(BF16),
        bk=bk.reshape(1, -1), bv=bv.reshape(1, -1), wo_big=wo_big.astype(BF16),
    )


def _trunk(x, mod_all, state_ssm, state_conv, cache_k, cache_v, norm_pre, norm_post, ffn_in, ffn_out,
           ssm_w, swa_w, bias, tiles):
    bt, lt, n_sub = tiles["ffn"]
    b, l, d = x.shape
    sample = state_ssm is not None
    outs = {}
    for i in range(2):
        mod = mod_all[i].reshape(b, N_SUB * 3, 1, d)
        gpre = norm_pre[i].reshape(N_SUB, 1, 1, d)
        gpost = norm_post[i].reshape(N_SUB, 1, 1, d)
        x = _ffn(x, mod, 0, gpre[0], gpost[0], ffn_in, ffn_out, i, 0, bt, lt, n_sub)
        if i == 0:
            if sample:
                x, conv_new, ssm_new = _ssm_sample(x, mod, gpre[1], gpost[1], ssm_w, state_conv[0], state_ssm[0],
                                                   tiles["ssm"])
            else:
                x, conv_new, ssm_new = _ssm_prompt(x, mod, gpre[1], gpost[1], ssm_w)
            outs["conv"] = conv_new[None]
            outs["ssm"] = ssm_new[None]
        else:
            if sample:
                x, k_new, v_new = _swa_sample(x, mod, gpre[1], gpost[1], swa_w, bias,
                                              cache_k[0].reshape(b, -1, ATTN_KV_DIM),
                                              cache_v[0].reshape(b, -1, ATTN_KV_DIM), tiles["swa"])
            else:
                x, k_new, v_new = _swa_prompt(x, mod, gpre[1], gpost[1], swa_w, bias, tiles["swa"])
            shape = (1, b, -1, ATTN_KV_HEADS, ATTN_HEAD_DIM)
            outs["k"] = k_new.reshape(shape)
            outs["v"] = v_new.reshape(shape)
        x = _ffn(x, mod, 2, gpre[2], gpost[2], ffn_in, ffn_out, i, 1, bt, lt, n_sub)
    return x, outs["ssm"], outs["conv"], outs["k"], outs["v"]


def kernel(x_prompt, x_sample, state_ssm, state_conv, cache_k, cache_v, c_prompt, c_sample, ada_w, ada_b, norm_pre, norm_post, ffn_w_in, ffn_w_out, ssm_in_w, ssm_conv_w, ssm_conv_b, ssm_dt_bias, ssm_a_log, ssm_d, ssm_norm_w, ssm_out_w, attn_qkv_w, attn_qkv_b, attn_sinks, attn_o_w, attn_o_b, rel_bias):
    nb = x_prompt.shape[0]
    mod_all = _ada(jnp.concatenate([c_prompt, c_sample], axis=0), ada_w, ada_b)
    bias = _bias_table(rel_bias)
    ffn_in = ffn_w_in.astype(BF16)
    ffn_out = ffn_w_out.astype(BF16)
    ssm_w = _ssm_weights(ssm_in_w[0], ssm_conv_w[0], ssm_conv_b[0], ssm_dt_bias[0], ssm_a_log[0], ssm_d[0],
                         ssm_norm_w[0], ssm_out_w[0])
    swa_w = _swa_weights(attn_qkv_w[0], attn_qkv_b[0], attn_sinks[0], attn_o_w[0], attn_o_b[0])
    common = (norm_pre, norm_post, ffn_in, ffn_out, ssm_w, swa_w, bias)

    p_tiles = dict(ffn=(1, 1024, 4), swa=512)
    y_p, ssm_p, conv_p, k_p, v_p = _trunk(x_prompt, mod_all[:, :nb], None, None, None, None, *common, p_tiles)
    ns, ls = x_sample.shape[:2]
    s_tiles = dict(ffn=(min(64, ns), ls, 1), ssm=min(8, ns), swa=min(16, ns))
    y_s, ssm_s, conv_s, k_s, v_s = _trunk(x_sample, mod_all[:, nb:], state_ssm, state_conv, cache_k, cache_v,
                                          *common, s_tiles)
    return (y_p, y_s, ssm_p, conv_p, k_p, v_p, ssm_s, conv_s, k_s, v_s)
```

```python
import functools
import math

import numpy as np
import jax
import jax.numpy as jnp
from jax import lax
from jax.experimental import pallas as pl
from jax.experimental.pallas import tpu as pltpu

F32 = jnp.float32
BF16 = jnp.bfloat16

D_MODEL = 1024
N_SUB = 3
RMS_EPS = 1e-6
FFN_RES = 0.5
D_FF = 2816

SSM_D_INNER = 2048
SSM_HEAD_DIM = 64
SSM_HEADS = 32
SSM_GROUPS = 4
SSM_HPG = 8
SSM_STATE = 128
SSM_CONV = 4
SSM_CHUNK = 128
SSM_GN = SSM_GROUPS * SSM_STATE
SSM_CONV_DIM = SSM_D_INNER + 2 * SSM_GN
SSM_GROUP_WIDTH = SSM_HPG * SSM_HEAD_DIM

ATTN_HEAD_DIM = 64
ATTN_HEADS = 16
ATTN_KV_HEADS = 4
ATTN_REP = 4
WINDOW = 128
REL_BUCKETS = 32
ATTN_Q_DIM = ATTN_HEADS * ATTN_HEAD_DIM
ATTN_KV_DIM = ATTN_KV_HEADS * ATTN_HEAD_DIM

LANES = 128
SUBLANES = 8
HALF = LANES // 2
VMEM_LIMIT_BYTES = 56 * 1024 * 1024
CONV_PAD = SUBLANES


def _dot(a, b):
    return jnp.dot(a, b, preferred_element_type=F32)


def _dot_nt(a, b):
    return lax.dot_general(a, b, (((1,), (1,)), ((), ())), preferred_element_type=F32)


def _dot_tn(a, b):
    return lax.dot_general(a, b, (((0,), (0,)), ((), ())), preferred_element_type=F32)


def _silu(x):
    h = 0.5 * x
    return h * jnp.tanh(h) + h


def _rms(x, g):
    return x * lax.rsqrt(jnp.mean(x * x, axis=-1, keepdims=True) + RMS_EPS) * g


def _mod_norm(x, g, scale, shift):
    return _rms(x, g) * (1.0 + scale) + shift


def _split3(x):
    hi = x.astype(BF16)
    r1 = x - hi.astype(F32)
    mid = r1.astype(BF16)
    lo = (r1 - mid.astype(F32)).astype(BF16)
    return hi, mid, lo


def _const_spec(shape, single_buffer=False):
    nd = len(shape)
    kw = {"pipeline_mode": pl.Buffered(1)} if single_buffer else {}
    return pl.BlockSpec(shape, lambda *_: (0,) * nd, **kw)


def _params(n_grid, vmem_limit=VMEM_LIMIT_BYTES):
    return pltpu.CompilerParams(
        dimension_semantics=("arbitrary",) * n_grid,
        vmem_limit_bytes=vmem_limit,
    )


def _ada_kernel(c_ref, w_ref, b_ref, o_ref):
    cs = _silu(c_ref[...]).astype(BF16)
    o_ref[0] = _dot(cs, w_ref[0].astype(BF16)) + b_ref[0]


def _ada(c_all, ada_w, ada_b, tn=1152):
    depth, d, n = ada_w.shape
    bc = c_all.shape[0]
    return pl.pallas_call(
        _ada_kernel,
        out_shape=jax.ShapeDtypeStruct((depth, bc, n), F32),
        grid=(depth, n // tn),
        in_specs=[
            pl.BlockSpec((bc, d), lambda l, j: (0, 0)),
            pl.BlockSpec((1, d, tn), lambda l, j: (l, 0, j)),
            pl.BlockSpec((1, 1, tn), lambda l, j: (l, 0, j)),
        ],
        out_specs=pl.BlockSpec((1, bc, tn), lambda l, j: (l, 0, j)),
        compiler_params=_params(2),
        name="ada_mod",
    )(c_all, ada_w, ada_b.reshape(depth, 1, n))


def _t5_bucket_table():
    i = np.arange(WINDOW)[:, None]
    j = np.arange(2 * WINDOW)[None, :]
    dist = i + WINDOW - j
    exact = REL_BUCKETS // 2
    df = np.maximum(dist, 1).astype(np.float32)
    large = exact + (np.log(df / np.float32(exact)) / np.float32(math.log(WINDOW / exact))
                     * np.float32(REL_BUCKETS - exact)).astype(np.int32)
    large = np.minimum(large, REL_BUCKETS - 1)
    bucket = np.where(dist < exact, dist, large)
    valid = (dist >= 0) & (dist <= WINDOW)
    return np.where(valid, bucket, -1).astype(np.int32)


def _bias_kernel(rb_ref, idx_ref, o_ref):
    h = pl.program_id(0)
    idx = idx_ref[...]
    acc = jnp.full(idx.shape, -jnp.inf, F32)
    for b in range(REL_BUCKETS):
        acc = jnp.where(idx == b, rb_ref[b, h], acc)
    o_ref[0, 0] = acc
    col = lax.broadcasted_iota(jnp.int32, idx.shape, 1)
    o_ref[1, 0] = jnp.where(col < WINDOW, -jnp.inf, acc)


def _bias_table(rel_bias):
    idx = jnp.asarray(_t5_bucket_table())
    return pl.pallas_call(
        _bias_kernel,
        out_shape=jax.ShapeDtypeStruct((2, ATTN_HEADS, WINDOW, 2 * WINDOW), F32),
        grid=(ATTN_HEADS,),
        in_specs=[
            pl.BlockSpec(memory_space=pltpu.SMEM),
            pl.BlockSpec((WINDOW, 2 * WINDOW), lambda h: (0, 0)),
        ],
        out_specs=pl.BlockSpec((2, 1, WINDOW, 2 * WINDOW), lambda h: (0, h, 0, 0)),
        compiler_params=_params(1),
        name="rel_bias_table",
    )(rel_bias, idx)


MXU_TILE = 256
FF_CHUNKS = ((0, D_FF),)


def _ffn_kernel(x_ref, sh_ref, sc_ref, gt_ref, gpre_ref, gpost_ref, win_ref, wout_ref, o_ref, *, ff_chunks, n_sub):
    bt, lt, d = x_ref.shape

    def sub_slices(s):
        if bt == 1:
            return slice(None), slice(s * (lt // n_sub), (s + 1) * (lt // n_sub))
        return slice(s * (bt // n_sub), (s + 1) * (bt // n_sub)), slice(None)

    def pre(s):
        bs, ls = sub_slices(s)
        x = x_ref[bs, ls, :]
        h = _mod_norm(x, gpre_ref[...], sc_ref[bs], sh_ref[bs])
        return h.reshape(x.shape[0] * x.shape[1], d).astype(BF16)

    def post(s, acc):
        bs, ls = sub_slices(s)
        x = x_ref[bs, ls, :]
        o_ref[bs, ls, :] = x + FFN_RES * gt_ref[bs] * _rms(acc.reshape(x.shape), gpost_ref[...])

    hb = pre(0)
    prev = None
    for s in range(n_sub):
        acc = None
        hb_next = None
        for c, (lo, hi) in enumerate(ff_chunks):
            g = _dot(hb, win_ref[:, lo:hi])
            u = _dot(hb, win_ref[:, D_FF + lo:D_FF + hi])
            a = (_silu(g) * u).astype(BF16)
            part = _dot(a, wout_ref[lo:hi, :])
            acc = part if acc is None else acc + part
            if c == 0:
                if prev is not None:
                    post(s - 1, prev)
                if s + 1 < n_sub:
                    hb_next = pre(s + 1)
        prev = acc
        hb = hb_next
    post(n_sub - 1, prev)


def _mod_specs(bt, sub):
    return [pl.BlockSpec((bt, None, 1, D_MODEL), functools.partial(lambda k, i, j: (i, k, 0, 0), sub * 3 + k))
            for k in range(3)]


def _ffn(x, mod, sub, gpre, gpost, w_in, w_out, layer, which, bt, lt, n_sub):
    b, l, d = x.shape
    xspec = pl.BlockSpec((bt, lt, d), lambda i, j: (i, j, 0))
    wspec = [pl.BlockSpec((None, None) + w.shape[2:], lambda i, j: (layer, which, 0, 0),
                          pipeline_mode=pl.Buffered(1)) for w in (w_in, w_out)]
    return pl.pallas_call(
        functools.partial(_ffn_kernel, ff_chunks=FF_CHUNKS, n_sub=n_sub),
        out_shape=jax.ShapeDtypeStruct(x.shape, F32),
        grid=(b // bt, l // lt),
        in_specs=[xspec] + _mod_specs(bt, sub) + [
            _const_spec((1, 1, d)), _const_spec((1, 1, d))] + wspec,
        out_specs=xspec,
        compiler_params=_params(2),
        name="ffn_sublayer",
    )(x, mod, mod, mod, gpre, gpost, w_in, w_out)


def _chunk_consts(qc, seg):
    r = lax.broadcasted_iota(jnp.int32, (qc, qc), 0)
    c = lax.broadcasted_iota(jnp.int32, (qc, qc), 1)
    seg_shift = seg.bit_length() - 1
    same = jnp.right_shift(r, seg_shift) == jnp.right_shift(c, seg_shift)
    causal = same & (r >= c)
    tri = jnp.where(causal, 1.0, 0.0)
    upper = jnp.where(same & (c > r), 1.0, 0.0)
    tu = jnp.concatenate([tri, upper], axis=0).astype(BF16)
    er = lax.broadcasted_iota(jnp.int32, (LANES, LANES), 0)
    ec = lax.broadcasted_iota(jnp.int32, (LANES, LANES), 1)
    eye = jnp.where(er == ec, 1.0, 0.0).astype(BF16)
    lane_lo = lax.broadcasted_iota(jnp.int32, (qc, LANES), 1) < HALF
    return tu, eye, causal, lane_lo


def _ssm_project(hb, wz_ref, wxbc_ref, wdt_ref, dtb_ref, alog_ref):
    z = _dot(hb, wz_ref[...])
    xbc_raw = _dot(hb, wxbc_ref[...])
    dt_raw = _dot(hb, wdt_ref[...]) + dtb_ref[...]
    dt = jnp.maximum(dt_raw, 0.0) + jnp.log1p(jnp.exp(-jnp.abs(dt_raw)))
    return z, xbc_raw, dt, dt * (-jnp.exp(alog_ref[...]))


def _gate_norm(y, xs, z, dx, nw):
    yg = (y + xs * dx) * _silu(z)
    parts = []
    for g in range(SSM_GROUPS):
        v = yg[:, g * SSM_GROUP_WIDTH:(g + 1) * SSM_GROUP_WIDTH]
        parts.append(v * lax.rsqrt(jnp.mean(v * v, axis=-1, keepdims=True) + RMS_EPS))
    return (jnp.concatenate(parts, axis=1) * nw).astype(BF16)


def _cols(ref, rows, c0, c1):
    if len(ref.shape) == 2:
        return ref[rows, c0:c1]
    return jnp.concatenate([ref[c, rows, :] for c in range(c0 // LANES, c1 // LANES)], axis=1)


def _ssd_chunk(r0, qc, seg, s_first, *, xbc_s, dt_s, da_s, y_s, e2_ref, h_in, h_out, consts, reset=None):
    tu, eye, causal, lane_lo = consts
    rows = pl.ds(r0, qc)
    hi, mid, lo = _split3(da_s[rows, :])
    cs2 = _dot(tu, hi) + _dot(tu, mid) + _dot(tu, lo)
    a_cs = cs2[:qc]
    ea = jnp.exp(a_cs)
    dte = jnp.exp(cs2[qc:])

    stack = jnp.concatenate([dt_s[rows, :], ea, dte], axis=0)
    s_hi = stack.astype(BF16)
    s_lo = (stack - s_hi.astype(F32)).astype(BF16)
    sx = _dot(jnp.concatenate([s_hi, s_lo], axis=1), e2_ref[...])
    dt_x, ea_x, dte_x = sx[:qc], sx[qc:2 * qc], sx[2 * qc:]

    xdt = _cols(xbc_s, rows, 0, SSM_D_INNER) * dt_x
    xdt_b = xdt.astype(BF16)
    xd_b = (xdt * dte_x).astype(BF16)
    bm = _cols(xbc_s, rows, SSM_D_INNER, SSM_D_INNER + SSM_GN).astype(BF16)
    cm = _cols(xbc_s, rows, SSM_D_INNER + SSM_GN, SSM_CONV_DIM).astype(BF16)

    a_hi, a_mid, a_lo = _split3(a_cs)
    a_cs_t = _dot_nt(eye, a_hi) + _dot_nt(eye, a_mid) + _dot_nt(eye, a_lo)

    zero_b = jnp.zeros((), BF16)
    for g in range(SSM_GROUPS):
        gsl = slice(g * SSM_STATE, (g + 1) * SSM_STATE)
        cb = _dot_nt(cm[:, gsl], bm[:, gsl])
        for pr in range(SSM_HPG // 2):
            h0 = g * SSM_HPG + 2 * pr
            psl = slice((h0 // 2) * LANES, (h0 // 2 + 1) * LANES)
            xp = xdt_b[:, psl]
            acc = None
            for half in range(2):
                h = h0 + half
                seg_sum = a_cs[:, h:h + 1] - a_cs_t[h:h + 1, :]
                decay = jnp.exp(jnp.where(causal, seg_sum, -jnp.inf))
                w = (decay * cb).astype(BF16)
                xh = jnp.where(lane_lo if half == 0 else jnp.logical_not(lane_lo), xp, zero_b)
                o = _dot(w, xh)
                acc = o if acc is None else acc + o
            y_s[rows, psl] = acc

    for t in range(qc // seg):
        tr = slice(t * seg, (t + 1) * seg)
        trows = pl.ds(r0 + t * seg, seg)
        last = t * seg + seg - 1
        for g in range(SSM_GROUPS):
            gsl = slice(g * SSM_STATE, (g + 1) * SSM_STATE)
            csl = slice(g * SSM_GROUP_WIDTH, (g + 1) * SSM_GROUP_WIDTH)
            hsl = slice(g * SSM_HPG, (g + 1) * SSM_HPG)
            hg = h_in[s_first + t, hsl].reshape(SSM_GROUP_WIDTH, SSM_STATE)
            if reset is not None:
                hg = jnp.where(reset, 0.0, hg)
            y_off = _dot_nt(cm[tr, gsl], hg.astype(BF16)) * ea_x[tr, csl]
            y_s[trows, csl] = y_s[trows, csl] + y_off
            upd = _dot_tn(xd_b[tr, csl], bm[tr, gsl])
            cdec = jnp.concatenate(
                [jnp.broadcast_to(ea[last:last + 1, g * SSM_HPG + r:g * SSM_HPG + r + 1],
                                  (SSM_HEAD_DIM, SSM_STATE)) for r in range(SSM_HPG)], axis=0)
            h_out[s_first + t, hsl] = (hg * cdec + upd).reshape(SSM_HPG, SSM_HEAD_DIM, SSM_STATE)


HIST = SSM_CONV - 1
_SSM_WEIGHT_NAMES = ("wz", "wxbc", "wdt", "conv_w", "conv_b", "dt_bias", "a_log", "d_x", "norm_w", "wo", "e2")
_STATE_SHAPE = (SSM_HEADS, SSM_HEAD_DIM, SSM_STATE)


def _conv_silu(xpad, cw_ref, cb_ref, lt):
    conv = cb_ref[...]
    for k in range(SSM_CONV):
        conv = conv + xpad[:, CONV_PAD - HIST + k:CONV_PAD - HIST + k + lt, :] * cw_ref[k:k + 1, :]
    return _silu(conv)


def _ssm_sample_kernel(x_ref, sh_ref, sc_ref, gt_ref, gpre_ref, gpost_ref, wz_ref, wxbc_ref, wdt_ref, cw_ref,
                       cb_ref, dtb_ref, alog_ref, dx_ref, nw_ref, wo_ref, e2_ref, conv_in_ref, h0_ref,
                       o_ref, conv_out_ref, h_out_ref, xpad, xbc_s, dt_s, da_s, y_s):
    x = x_ref[...]
    bt, lt, d = x.shape
    m = bt * lt
    hb = _mod_norm(x, gpre_ref[...], sc_ref[...], sh_ref[...]).reshape(m, d).astype(BF16)
    z, xbc_raw, dt, da = _ssm_project(hb, wz_ref, wxbc_ref, wdt_ref, dtb_ref, alog_ref)
    dt_s[...] = dt
    da_s[...] = da
    xpad[:, CONV_PAD:CONV_PAD + lt, :] = xbc_raw.reshape(bt, lt, SSM_CONV_DIM)
    xpad[:, CONV_PAD - HIST:CONV_PAD, :] = conv_in_ref[...]
    xbc_s[...] = _conv_silu(xpad, cw_ref, cb_ref, lt).reshape(m, SSM_CONV_DIM)
    conv_out_ref[...] = xpad[:, CONV_PAD + lt - HIST:CONV_PAD + lt, :]
    _ssd_chunk(0, m, lt, 0, xbc_s=xbc_s, dt_s=dt_s, da_s=da_s, y_s=y_s, e2_ref=e2_ref,
               h_in=h0_ref, h_out=h_out_ref, consts=_chunk_consts(m, lt))
    yn = _gate_norm(y_s[...], xbc_s[:, 0:SSM_D_INNER], z, dx_ref[...], nw_ref[...])
    out = _dot(yn, wo_ref[...]).reshape(bt, lt, d)
    o_ref[...] = x + gt_ref[...] * _rms(out, gpost_ref[...])


def _ssm_sample(x, mod, gpre, gpost, w, conv_in, h0, bt):
    b, lt, d = x.shape
    m = bt * lt
    xspec = pl.BlockSpec((bt, lt, d), lambda i: (i, 0, 0))
    conv_spec = pl.BlockSpec((bt, HIST, SSM_CONV_DIM), lambda i: (i, 0, 0))
    state_spec = pl.BlockSpec((bt,) + _STATE_SHAPE, lambda i: (i, 0, 0, 0))
    mods = [pl.BlockSpec((bt, None, 1, d), functools.partial(lambda k, i: (i, k, 0, 0), 3 + k)) for k in range(3)]
    weights = [w[n] for n in _SSM_WEIGHT_NAMES]
    return pl.pallas_call(
        _ssm_sample_kernel,
        out_shape=(jax.ShapeDtypeStruct(x.shape, F32),
                   jax.ShapeDtypeStruct((b, HIST, SSM_CONV_DIM), F32),
                   jax.ShapeDtypeStruct((b,) + _STATE_SHAPE, F32)),
        grid=(b // bt,),
        in_specs=[xspec] + mods + [_const_spec((1, 1, d)), _const_spec((1, 1, d))]
        + [_const_spec(a.shape, a.dtype == BF16) for a in weights] + [conv_spec, state_spec],
        out_specs=(xspec, conv_spec, state_spec),
        scratch_shapes=[
            pltpu.VMEM((bt, CONV_PAD + lt, SSM_CONV_DIM), F32),
            pltpu.VMEM((m, SSM_CONV_DIM), F32),
            pltpu.VMEM((m, LANES), F32),
            pltpu.VMEM((m, LANES), F32),
            pltpu.VMEM((m, SSM_D_INNER), F32),
        ],
        compiler_params=_params(1, 62 * 1024 * 1024),
        name="ssm_sublayer_state",
    )(x, mod, mod, mod, gpre, gpost, *weights, conv_in, h0)


def _ssm_prompt_kernel(xa_ref, xc_ref, sh_ref, sc_ref, gt_ref, gpre_ref, gpost_ref, wz_ref, wxbc_ref, wdt_ref,
                       cw_ref, cb_ref, dtb_ref, alog_ref, dx_ref, nw_ref, wo_ref, e2_ref,
                       o_ref, conv_out_ref, h_out_ref,
                       xpad, z_s, xbc_s, dt_s, da_s, zc_s, xbcc_s, dtc_s, dac_s, y_s, yn_s, ynp_s, h_s,
                       *, steps_per_seq, n_blocks):
    s = pl.program_id(0)
    q = SSM_CHUNK
    lt = 2 * q
    seq_start = (s % steps_per_seq) == 0
    scan_start = ((s + steps_per_seq - 1) % steps_per_seq) == 0
    n_slabs = SSM_CONV_DIM // LANES

    @pl.when(s == 0)
    def _():
        for ref in (z_s, xbc_s, dt_s, da_s, yn_s, h_s):
            ref[...] = jnp.zeros(ref.shape, ref.dtype)

    @pl.when(seq_start)
    def _():
        xpad[:, CONV_PAD - HIST:CONV_PAD, :] = jnp.zeros((n_slabs, HIST, LANES), F32)

    ynp_s[...] = yn_s[...]
    zc_s[...] = z_s[...]
    xbcc_s[...] = xbc_s[...]
    dtc_s[...] = dt_s[...]
    dac_s[...] = da_s[...]
    consts = _chunk_consts(q, q)

    def scan(t, reset):
        rows = slice(t * q, (t + 1) * q)
        xbc_v = xbcc_s.at[:, rows, :]
        _ssd_chunk(0, q, q, 0, xbc_s=xbc_v, dt_s=dtc_s.at[rows], da_s=dac_s.at[rows], y_s=y_s, e2_ref=e2_ref,
                   h_in=h_s, h_out=h_s, consts=consts, reset=reset)
        xs = _cols(xbc_v, slice(None), 0, SSM_D_INNER)
        yn_s[rows, :] = _gate_norm(y_s[...], xs, zc_s[rows, :], dx_ref[...], nw_ref[...])

    out = _dot(ynp_s[...], wo_ref[...])
    o_ref[0] = xc_ref[0] + gt_ref[0] * _rms(out, gpost_ref[0])

    hb = _mod_norm(xa_ref[0], gpre_ref[0], sc_ref[0], sh_ref[0]).astype(BF16)
    z, xbc_raw, dt, da = _ssm_project(hb, wz_ref, wxbc_ref, wdt_ref, dtb_ref, alog_ref)
    z_s[...] = z
    dt_s[...] = dt
    da_s[...] = da
    for c in range(n_slabs):
        xpad[c, CONV_PAD:CONV_PAD + lt, :] = xbc_raw[:, c * LANES:(c + 1) * LANES]

    scan(0, scan_start)

    half = lt // 2
    for c in range(n_slabs):
        csl = slice(c * LANES, (c + 1) * LANES)
        for par in range(2):
            acc = cb_ref[:, csl]
            for k in range(SSM_CONV):
                rows = pl.ds(CONV_PAD - HIST + par + k, half, stride=2)
                acc = acc + xpad[c, rows, :] * cw_ref[k:k + 1, csl]
            xbc_s[c, pl.ds(par, half, stride=2), :] = _silu(acc)
        new_hist = xpad[c, CONV_PAD + lt - HIST:CONV_PAD + lt, :]
        conv_out_ref[0, :, csl] = new_hist
        xpad[c, CONV_PAD - HIST:CONV_PAD, :] = new_hist

    scan(1, None)

    @pl.when(s <= n_blocks)
    def _():
        h_out_ref[...] = h_s[...]


def _ssm_prompt(x, mod, gpre, gpost, w):
    b, l, d = x.shape
    q = SSM_CHUNK
    lt = 2 * q
    spq = l // lt
    n_blocks = b * spq

    def lagged(lag):
        def index(s):
            blk = jnp.clip(s - lag, 0, n_blocks - 1)
            return blk // spq, blk % spq
        return index

    cur, prev, prev2 = lagged(0), lagged(1), lagged(2)
    xa_spec = pl.BlockSpec((1, lt, d), lambda s: cur(s) + (0,))
    xc_spec = pl.BlockSpec((1, lt, d), lambda s: prev2(s) + (0,))
    mod_a = [pl.BlockSpec((1, None, 1, d), functools.partial(lambda k, s: (cur(s)[0], k, 0, 0), 3 + k))
             for k in range(2)]
    mod_c = pl.BlockSpec((1, None, 1, d), lambda s: (prev2(s)[0], 5, 0, 0))
    conv_spec = pl.BlockSpec((1, HIST, SSM_CONV_DIM), lambda s: (cur(s)[0], 0, 0))
    state_spec = pl.BlockSpec((1,) + _STATE_SHAPE, lambda s: (prev(s)[0], 0, 0, 0))
    weights = [w[n] for n in _SSM_WEIGHT_NAMES]
    return pl.pallas_call(
        functools.partial(_ssm_prompt_kernel, steps_per_seq=spq, n_blocks=n_blocks),
        out_shape=(jax.ShapeDtypeStruct(x.shape, F32),
                   jax.ShapeDtypeStruct((b, HIST, SSM_CONV_DIM), F32),
                   jax.ShapeDtypeStruct((b,) + _STATE_SHAPE, F32)),
        grid=(n_blocks + 2,),
        in_specs=[xa_spec, xc_spec] + mod_a + [mod_c, _const_spec((1, 1, d)), _const_spec((1, 1, d))]
        + [_const_spec(a.shape, a.dtype == BF16) for a in weights],
        out_specs=(xc_spec, conv_spec, state_spec),
        scratch_shapes=[
            pltpu.VMEM((SSM_CONV_DIM // LANES, CONV_PAD + lt, LANES), F32),
            pltpu.VMEM((lt, SSM_D_INNER), F32),
            pltpu.VMEM((SSM_CONV_DIM // LANES, lt, LANES), F32),
            pltpu.VMEM((lt, LANES), F32),
            pltpu.VMEM((lt, LANES), F32),
            pltpu.VMEM((lt, SSM_D_INNER), F32),
            pltpu.VMEM((SSM_CONV_DIM // LANES, lt, LANES), F32),
            pltpu.VMEM((lt, LANES), F32),
            pltpu.VMEM((lt, LANES), F32),
            pltpu.VMEM((q, SSM_D_INNER), F32),
            pltpu.VMEM((lt, SSM_D_INNER), BF16),
            pltpu.VMEM((lt, SSM_D_INNER), BF16),
            pltpu.VMEM((1,) + _STATE_SHAPE, F32),
        ],
        compiler_params=_params(1),
        name="ssm_sublayer",
    )(x, x, mod, mod, mod, gpre, gpost, *weights)


def _ssm_weights(in_w, conv_w, conv_b, dt_bias, a_log, d_skip, norm_w, out_w):
    pad = LANES - SSM_HEADS
    heads = np.arange(SSM_D_INNER) // SSM_HEAD_DIM
    e = (np.arange(LANES)[:, None] == heads[None, :]).astype(np.float32)
    return dict(
        wz=in_w[:, :SSM_D_INNER].astype(BF16),
        wxbc=in_w[:, SSM_D_INNER:SSM_D_INNER + SSM_CONV_DIM].astype(BF16),
        wdt=jnp.pad(in_w[:, SSM_D_INNER + SSM_CONV_DIM:], ((0, 0), (0, pad))).astype(BF16),
        conv_w=conv_w,
        conv_b=conv_b.reshape(1, SSM_CONV_DIM),
        dt_bias=jnp.pad(dt_bias, (0, pad)).reshape(1, LANES),
        a_log=jnp.pad(a_log, (0, pad)).reshape(1, LANES),
        d_x=jnp.repeat(d_skip, SSM_HEAD_DIM).reshape(1, SSM_D_INNER),
        norm_w=norm_w.reshape(1, SSM_D_INNER),
        wo=out_w.astype(BF16),
        e2=jnp.asarray(np.concatenate([e, e], axis=0), BF16),
    )


SOFTMAX_ROWS = 64


def _softmax_sink(logits, sink):
    mx = jnp.maximum(jnp.max(logits, axis=-1, keepdims=True), sink)
    e = jnp.exp(logits - mx)
    denom = jnp.sum(e, axis=-1, keepdims=True) + jnp.exp(sink - mx)
    return e * (1.0 / denom)


def _dedup(t):
    lane_lo = lax.broadcasted_iota(jnp.int32, (t.shape[0], LANES), 1) < HALF
    tiles = [jnp.where(lane_lo, t[:, (2 * i) * LANES:(2 * i + 1) * LANES],
                       t[:, (2 * i + 1) * LANES:(2 * i + 2) * LANES]) for i in range(ATTN_KV_HEADS // 2)]
    return jnp.concatenate(tiles, axis=1)


def _swa_prompt_kernel(sinks_ref, xa_ref, xc_ref, sh_ref, sc_ref, gt_ref, gpre_ref, gpost_ref, wq_ref, wk_ref,
                       wv_ref, bq_ref, bk_ref, bv_ref, bias_ref, wo_ref, bo_ref,
                       o_ref, kc_ref, vc_ref, qn_s, kn_s, vn_s, q_s, kbuf, vbuf, att_s, lg_s, p_s,
                       *, steps_per_seq):
    s = pl.program_id(0)
    tq = xa_ref.shape[1]
    n_tiles = 2 * ATTN_KV_HEADS
    pairs = ATTN_REP // 2

    @pl.when(s == 0)
    def _():
        for ref in (qn_s, kn_s, vn_s, kbuf, vbuf):
            ref[...] = jnp.zeros(ref.shape, ref.dtype)

    kbuf[0:WINDOW, :] = kbuf[tq:tq + WINDOW, :]
    vbuf[0:WINDOW, :] = vbuf[tq:tq + WINDOW, :]
    kbuf[WINDOW:WINDOW + tq, :] = kn_s[...]
    vbuf[WINDOW:WINDOW + tq, :] = vn_s[...]
    q_s[...] = qn_s[...]

    hb = _mod_norm(xa_ref[0], gpre_ref[0], sc_ref[0], sh_ref[0]).astype(BF16)
    half_q = ATTN_Q_DIM // 2

    def project_q(lo):
        qn_s[:, lo:lo + half_q] = ((_dot(hb, wq_ref[:, lo:lo + half_q]) + bq_ref[:, lo:lo + half_q])
                                   * (ATTN_HEAD_DIM ** -0.5)).astype(BF16)

    def project_kv(w_ref, b_ref, nxt, cache_ref):
        kv = _dot(hb, w_ref[...]) + b_ref[...]
        nxt[...] = kv.astype(BF16)
        cache_ref[0] = _dedup(kv[tq - WINDOW:, :])

    parts = [functools.partial(project_q, 0), functools.partial(project_q, half_q),
             functools.partial(project_kv, wk_ref, bk_ref, kn_s, kc_ref),
             functools.partial(project_kv, wv_ref, bv_ref, vn_s, vc_ref)]

    lane_lo = lax.broadcasted_iota(jnp.int32, (2 * WINDOW, LANES), 1) < HALF
    zero_b = jnp.zeros((), BF16)
    seq_first = jnp.where(((s + steps_per_seq - 1) % steps_per_seq) == 0, n_tiles, 0)
    n_q = tq // WINDOW
    for bi in range(n_q):
        r0 = bi * WINDOW
        qrows = slice(r0, r0 + WINDOW)
        first = seq_first if bi == 0 else 0
        lg, pb = lg_s.at[bi], p_s.at[bi]
        for g in range(ATTN_KV_HEADS):
            kd = kbuf[r0:r0 + 2 * WINDOW, g * LANES:(g + 1) * LANES]
            q2 = jnp.concatenate([q_s[qrows, (g * pairs + pr) * LANES:(g * pairs + pr + 1) * LANES]
                                  for pr in range(pairs)], axis=0)
            for half in range(2):
                kh = jnp.where(lane_lo, kd, zero_b) if half == 0 else jnp.where(lane_lo, zero_b, kd)
                t = 2 * g + half
                lg[t] = _dot_nt(q2, kh) + bias_ref[first + t]
        for part in parts[bi * len(parts) // n_q:(bi + 1) * len(parts) // n_q]:
            part()
        for t in range(n_tiles):
            for rc in range(2 * WINDOW // SOFTMAX_ROWS):
                rs = slice(rc * SOFTMAX_ROWS, (rc + 1) * SOFTMAX_ROWS)
                h = 4 * (t // 2) + (t % 2) + 2 * ((rc * SOFTMAX_ROWS) // WINDOW)
                pb[t, rs, :] = _softmax_sink(lg[t, rs, :], sinks_ref[h]).astype(BF16)
        for g in range(ATTN_KV_HEADS):
            vd = vbuf[r0:r0 + 2 * WINDOW, g * LANES:(g + 1) * LANES]
            acc = (_dot(pb[2 * g], jnp.where(lane_lo, vd, zero_b))
                   + _dot(pb[2 * g + 1], jnp.where(lane_lo, zero_b, vd)))
            for pr in range(pairs):
                pair = g * pairs + pr
                att_s[qrows, pair * LANES:(pair + 1) * LANES] = acc[pr * WINDOW:(pr + 1) * WINDOW].astype(BF16)
        out = _dot(att_s[qrows, :], wo_ref[...]) + bo_ref[...]
        o_ref[0, qrows, :] = xc_ref[0, qrows, :] + gt_ref[0] * _rms(out, gpost_ref[0])


def _swa_prompt(x, mod, gpre, gpost, w, bias, tq):
    b, l, d = x.shape
    kvw = ATTN_KV_HEADS * LANES
    pairs = ATTN_REP // 2
    n_tiles = 2 * ATTN_KV_HEADS
    spq = l // tq
    n_blocks = b * spq
    bias = bias.reshape(2, ATTN_KV_HEADS, pairs, 2, WINDOW, 2 * WINDOW).transpose(0, 1, 3, 2, 4, 5)
    bias = bias.reshape(2 * n_tiles, pairs * WINDOW, 2 * WINDOW)

    def cur(s):
        blk = jnp.minimum(s, n_blocks - 1)
        return blk // spq, blk % spq

    def prev(s):
        blk = jnp.maximum(s - 1, 0)
        return blk // spq, blk % spq

    xa_spec = pl.BlockSpec((1, tq, d), lambda s: cur(s) + (0,))
    xc_spec = pl.BlockSpec((1, tq, d), lambda s: prev(s) + (0,))
    mod_a = [pl.BlockSpec((1, None, 1, d), functools.partial(lambda k, s: (cur(s)[0], k, 0, 0), 3 + k))
             for k in range(2)]
    mod_c = pl.BlockSpec((1, None, 1, d), lambda s: (prev(s)[0], 5, 0, 0))
    cache_spec = pl.BlockSpec((1, WINDOW, ATTN_KV_DIM), lambda s: (cur(s)[0], 0, 0))
    consts = [w["wq"], w["wk_dup"], w["wv_dup"], w["bq"], w["bk_dup"], w["bv_dup"], bias, w["wo"], w["bo"]]
    n_q = tq // WINDOW
    return pl.pallas_call(
        functools.partial(_swa_prompt_kernel, steps_per_seq=spq),
        out_shape=(jax.ShapeDtypeStruct(x.shape, F32),
                   jax.ShapeDtypeStruct((b, WINDOW, ATTN_KV_DIM), F32),
                   jax.ShapeDtypeStruct((b, WINDOW, ATTN_KV_DIM), F32)),
        grid=(n_blocks + 1,),
        in_specs=[pl.BlockSpec(memory_space=pltpu.SMEM), xa_spec, xc_spec] + mod_a
        + [mod_c, _const_spec((1, 1, d)), _const_spec((1, 1, d))]
        + [_const_spec(a.shape, a.dtype == BF16) for a in consts],
        out_specs=(xc_spec, cache_spec, cache_spec),
        scratch_shapes=[
            pltpu.VMEM((tq, ATTN_Q_DIM), BF16),
            pltpu.VMEM((tq, kvw), BF16),
            pltpu.VMEM((tq, kvw), BF16),
            pltpu.VMEM((tq, ATTN_Q_DIM), BF16),
            pltpu.VMEM((WINDOW + tq, kvw), BF16),
            pltpu.VMEM((WINDOW + tq, kvw), BF16),
            pltpu.VMEM((tq, ATTN_Q_DIM), BF16),
            pltpu.VMEM((n_q, n_tiles, pairs * WINDOW, 2 * WINDOW), F32),
            pltpu.VMEM((n_q, n_tiles, pairs * WINDOW, 2 * WINDOW), BF16),
        ],
        compiler_params=_params(1),
        name="swa_prompt_sublayer",
    )(w["sinks"], x, x, mod, mod, mod, gpre, gpost, *consts)


def _swa_sample_kernel(x_ref, sh_ref, sc_ref, gt_ref, gpre_ref, gpost_ref, wq_ref, wk_ref, wv_ref,
                       bq_ref, bk_ref, bv_ref, ck_ref, cv_ref, bias_ref, sink_ref, wo_ref, bo_ref,
                       o_ref, ko_ref, vo_ref, q_s, kn_s, vn_s, att_s):
    x = x_ref[...]
    bt, lt, d = x.shape
    m = bt * lt
    nbuf = ck_ref.shape[1]
    wide = ATTN_KV_HEADS * ATTN_HEAD_DIM

    hb = _mod_norm(x, gpre_ref[...], sc_ref[...], sh_ref[...]).reshape(m, d).astype(BF16)
    q_s[...] = (_dot(hb, wq_ref[...]) + bq_ref[...]) * (ATTN_HEAD_DIM ** -0.5)
    kn_s[...] = _dot(hb, wk_ref[...]) + bk_ref[...]
    vn_s[...] = _dot(hb, wv_ref[...]) + bv_ref[...]
    zpad = jnp.zeros((bias_ref.shape[1] - nbuf - lt, ATTN_KV_DIM), F32)

    def seq(bi, carry):
        rows = pl.ds(pl.multiple_of(bi * lt, lt), lt)
        kn = kn_s[rows, :]
        vn = vn_s[rows, :]
        ck = ck_ref[bi]
        cv = cv_ref[bi]
        kfull = jnp.concatenate([ck, kn, zpad], axis=0).astype(BF16)
        vfull = jnp.concatenate([cv, vn, zpad], axis=0).astype(BF16)
        ko_ref[bi, 0:nbuf - lt, :] = ck[lt:, :]
        vo_ref[bi, 0:nbuf - lt, :] = cv[lt:, :]
        ko_ref[bi, nbuf - lt:nbuf, :] = kn
        vo_ref[bi, nbuf - lt:nbuf, :] = vn
        qbig = jnp.concatenate([q_s[rows, h * wide:(h + 1) * wide] for h in range(ATTN_HEADS)], axis=0)
        logits = _dot_nt(qbig.astype(BF16), kfull) + bias_ref[...]
        p = _softmax_sink(logits, sink_ref[:, 0:1]).astype(BF16)
        res = _dot(p, vfull)
        att_s[rows, :] = jnp.concatenate([res[h * lt:(h + 1) * lt, :] for h in range(ATTN_HEADS)], axis=1)
        return carry

    lax.fori_loop(0, bt, seq, 0, unroll=8)

    out = (_dot(att_s[...].astype(BF16), wo_ref[...]) + bo_ref[...]).reshape(bt, lt, d)
    o_ref[...] = x + gt_ref[...] * _rms(out, gpost_ref[...])


def _swa_sample(x, mod, gpre, gpost, w, bias, cache_k, cache_v, bt):
    b, lt, d = x.shape
    nbuf = cache_k.shape[1]
    m = bt * lt
    wide = ATTN_KV_HEADS * ATTN_HEAD_DIM
    big = ATTN_HEADS * wide
    keys = 2 * WINDOW
    xspec = pl.BlockSpec((bt, lt, d), lambda i: (i, 0, 0))
    cache_spec = pl.BlockSpec((bt, nbuf, ATTN_KV_DIM), lambda i: (i, 0, 0))
    bias_s = bias[0, :, :lt, :].reshape(ATTN_HEADS * lt, keys)
    sink_col = jnp.broadcast_to(jnp.repeat(w["sinks"], lt)[:, None], (ATTN_HEADS * lt, LANES))
    mods = [pl.BlockSpec((bt, None, 1, D_MODEL), functools.partial(lambda k, i: (i, k, 0, 0), 3 + k))
            for k in range(3)]
    pre = [w["wq_big"], w["wk"], w["wv"], w["bq_big"], w["bk"], w["bv"]]
    post = [bias_s, sink_col, w["wo_big"], w["bo"]]
    return pl.pallas_call(
        _swa_sample_kernel,
        out_shape=(jax.ShapeDtypeStruct(x.shape, F32),
                   jax.ShapeDtypeStruct(cache_k.shape, F32),
                   jax.ShapeDtypeStruct(cache_v.shape, F32)),
        grid=(b // bt,),
        in_specs=[xspec] + mods + [_const_spec((1, 1, d)), _const_spec((1, 1, d))]
        + [_const_spec(a.shape, a.dtype == BF16) for a in pre] + [cache_spec, cache_spec]
        + [_const_spec(a.shape, a.dtype == BF16) for a in post],
        out_specs=(xspec, cache_spec, cache_spec),
        scratch_shapes=[
            pltpu.VMEM((m, big), F32),
            pltpu.VMEM((m, ATTN_KV_DIM), F32),
            pltpu.VMEM((m, ATTN_KV_DIM), F32),
            pltpu.VMEM((m, big), F32),
        ],
        compiler_params=_params(1),
        name="swa_sample_sublayer",
    )(x, mod, mod, mod, gpre, gpost, *pre, cache_k, cache_v, *post)


def _swa_weights(qkv_w, qkv_b, sinks, o_w, o_b):
    d = qkv_w.shape[0]
    wq = qkv_w[:, :ATTN_Q_DIM]
    wk = qkv_w[:, ATTN_Q_DIM:ATTN_Q_DIM + ATTN_KV_DIM]
    wv = qkv_w[:, ATTN_Q_DIM + ATTN_KV_DIM:]
    bq = qkv_b[:ATTN_Q_DIM]
    bk = qkv_b[ATTN_Q_DIM:ATTN_Q_DIM + ATTN_KV_DIM]
    bv = qkv_b[ATTN_Q_DIM + ATTN_KV_DIM:]

    def dup(a):
        a4 = a.reshape(a.shape[:-1] + (ATTN_KV_HEADS, 1, ATTN_HEAD_DIM))
        a4 = jnp.broadcast_to(a4, a.shape[:-1] + (ATTN_KV_HEADS, 2, ATTN_HEAD_DIM))
        return a4.reshape(a.shape[:-1] + (ATTN_KV_HEADS * LANES,))

    own = jnp.asarray((np.arange(ATTN_HEADS)[:, None] // ATTN_REP == np.arange(ATTN_KV_HEADS)[None, :])
                      .astype(np.float32))
    wq_big = (wq.reshape(d, ATTN_HEADS, 1, ATTN_HEAD_DIM) * own[None, :, :, None]).reshape(d, -1)
    bq_big = (bq.reshape(ATTN_HEADS, 1, ATTN_HEAD_DIM) * own[:, :, None]).reshape(1, -1)
    wo_big = (o_w.reshape(ATTN_HEADS, 1, ATTN_HEAD_DIM, d) * own[:, :, None, None]).reshape(-1, d)
    return dict(
        sinks=sinks,
        wq=wq.astype(BF16), wk_dup=dup(wk).astype(BF16), wv_dup=dup(wv).astype(BF16),
        bq=bq.reshape(1, -1), bk_dup=dup(bk).reshape(1, -1), bv_dup=dup(bv).reshape(1, -1),
        wo=o_w.astype(BF16), bo=o_b.reshape(1, d),
        wq_big=wq_big.astype(BF16), bq_big=bq_big, wk=wk.astype(BF16), wv=wv.astype(BF16),
        bk=bk.reshape(1, -1), bv=bv.reshape(1, -1), wo_big=wo_big.astype(BF16),
    )


def _trunk(x, mod_all, state_ssm, state_conv, cache_k, cache_v, norm_pre, norm_post, ffn_in, ffn_out,
           ssm_w, swa_w, bias, tiles):
    bt, lt, n_sub = tiles["ffn"]
    b, l, d = x.shape
    sample = state_ssm is not None
    outs = {}
    for i in range(2):
        mod = mod_all[i].reshape(b, N_SUB * 3, 1, d)
        gpre = norm_pre[i].reshape(N_SUB, 1, 1, d)
        gpost = norm_post[i].reshape(N_SUB, 1, 1, d)
        x = _ffn(x, mod, 0, gpre[0], gpost[0], ffn_in, ffn_out, i, 0, bt, lt, n_sub)
        if i == 0:
            if sample:
                x, conv_new, ssm_new = _ssm_sample(x, mod, gpre[1], gpost[1], ssm_w, state_conv[0], state_ssm[0],
                                                   tiles["ssm"])
            else:
                x, conv_new, ssm_new = _ssm_prompt(x, mod, gpre[1], gpost[1], ssm_w)
            outs["conv"] = conv_new[None]
            outs["ssm"] = ssm_new[None]
        else:
            if sample:
                x, k_new, v_new = _swa_sample(x, mod, gpre[1], gpost[1], swa_w, bias,
                                              cache_k[0].reshape(b, -1, ATTN_KV_DIM),
                                              cache_v[0].reshape(b, -1, ATTN_KV_DIM), tiles["swa"])
            else:
                x, k_new, v_new = _swa_prompt(x, mod, gpre[1], gpost[1], swa_w, bias, tiles["swa"])
            shape = (1, b, -1, ATTN_KV_HEADS, ATTN_HEAD_DIM)
            outs["k"] = k_new.reshape(shape)
            outs["v"] = v_new.reshape(shape)
        x = _ffn(x, mod, 2, gpre[2], gpost[2], ffn_in, ffn_out, i, 1, bt, lt, n_sub)
    return x, outs["ssm"], outs["conv"], outs["k"], outs["v"]


def kernel(x_prompt, x_sample, state_ssm, state_conv, cache_k, cache_v, c_prompt, c_sample, ada_w, ada_b, norm_pre, norm_post, ffn_w_in, ffn_w_out, ssm_in_w, ssm_conv_w, ssm_conv_b, ssm_dt_bias, ssm_a_log, ssm_d, ssm_norm_w, ssm_out_w, attn_qkv_w, attn_qkv_b, attn_sinks, attn_o_w, attn_o_b, rel_bias):
    nb = x_prompt.shape[0]
    mod_all = _ada(jnp.concatenate([c_prompt, c_sample], axis=0), ada_w, ada_b)
    bias = _bias_table(rel_bias)
    ffn_in = ffn_w_in.astype(BF16)
    ffn_out = ffn_w_out.astype(BF16)
    ssm_w = _ssm_weights(ssm_in_w[0], ssm_conv_w[0], ssm_conv_b[0], ssm_dt_bias[0], ssm_a_log[0], ssm_d[0],
                         ssm_norm_w[0], ssm_out_w[0])
    swa_w = _swa_weights(attn_qkv_w[0], attn_qkv_b[0], attn_sinks[0], attn_o_w[0], attn_o_b[0])
    common = (norm_pre, norm_post, ffn_in, ffn_out, ssm_w, swa_w, bias)

    p_tiles = dict(ffn=(1, 1024, 2), swa=512)
    y_p, ssm_p, conv_p, k_p, v_p = _trunk(x_prompt, mod_all[:, :nb], None, None, None, None, *common, p_tiles)
    ns, ls = x_sample.shape[:2]
    s_tiles = dict(ffn=(min(64, ns), ls, 1), ssm=min(8, ns), swa=min(16, ns))
    y_s, ssm_s, conv_s, k_s, v_s = _trunk(x_sample, mod_all[:, nb:], state_ssm, state_conv, cache_k, cache_v,
                                          *common, s_tiles)
    return (y_p, y_s, ssm_p, conv_p, k_p, v_p, ssm_s, conv_s, k_s, v_s)
```

```python
import functools
import math

import numpy as np
import jax
import jax.numpy as jnp
from jax import lax
from jax.experimental import pallas as pl
from jax.experimental.pallas import tpu as pltpu

F32 = jnp.float32
BF16 = jnp.bfloat16

D_MODEL = 1024
N_SUB = 3
RMS_EPS = 1e-6
FFN_RES = 0.5
D_FF = 2816

SSM_D_INNER = 2048
SSM_HEAD_DIM = 64
SSM_HEADS = 32
SSM_GROUPS = 4
SSM_HPG = 8
SSM_STATE = 128
SSM_CONV = 4
SSM_CHUNK = 128
SSM_GN = SSM_GROUPS * SSM_STATE
SSM_CONV_DIM = SSM_D_INNER + 2 * SSM_GN
SSM_GROUP_WIDTH = SSM_HPG * SSM_HEAD_DIM

ATTN_HEAD_DIM = 64
ATTN_HEADS = 16
ATTN_KV_HEADS = 4
ATTN_REP = 4
WINDOW = 128
REL_BUCKETS = 32
ATTN_Q_DIM = ATTN_HEADS * ATTN_HEAD_DIM
ATTN_KV_DIM = ATTN_KV_HEADS * ATTN_HEAD_DIM

LANES = 128
SUBLANES = 8
HALF = LANES // 2
VMEM_LIMIT_BYTES = 56 * 1024 * 1024
SSM_SAMPLE_VMEM_LIMIT_BYTES = 62 * 1024 * 1024
CONV_PAD = SUBLANES

ADA_TILE_COLS = 9 * LANES
FFN_TILE_ROWS = 1024
FFN_SUB_TILES = 2
SWA_TILE_ROWS = 512
SAMPLE_FFN_SEQS = 64
SAMPLE_SSM_SEQS = 8
SAMPLE_SWA_SEQS = 16


def _dot(a, b):
    return jnp.dot(a, b, preferred_element_type=F32)


def _dot_nt(a, b):
    return lax.dot_general(a, b, (((1,), (1,)), ((), ())), preferred_element_type=F32)


def _dot_tn(a, b):
    return lax.dot_general(a, b, (((0,), (0,)), ((), ())), preferred_element_type=F32)


def _silu(x):
    h = 0.5 * x
    return h * jnp.tanh(h) + h


def _rms(x, g):
    return x * lax.rsqrt(jnp.mean(x * x, axis=-1, keepdims=True) + RMS_EPS) * g


def _mod_norm(x, g, scale, shift):
    return _rms(x, g) * (1.0 + scale) + shift


def _split3(x):
    hi = x.astype(BF16)
    r1 = x - hi.astype(F32)
    mid = r1.astype(BF16)
    lo = (r1 - mid.astype(F32)).astype(BF16)
    return hi, mid, lo


def _const_spec(shape, single_buffer=False):
    nd = len(shape)
    kw = {"pipeline_mode": pl.Buffered(1)} if single_buffer else {}
    return pl.BlockSpec(shape, lambda *_: (0,) * nd, **kw)


def _params(n_grid, vmem_limit=VMEM_LIMIT_BYTES):
    return pltpu.CompilerParams(
        dimension_semantics=("arbitrary",) * n_grid,
        vmem_limit_bytes=vmem_limit,
    )


def _ada_kernel(c_ref, w_ref, b_ref, o_ref):
    cs = _silu(c_ref[...]).astype(BF16)
    o_ref[0] = _dot(cs, w_ref[0].astype(BF16)) + b_ref[0]


def _ada(c_all, ada_w, ada_b, tn=ADA_TILE_COLS):
    depth, d, n = ada_w.shape
    bc = c_all.shape[0]
    return pl.pallas_call(
        _ada_kernel,
        out_shape=jax.ShapeDtypeStruct((depth, bc, n), F32),
        grid=(depth, n // tn),
        in_specs=[
            pl.BlockSpec((bc, d), lambda l, j: (0, 0)),
            pl.BlockSpec((1, d, tn), lambda l, j: (l, 0, j)),
            pl.BlockSpec((1, 1, tn), lambda l, j: (l, 0, j)),
        ],
        out_specs=pl.BlockSpec((1, bc, tn), lambda l, j: (l, 0, j)),
        compiler_params=_params(2),
        name="ada_mod",
    )(c_all, ada_w, ada_b.reshape(depth, 1, n))


def _t5_bucket_table():
    i = np.arange(WINDOW)[:, None]
    j = np.arange(2 * WINDOW)[None, :]
    dist = i + WINDOW - j
    exact = REL_BUCKETS // 2
    df = np.maximum(dist, 1).astype(np.float32)
    large = exact + (np.log(df / np.float32(exact)) / np.float32(math.log(WINDOW / exact))
                     * np.float32(REL_BUCKETS - exact)).astype(np.int32)
    large = np.minimum(large, REL_BUCKETS - 1)
    bucket = np.where(dist < exact, dist, large)
    valid = (dist >= 0) & (dist <= WINDOW)
    return np.where(valid, bucket, -1).astype(np.int32)


def _bias_kernel(rb_ref, idx_ref, o_ref):
    h = pl.program_id(0)
    idx = idx_ref[...]
    acc = jnp.full(idx.shape, -jnp.inf, F32)
    for b in range(REL_BUCKETS):
        acc = jnp.where(idx == b, rb_ref[b, h], acc)
    o_ref[0, 0] = acc
    col = lax.broadcasted_iota(jnp.int32, idx.shape, 1)
    o_ref[1, 0] = jnp.where(col < WINDOW, -jnp.inf, acc)


def _bias_table(rel_bias):
    idx = jnp.asarray(_t5_bucket_table())
    return pl.pallas_call(
        _bias_kernel,
        out_shape=jax.ShapeDtypeStruct((2, ATTN_HEADS, WINDOW, 2 * WINDOW), F32),
        grid=(ATTN_HEADS,),
        in_specs=[
            pl.BlockSpec(memory_space=pltpu.SMEM),
            pl.BlockSpec((WINDOW, 2 * WINDOW), lambda h: (0, 0)),
        ],
        out_specs=pl.BlockSpec((2, 1, WINDOW, 2 * WINDOW), lambda h: (0, h, 0, 0)),
        compiler_params=_params(1),
        name="rel_bias_table",
    )(rel_bias, idx)


def _ffn_kernel(x_ref, sh_ref, sc_ref, gt_ref, gpre_ref, gpost_ref, win_ref, wout_ref, o_ref, *, n_sub):
    bt, lt, d = x_ref.shape

    def sub_slices(s):
        if bt == 1:
            return slice(None), slice(s * (lt // n_sub), (s + 1) * (lt // n_sub))
        return slice(s * (bt // n_sub), (s + 1) * (bt // n_sub)), slice(None)

    def pre(s):
        bs, ls = sub_slices(s)
        x = x_ref[bs, ls, :]
        h = _mod_norm(x, gpre_ref[...], sc_ref[bs], sh_ref[bs])
        return h.reshape(x.shape[0] * x.shape[1], d).astype(BF16)

    def post(s, acc):
        bs, ls = sub_slices(s)
        x = x_ref[bs, ls, :]
        o_ref[bs, ls, :] = x + FFN_RES * gt_ref[bs] * _rms(acc.reshape(x.shape), gpost_ref[...])

    hb = pre(0)
    prev = None
    for s in range(n_sub):
        g = _dot(hb, win_ref[:, 0:D_FF])
        u = _dot(hb, win_ref[:, D_FF:2 * D_FF])
        acc = _dot((_silu(g) * u).astype(BF16), wout_ref[...])
        if prev is not None:
            post(s - 1, prev)
        if s + 1 < n_sub:
            hb = pre(s + 1)
        prev = acc
    post(n_sub - 1, prev)


def _mod_specs(bt, sub):
    return [pl.BlockSpec((bt, None, 1, D_MODEL), functools.partial(lambda k, i, j: (i, k, 0, 0), sub * 3 + k))
            for k in range(3)]


def _ffn(x, mod, sub, gpre, gpost, w_in, w_out, layer, which, bt, lt, n_sub):
    b, l, d = x.shape
    assert b % bt == 0 and l % lt == 0 and (lt if bt == 1 else bt) % n_sub == 0
    xspec = pl.BlockSpec((bt, lt, d), lambda i, j: (i, j, 0))
    wspec = [pl.BlockSpec((None, None) + w.shape[2:], lambda i, j: (layer, which, 0, 0),
                          pipeline_mode=pl.Buffered(1)) for w in (w_in, w_out)]
    return pl.pallas_call(
        functools.partial(_ffn_kernel, n_sub=n_sub),
        out_shape=jax.ShapeDtypeStruct(x.shape, F32),
        grid=(b // bt, l // lt),
        in_specs=[xspec] + _mod_specs(bt, sub) + [
            _const_spec((1, 1, d)), _const_spec((1, 1, d))] + wspec,
        out_specs=xspec,
        compiler_params=_params(2),
        name="ffn_sublayer",
    )(x, mod, mod, mod, gpre, gpost, w_in, w_out)


def _chunk_consts(qc, seg):
    r = lax.broadcasted_iota(jnp.int32, (qc, qc), 0)
    c = lax.broadcasted_iota(jnp.int32, (qc, qc), 1)
    seg_shift = seg.bit_length() - 1
    same = jnp.right_shift(r, seg_shift) == jnp.right_shift(c, seg_shift)
    causal = same & (r >= c)
    tri = jnp.where(causal, 1.0, 0.0)
    upper = jnp.where(same & (c > r), 1.0, 0.0)
    tu = jnp.concatenate([tri, upper], axis=0).astype(BF16)
    er = lax.broadcasted_iota(jnp.int32, (LANES, LANES), 0)
    ec = lax.broadcasted_iota(jnp.int32, (LANES, LANES), 1)
    eye = jnp.where(er == ec, 1.0, 0.0).astype(BF16)
    lane_lo = lax.broadcasted_iota(jnp.int32, (qc, LANES), 1) < HALF
    return tu, eye, causal, lane_lo


def _ssm_project(hb, wz_ref, wxbc_ref, wdt_ref, dtb_ref, alog_ref):
    z = _dot(hb, wz_ref[...])
    xbc_raw = _dot(hb, wxbc_ref[...])
    dt_raw = _dot(hb, wdt_ref[...]) + dtb_ref[...]
    dt = jnp.maximum(dt_raw, 0.0) + jnp.log1p(jnp.exp(-jnp.abs(dt_raw)))
    return z, xbc_raw, dt, dt * (-jnp.exp(alog_ref[...]))


def _gate_norm(y, xs, z, dx, nw):
    yg = (y + xs * dx) * _silu(z)
    parts = []
    for g in range(SSM_GROUPS):
        v = yg[:, g * SSM_GROUP_WIDTH:(g + 1) * SSM_GROUP_WIDTH]
        parts.append(v * lax.rsqrt(jnp.mean(v * v, axis=-1, keepdims=True) + RMS_EPS))
    return (jnp.concatenate(parts, axis=1) * nw).astype(BF16)


def _cols(ref, rows, c0, c1):
    if len(ref.shape) == 2:
        return ref[rows, c0:c1]
    return jnp.concatenate([ref[c, rows, :] for c in range(c0 // LANES, c1 // LANES)], axis=1)


def _ssd_chunk(r0, qc, seg, s_first, *, xbc_s, dt_s, da_s, y_s, e2_ref, h_in, h_out, consts, reset=None):
    tu, eye, causal, lane_lo = consts
    rows = pl.ds(r0, qc)
    hi, mid, lo = _split3(da_s[rows, :])
    cs2 = _dot(tu, hi) + _dot(tu, mid) + _dot(tu, lo)
    a_cs = cs2[:qc]
    ea = jnp.exp(a_cs)
    dte = jnp.exp(cs2[qc:])

    stack = jnp.concatenate([dt_s[rows, :], ea, dte], axis=0)
    s_hi = stack.astype(BF16)
    s_lo = (stack - s_hi.astype(F32)).astype(BF16)
    sx = _dot(jnp.concatenate([s_hi, s_lo], axis=1), e2_ref[...])
    dt_x, ea_x, dte_x = sx[:qc], sx[qc:2 * qc], sx[2 * qc:]

    xdt = _cols(xbc_s, rows, 0, SSM_D_INNER) * dt_x
    xdt_b = xdt.astype(BF16)
    xd_b = (xdt * dte_x).astype(BF16)
    bm = _cols(xbc_s, rows, SSM_D_INNER, SSM_D_INNER + SSM_GN).astype(BF16)
    cm = _cols(xbc_s, rows, SSM_D_INNER + SSM_GN, SSM_CONV_DIM).astype(BF16)

    a_hi, a_mid, a_lo = _split3(a_cs)
    a_cs_t = _dot_nt(eye, a_hi) + _dot_nt(eye, a_mid) + _dot_nt(eye, a_lo)

    zero_b = jnp.zeros((), BF16)
    for g in range(SSM_GROUPS):
        gsl = slice(g * SSM_STATE, (g + 1) * SSM_STATE)
        cb = _dot_nt(cm[:, gsl], bm[:, gsl])
        for pr in range(SSM_HPG // 2):
            h0 = g * SSM_HPG + 2 * pr
            psl = slice((h0 // 2) * LANES, (h0 // 2 + 1) * LANES)
            xp = xdt_b[:, psl]
            acc = None
            for half in range(2):
                h = h0 + half
                seg_sum = a_cs[:, h:h + 1] - a_cs_t[h:h + 1, :]
                decay = jnp.exp(jnp.where(causal, seg_sum, -jnp.inf))
                w = (decay * cb).astype(BF16)
                xh = jnp.where(lane_lo if half == 0 else jnp.logical_not(lane_lo), xp, zero_b)
                o = _dot(w, xh)
                acc = o if acc is None else acc + o
            y_s[rows, psl] = acc

    for t in range(qc // seg):
        tr = slice(t * seg, (t + 1) * seg)
        trows = pl.ds(r0 + t * seg, seg)
        last = t * seg + seg - 1
        for g in range(SSM_GROUPS):
            gsl = slice(g * SSM_STATE, (g + 1) * SSM_STATE)
            csl = slice(g * SSM_GROUP_WIDTH, (g + 1) * SSM_GROUP_WIDTH)
            hsl = slice(g * SSM_HPG, (g + 1) * SSM_HPG)
            hg = h_in[s_first + t, hsl].reshape(SSM_GROUP_WIDTH, SSM_STATE)
            if reset is not None:
                hg = jnp.where(reset, 0.0, hg)
            y_off = _dot_nt(cm[tr, gsl], hg.astype(BF16)) * ea_x[tr, csl]
            y_s[trows, csl] = y_s[trows, csl] + y_off
            upd = _dot_tn(xd_b[tr, csl], bm[tr, gsl])
            cdec = jnp.concatenate(
                [jnp.broadcast_to(ea[last:last + 1, g * SSM_HPG + r:g * SSM_HPG + r + 1],
                                  (SSM_HEAD_DIM, SSM_STATE)) for r in range(SSM_HPG)], axis=0)
            h_out[s_first + t, hsl] = (hg * cdec + upd).reshape(SSM_HPG, SSM_HEAD_DIM, SSM_STATE)


HIST = SSM_CONV - 1
_SSM_WEIGHT_NAMES = ("wz", "wxbc", "wdt", "conv_w", "conv_b", "dt_bias", "a_log", "d_x", "norm_w", "wo", "e2")
_STATE_SHAPE = (SSM_HEADS, SSM_HEAD_DIM, SSM_STATE)


def _conv_silu(xpad, cw_ref, cb_ref, lt):
    conv = cb_ref[...]
    for k in range(SSM_CONV):
        conv = conv + xpad[:, CONV_PAD - HIST + k:CONV_PAD - HIST + k + lt, :] * cw_ref[k:k + 1, :]
    return _silu(conv)


def _ssm_sample_kernel(x_ref, sh_ref, sc_ref, gt_ref, gpre_ref, gpost_ref, wz_ref, wxbc_ref, wdt_ref, cw_ref,
                       cb_ref, dtb_ref, alog_ref, dx_ref, nw_ref, wo_ref, e2_ref, conv_in_ref, h0_ref,
                       o_ref, conv_out_ref, h_out_ref, xpad, xbc_s, dt_s, da_s, y_s):
    x = x_ref[...]
    bt, lt, d = x.shape
    m = bt * lt
    hb = _mod_norm(x, gpre_ref[...], sc_ref[...], sh_ref[...]).reshape(m, d).astype(BF16)
    z, xbc_raw, dt, da = _ssm_project(hb, wz_ref, wxbc_ref, wdt_ref, dtb_ref, alog_ref)
    dt_s[...] = dt
    da_s[...] = da
    xpad[:, CONV_PAD:CONV_PAD + lt, :] = xbc_raw.reshape(bt, lt, SSM_CONV_DIM)
    xpad[:, CONV_PAD - HIST:CONV_PAD, :] = conv_in_ref[...]
    xbc_s[...] = _conv_silu(xpad, cw_ref, cb_ref, lt).reshape(m, SSM_CONV_DIM)
    conv_out_ref[...] = xpad[:, CONV_PAD + lt - HIST:CONV_PAD + lt, :]
    _ssd_chunk(0, m, lt, 0, xbc_s=xbc_s, dt_s=dt_s, da_s=da_s, y_s=y_s, e2_ref=e2_ref,
               h_in=h0_ref, h_out=h_out_ref, consts=_chunk_consts(m, lt))
    yn = _gate_norm(y_s[...], xbc_s[:, 0:SSM_D_INNER], z, dx_ref[...], nw_ref[...])
    out = _dot(yn, wo_ref[...]).reshape(bt, lt, d)
    o_ref[...] = x + gt_ref[...] * _rms(out, gpost_ref[...])


def _ssm_sample(x, mod, gpre, gpost, w, conv_in, h0, bt):
    b, lt, d = x.shape
    m = bt * lt
    xspec = pl.BlockSpec((bt, lt, d), lambda i: (i, 0, 0))
    conv_spec = pl.BlockSpec((bt, HIST, SSM_CONV_DIM), lambda i: (i, 0, 0))
    state_spec = pl.BlockSpec((bt,) + _STATE_SHAPE, lambda i: (i, 0, 0, 0))
    mods = [pl.BlockSpec((bt, None, 1, d), functools.partial(lambda k, i: (i, k, 0, 0), 3 + k)) for k in range(3)]
    weights = [w[n] for n in _SSM_WEIGHT_NAMES]
    return pl.pallas_call(
        _ssm_sample_kernel,
        out_shape=(jax.ShapeDtypeStruct(x.shape, F32),
                   jax.ShapeDtypeStruct((b, HIST, SSM_CONV_DIM), F32),
                   jax.ShapeDtypeStruct((b,) + _STATE_SHAPE, F32)),
        grid=(b // bt,),
        in_specs=[xspec] + mods + [_const_spec((1, 1, d)), _const_spec((1, 1, d))]
        + [_const_spec(a.shape, a.dtype == BF16) for a in weights] + [conv_spec, state_spec],
        out_specs=(xspec, conv_spec, state_spec),
        scratch_shapes=[
            pltpu.VMEM((bt, CONV_PAD + lt, SSM_CONV_DIM), F32),
            pltpu.VMEM((m, SSM_CONV_DIM), F32),
            pltpu.VMEM((m, LANES), F32),
            pltpu.VMEM((m, LANES), F32),
            pltpu.VMEM((m, SSM_D_INNER), F32),
        ],
        compiler_params=_params(1, SSM_SAMPLE_VMEM_LIMIT_BYTES),
        name="ssm_sublayer_state",
    )(x, mod, mod, mod, gpre, gpost, *weights, conv_in, h0)


def _ssm_prompt_kernel(xa_ref, xc_ref, sh_ref, sc_ref, gt_ref, gpre_ref, gpost_ref, wz_ref, wxbc_ref, wdt_ref,
                       cw_ref, cb_ref, dtb_ref, alog_ref, dx_ref, nw_ref, wo_ref, e2_ref,
                       o_ref, conv_out_ref, h_out_ref,
                       xpad, z_s, xbc_s, dt_s, da_s, zc_s, xbcc_s, dtc_s, dac_s, y_s, yn_s, ynp_s, h_s,
                       *, steps_per_seq, n_blocks):
    s = pl.program_id(0)
    q = SSM_CHUNK
    lt = 2 * q
    seq_start = (s % steps_per_seq) == 0
    scan_start = ((s + steps_per_seq - 1) % steps_per_seq) == 0
    n_slabs = SSM_CONV_DIM // LANES

    @pl.when(s == 0)
    def _():
        for ref in (z_s, xbc_s, dt_s, da_s, yn_s, h_s):
            ref[...] = jnp.zeros(ref.shape, ref.dtype)

    @pl.when(seq_start)
    def _():
        xpad[:, CONV_PAD - HIST:CONV_PAD, :] = jnp.zeros((n_slabs, HIST, LANES), F32)

    ynp_s[...] = yn_s[...]
    zc_s[...] = z_s[...]
    xbcc_s[...] = xbc_s[...]
    dtc_s[...] = dt_s[...]
    dac_s[...] = da_s[...]
    consts = _chunk_consts(q, q)

    def scan(t, reset):
        rows = slice(t * q, (t + 1) * q)
        xbc_v = xbcc_s.at[:, rows, :]
        _ssd_chunk(0, q, q, 0, xbc_s=xbc_v, dt_s=dtc_s.at[rows], da_s=dac_s.at[rows], y_s=y_s, e2_ref=e2_ref,
                   h_in=h_s, h_out=h_s, consts=consts, reset=reset)
        xs = _cols(xbc_v, slice(None), 0, SSM_D_INNER)
        yn_s[rows, :] = _gate_norm(y_s[...], xs, zc_s[rows, :], dx_ref[...], nw_ref[...])

    out = _dot(ynp_s[...], wo_ref[...])
    o_ref[0] = xc_ref[0] + gt_ref[0] * _rms(out, gpost_ref[0])

    hb = _mod_norm(xa_ref[0], gpre_ref[0], sc_ref[0], sh_ref[0]).astype(BF16)
    z, xbc_raw, dt, da = _ssm_project(hb, wz_ref, wxbc_ref, wdt_ref, dtb_ref, alog_ref)
    z_s[...] = z
    dt_s[...] = dt
    da_s[...] = da
    for c in range(n_slabs):
        xpad[c, CONV_PAD:CONV_PAD + lt, :] = xbc_raw[:, c * LANES:(c + 1) * LANES]

    scan(0, scan_start)

    half = lt // 2
    for c in range(n_slabs):
        csl = slice(c * LANES, (c + 1) * LANES)
        for par in range(2):
            acc = cb_ref[:, csl]
            for k in range(SSM_CONV):
                rows = pl.ds(CONV_PAD - HIST + par + k, half, stride=2)
                acc = acc + xpad[c, rows, :] * cw_ref[k:k + 1, csl]
            xbc_s[c, pl.ds(par, half, stride=2), :] = _silu(acc)
        new_hist = xpad[c, CONV_PAD + lt - HIST:CONV_PAD + lt, :]
        conv_out_ref[0, :, csl] = new_hist
        xpad[c, CONV_PAD - HIST:CONV_PAD, :] = new_hist

    scan(1, None)

    @pl.when(s <= n_blocks)
    def _():
        h_out_ref[...] = h_s[...]


def _ssm_prompt(x, mod, gpre, gpost, w):
    b, l, d = x.shape
    q = SSM_CHUNK
    lt = 2 * q
    assert l % lt == 0
    spq = l // lt
    n_blocks = b * spq

    def lagged(lag):
        def index(s):
            blk = jnp.clip(s - lag, 0, n_blocks - 1)
            return blk // spq, blk % spq
        return index

    cur, prev, prev2 = lagged(0), lagged(1), lagged(2)
    xa_spec = pl.BlockSpec((1, lt, d), lambda s: cur(s) + (0,))
    xc_spec = pl.BlockSpec((1, lt, d), lambda s: prev2(s) + (0,))
    mod_a = [pl.BlockSpec((1, None, 1, d), functools.partial(lambda k, s: (cur(s)[0], k, 0, 0), 3 + k))
             for k in range(2)]
    mod_c = pl.BlockSpec((1, None, 1, d), lambda s: (prev2(s)[0], 5, 0, 0))
    conv_spec = pl.BlockSpec((1, HIST, SSM_CONV_DIM), lambda s: (cur(s)[0], 0, 0))
    state_spec = pl.BlockSpec((1,) + _STATE_SHAPE, lambda s: (prev(s)[0], 0, 0, 0))
    weights = [w[n] for n in _SSM_WEIGHT_NAMES]
    return pl.pallas_call(
        functools.partial(_ssm_prompt_kernel, steps_per_seq=spq, n_blocks=n_blocks),
        out_shape=(jax.ShapeDtypeStruct(x.shape, F32),
                   jax.ShapeDtypeStruct((b, HIST, SSM_CONV_DIM), F32),
                   jax.ShapeDtypeStruct((b,) + _STATE_SHAPE, F32)),
        grid=(n_blocks + 2,),
        in_specs=[xa_spec, xc_spec] + mod_a + [mod_c, _const_spec((1, 1, d)), _const_spec((1, 1, d))]
        + [_const_spec(a.shape, a.dtype == BF16) for a in weights],
        out_specs=(xc_spec, conv_spec, state_spec),
        scratch_shapes=[
            pltpu.VMEM((SSM_CONV_DIM // LANES, CONV_PAD + lt, LANES), F32),
            pltpu.VMEM((lt, SSM_D_INNER), F32),
            pltpu.VMEM((SSM_CONV_DIM // LANES, lt, LANES), F32),
            pltpu.VMEM((lt, LANES), F32),
            pltpu.VMEM((lt, LANES), F32),
            pltpu.VMEM((lt, SSM_D_INNER), F32),
            pltpu.VMEM((SSM_CONV_DIM // LANES, lt, LANES), F32),
            pltpu.VMEM((lt, LANES), F32),
            pltpu.VMEM((lt, LANES), F32),
            pltpu.VMEM((q, SSM_D_INNER), F32),
            pltpu.VMEM((lt, SSM_D_INNER), BF16),
            pltpu.VMEM((lt, SSM_D_INNER), BF16),
            pltpu.VMEM((1,) + _STATE_SHAPE, F32),
        ],
        compiler_params=_params(1),
        name="ssm_sublayer",
    )(x, x, mod, mod, mod, gpre, gpost, *weights)


def _ssm_weights(in_w, conv_w, conv_b, dt_bias, a_log, d_skip, norm_w, out_w):
    pad = LANES - SSM_HEADS
    heads = np.arange(SSM_D_INNER) // SSM_HEAD_DIM
    e = (np.arange(LANES)[:, None] == heads[None, :]).astype(np.float32)
    return dict(
        wz=in_w[:, :SSM_D_INNER].astype(BF16),
        wxbc=in_w[:, SSM_D_INNER:SSM_D_INNER + SSM_CONV_DIM].astype(BF16),
        wdt=jnp.pad(in_w[:, SSM_D_INNER + SSM_CONV_DIM:], ((0, 0), (0, pad))).astype(BF16),
        conv_w=conv_w,
        conv_b=conv_b.reshape(1, SSM_CONV_DIM),
        dt_bias=jnp.pad(dt_bias, (0, pad)).reshape(1, LANES),
        a_log=jnp.pad(a_log, (0, pad)).reshape(1, LANES),
        d_x=jnp.repeat(d_skip, SSM_HEAD_DIM).reshape(1, SSM_D_INNER),
        norm_w=norm_w.reshape(1, SSM_D_INNER),
        wo=out_w.astype(BF16),
        e2=jnp.asarray(np.concatenate([e, e], axis=0), BF16),
    )


SOFTMAX_ROWS = 64


def _softmax_sink(logits, sink):
    mx = jnp.maximum(jnp.max(logits, axis=-1, keepdims=True), sink)
    e = jnp.exp(logits - mx)
    denom = jnp.sum(e, axis=-1, keepdims=True) + jnp.exp(sink - mx)
    return e * (1.0 / denom)


def _dedup(t):
    lane_lo = lax.broadcasted_iota(jnp.int32, (t.shape[0], LANES), 1) < HALF
    tiles = [jnp.where(lane_lo, t[:, (2 * i) * LANES:(2 * i + 1) * LANES],
                       t[:, (2 * i + 1) * LANES:(2 * i + 2) * LANES]) for i in range(ATTN_KV_HEADS // 2)]
    return jnp.concatenate(tiles, axis=1)


def _swa_prompt_kernel(sinks_ref, xa_ref, xc_ref, sh_ref, sc_ref, gt_ref, gpre_ref, gpost_ref, wq_ref, wk_ref,
                       wv_ref, bq_ref, bk_ref, bv_ref, bias_ref, wo_ref, bo_ref,
                       o_ref, kc_ref, vc_ref, qn_s, kn_s, vn_s, q_s, kbuf, vbuf, att_s, lg_s, p_s,
                       *, steps_per_seq):
    s = pl.program_id(0)
    tq = xa_ref.shape[1]
    n_tiles = 2 * ATTN_KV_HEADS
    pairs = ATTN_REP // 2

    @pl.when(s == 0)
    def _():
        for ref in (qn_s, kn_s, vn_s, kbuf, vbuf):
            ref[...] = jnp.zeros(ref.shape, ref.dtype)

    kbuf[0:WINDOW, :] = kbuf[tq:tq + WINDOW, :]
    vbuf[0:WINDOW, :] = vbuf[tq:tq + WINDOW, :]
    kbuf[WINDOW:WINDOW + tq, :] = kn_s[...]
    vbuf[WINDOW:WINDOW + tq, :] = vn_s[...]
    q_s[...] = qn_s[...]

    hb = _mod_norm(xa_ref[0], gpre_ref[0], sc_ref[0], sh_ref[0]).astype(BF16)
    half_q = ATTN_Q_DIM // 2

    def project_q(lo):
        qn_s[:, lo:lo + half_q] = ((_dot(hb, wq_ref[:, lo:lo + half_q]) + bq_ref[:, lo:lo + half_q])
                                   * (ATTN_HEAD_DIM ** -0.5)).astype(BF16)

    def project_kv(w_ref, b_ref, nxt, cache_ref):
        kv = _dot(hb, w_ref[...]) + b_ref[...]
        nxt[...] = kv.astype(BF16)
        cache_ref[0] = _dedup(kv[tq - WINDOW:, :])

    parts = [functools.partial(project_q, 0), functools.partial(project_q, half_q),
             functools.partial(project_kv, wk_ref, bk_ref, kn_s, kc_ref),
             functools.partial(project_kv, wv_ref, bv_ref, vn_s, vc_ref)]

    lane_lo = lax.broadcasted_iota(jnp.int32, (2 * WINDOW, LANES), 1) < HALF
    zero_b = jnp.zeros((), BF16)
    seq_first = jnp.where(((s + steps_per_seq - 1) % steps_per_seq) == 0, n_tiles, 0)
    n_q = tq // WINDOW
    for bi in range(n_q):
        r0 = bi * WINDOW
        qrows = slice(r0, r0 + WINDOW)
        first = seq_first if bi == 0 else 0
        lg, pb = lg_s.at[bi], p_s.at[bi]
        for g in range(ATTN_KV_HEADS):
            kd = kbuf[r0:r0 + 2 * WINDOW, g * LANES:(g + 1) * LANES]
            q2 = jnp.concatenate([q_s[qrows, (g * pairs + pr) * LANES:(g * pairs + pr + 1) * LANES]
                                  for pr in range(pairs)], axis=0)
            for half in range(2):
                kh = jnp.where(lane_lo, kd, zero_b) if half == 0 else jnp.where(lane_lo, zero_b, kd)
                t = 2 * g + half
                lg[t] = _dot_nt(q2, kh) + bias_ref[first + t]
        for part in parts[bi * len(parts) // n_q:(bi + 1) * len(parts) // n_q]:
            part()
        for t in range(n_tiles):
            for rc in range(2 * WINDOW // SOFTMAX_ROWS):
                rs = slice(rc * SOFTMAX_ROWS, (rc + 1) * SOFTMAX_ROWS)
                h = 4 * (t // 2) + (t % 2) + 2 * ((rc * SOFTMAX_ROWS) // WINDOW)
                pb[t, rs, :] = _softmax_sink(lg[t, rs, :], sinks_ref[h]).astype(BF16)
        for g in range(ATTN_KV_HEADS):
            vd = vbuf[r0:r0 + 2 * WINDOW, g * LANES:(g + 1) * LANES]
            acc = (_dot(pb[2 * g], jnp.where(lane_lo, vd, zero_b))
                   + _dot(pb[2 * g + 1], jnp.where(lane_lo, zero_b, vd)))
            for pr in range(pairs):
                pair = g * pairs + pr
                att_s[qrows, pair * LANES:(pair + 1) * LANES] = acc[pr * WINDOW:(pr + 1) * WINDOW].astype(BF16)
        out = _dot(att_s[qrows, :], wo_ref[...]) + bo_ref[...]
        o_ref[0, qrows, :] = xc_ref[0, qrows, :] + gt_ref[0] * _rms(out, gpost_ref[0])


def _swa_prompt(x, mod, gpre, gpost, w, bias, tq):
    b, l, d = x.shape
    kvw = ATTN_KV_HEADS * LANES
    pairs = ATTN_REP // 2
    n_tiles = 2 * ATTN_KV_HEADS
    assert l % tq == 0 and tq % WINDOW == 0
    spq = l // tq
    n_blocks = b * spq
    bias = bias.reshape(2, ATTN_KV_HEADS, pairs, 2, WINDOW, 2 * WINDOW).transpose(0, 1, 3, 2, 4, 5)
    bias = bias.reshape(2 * n_tiles, pairs * WINDOW, 2 * WINDOW)

    def cur(s):
        blk = jnp.minimum(s, n_blocks - 1)
        return blk // spq, blk % spq

    def prev(s):
        blk = jnp.maximum(s - 1, 0)
        return blk // spq, blk % spq

    xa_spec = pl.BlockSpec((1, tq, d), lambda s: cur(s) + (0,))
    xc_spec = pl.BlockSpec((1, tq, d), lambda s: prev(s) + (0,))
    mod_a = [pl.BlockSpec((1, None, 1, d), functools.partial(lambda k, s: (cur(s)[0], k, 0, 0), 3 + k))
             for k in range(2)]
    mod_c = pl.BlockSpec((1, None, 1, d), lambda s: (prev(s)[0], 5, 0, 0))
    cache_spec = pl.BlockSpec((1, WINDOW, ATTN_KV_DIM), lambda s: (cur(s)[0], 0, 0))
    consts = [w["wq"], w["wk_dup"], w["wv_dup"], w["bq"], w["bk_dup"], w["bv_dup"], bias, w["wo"], w["bo"]]
    n_q = tq // WINDOW
    return pl.pallas_call(
        functools.partial(_swa_prompt_kernel, steps_per_seq=spq),
        out_shape=(jax.ShapeDtypeStruct(x.shape, F32),
                   jax.ShapeDtypeStruct((b, WINDOW, ATTN_KV_DIM), F32),
                   jax.ShapeDtypeStruct((b, WINDOW, ATTN_KV_DIM), F32)),
        grid=(n_blocks + 1,),
        in_specs=[pl.BlockSpec(memory_space=pltpu.SMEM), xa_spec, xc_spec] + mod_a
        + [mod_c, _const_spec((1, 1, d)), _const_spec((1, 1, d))]
        + [_const_spec(a.shape, a.dtype == BF16) for a in consts],
        out_specs=(xc_spec, cache_spec, cache_spec),
        scratch_shapes=[
            pltpu.VMEM((tq, ATTN_Q_DIM), BF16),
            pltpu.VMEM((tq, kvw), BF16),
            pltpu.VMEM((tq, kvw), BF16),
            pltpu.VMEM((tq, ATTN_Q_DIM), BF16),
            pltpu.VMEM((WINDOW + tq, kvw), BF16),
            pltpu.VMEM((WINDOW + tq, kvw), BF16),
            pltpu.VMEM((tq, ATTN_Q_DIM), BF16),
            pltpu.VMEM((n_q, n_tiles, pairs * WINDOW, 2 * WINDOW), F32),
            pltpu.VMEM((n_q, n_tiles, pairs * WINDOW, 2 * WINDOW), BF16),
        ],
        compiler_params=_params(1),
        name="swa_prompt_sublayer",
    )(w["sinks"], x, x, mod, mod, mod, gpre, gpost, *consts)


def _swa_sample_kernel(x_ref, sh_ref, sc_ref, gt_ref, gpre_ref, gpost_ref, wq_ref, wk_ref, wv_ref,
                       bq_ref, bk_ref, bv_ref, ck_ref, cv_ref, bias_ref, sink_ref, wo_ref, bo_ref,
                       o_ref, ko_ref, vo_ref, q_s, kn_s, vn_s, att_s):
    x = x_ref[...]
    bt, lt, d = x.shape
    m = bt * lt
    nbuf = ck_ref.shape[1]
    wide = ATTN_KV_HEADS * ATTN_HEAD_DIM

    hb = _mod_norm(x, gpre_ref[...], sc_ref[...], sh_ref[...]).reshape(m, d).astype(BF16)
    q_s[...] = (_dot(hb, wq_ref[...]) + bq_ref[...]) * (ATTN_HEAD_DIM ** -0.5)
    kn_s[...] = _dot(hb, wk_ref[...]) + bk_ref[...]
    vn_s[...] = _dot(hb, wv_ref[...]) + bv_ref[...]
    zpad = jnp.zeros((bias_ref.shape[1] - nbuf - lt, ATTN_KV_DIM), F32)

    def seq(bi, carry):
        rows = pl.ds(pl.multiple_of(bi * lt, lt), lt)
        kn = kn_s[rows, :]
        vn = vn_s[rows, :]
        ck = ck_ref[bi]
        cv = cv_ref[bi]
        kfull = jnp.concatenate([ck, kn, zpad], axis=0).astype(BF16)
        vfull = jnp.concatenate([cv, vn, zpad], axis=0).astype(BF16)
        ko_ref[bi, 0:nbuf - lt, :] = ck[lt:, :]
        vo_ref[bi, 0:nbuf - lt, :] = cv[lt:, :]
        ko_ref[bi, nbuf - lt:nbuf, :] = kn
        vo_ref[bi, nbuf - lt:nbuf, :] = vn
        qbig = jnp.concatenate([q_s[rows, h * wide:(h + 1) * wide] for h in range(ATTN_HEADS)], axis=0)
        logits = _dot_nt(qbig.astype(BF16), kfull) + bias_ref[...]
        p = _softmax_sink(logits, sink_ref[:, 0:1]).astype(BF16)
        res = _dot(p, vfull)
        att_s[rows, :] = jnp.concatenate([res[h * lt:(h + 1) * lt, :] for h in range(ATTN_HEADS)], axis=1)
        return carry

    lax.fori_loop(0, bt, seq, 0, unroll=8)

    out = (_dot(att_s[...].astype(BF16), wo_ref[...]) + bo_ref[...]).reshape(bt, lt, d)
    o_ref[...] = x + gt_ref[...] * _rms(out, gpost_ref[...])


def _swa_sample(x, mod, gpre, gpost, w, bias, cache_k, cache_v, bt):
    b, lt, d = x.shape
    nbuf = cache_k.shape[1]
    assert nbuf == WINDOW and lt <= WINDOW and b % bt == 0
    m = bt * lt
    wide = ATTN_KV_HEADS * ATTN_HEAD_DIM
    big = ATTN_HEADS * wide
    keys = 2 * WINDOW
    xspec = pl.BlockSpec((bt, lt, d), lambda i: (i, 0, 0))
    cache_spec = pl.BlockSpec((bt, nbuf, ATTN_KV_DIM), lambda i: (i, 0, 0))
    bias_s = bias[0, :, :lt, :].reshape(ATTN_HEADS * lt, keys)
    sink_col = jnp.broadcast_to(jnp.repeat(w["sinks"], lt)[:, None], (ATTN_HEADS * lt, LANES))
    mods = [pl.BlockSpec((bt, None, 1, D_MODEL), functools.partial(lambda k, i: (i, k, 0, 0), 3 + k))
            for k in range(3)]
    pre = [w["wq_big"], w["wk"], w["wv"], w["bq_big"], w["bk"], w["bv"]]
    post = [bias_s, sink_col, w["wo_big"], w["bo"]]
    return pl.pallas_call(
        _swa_sample_kernel,
        out_shape=(jax.ShapeDtypeStruct(x.shape, F32),
                   jax.ShapeDtypeStruct(cache_k.shape, F32),
                   jax.ShapeDtypeStruct(cache_v.shape, F32)),
        grid=(b // bt,),
        in_specs=[xspec] + mods + [_const_spec((1, 1, d)), _const_spec((1, 1, d))]
        + [_const_spec(a.shape, a.dtype == BF16) for a in pre] + [cache_spec, cache_spec]
        + [_const_spec(a.shape, a.dtype == BF16) for a in post],
        out_specs=(xspec, cache_spec, cache_spec),
        scratch_shapes=[
            pltpu.VMEM((m, big), F32),
            pltpu.VMEM((m, ATTN_KV_DIM), F32),
            pltpu.VMEM((m, ATTN_KV_DIM), F32),
            pltpu.VMEM((m, big), F32),
        ],
        compiler_params=_params(1),
        name="swa_sample_sublayer",
    )(x, mod, mod, mod, gpre, gpost, *pre, cache_k, cache_v, *post)


def _swa_weights(qkv_w, qkv_b, sinks, o_w, o_b):
    d = qkv_w.shape[0]
    wq = qkv_w[:, :ATTN_Q_DIM]
    wk = qkv_w[:, ATTN_Q_DIM:ATTN_Q_DIM + ATTN_KV_DIM]
    wv = qkv_w[:, ATTN_Q_DIM + ATTN_KV_DIM:]
    bq = qkv_b[:ATTN_Q_DIM]
    bk = qkv_b[ATTN_Q_DIM:ATTN_Q_DIM + ATTN_KV_DIM]
    bv = qkv_b[ATTN_Q_DIM + ATTN_KV_DIM:]

    def dup(a):
        a4 = a.reshape(a.shape[:-1] + (ATTN_KV_HEADS, 1, ATTN_HEAD_DIM))
        a4 = jnp.broadcast_to(a4, a.shape[:-1] + (ATTN_KV_HEADS, 2, ATTN_HEAD_DIM))
        return a4.reshape(a.shape[:-1] + (ATTN_KV_HEADS * LANES,))

    own = jnp.asarray((np.arange(ATTN_HEADS)[:, None] // ATTN_REP == np.arange(ATTN_KV_HEADS)[None, :])
                      .astype(np.float32))
    wq_big = (wq.reshape(d, ATTN_HEADS, 1, ATTN_HEAD_DIM) * own[None, :, :, None]).reshape(d, -1)
    bq_big = (bq.reshape(ATTN_HEADS, 1, ATTN_HEAD_DIM) * own[:, :, None]).reshape(1, -1)
    wo_big = (o_w.reshape(ATTN_HEADS, 1, ATTN_HEAD_DIM, d) * own[:, :, None, None]).reshape(-1, d)
    return dict(
        sinks=sinks,
        wq=wq.astype(BF16), wk_dup=dup(wk).astype(BF16), wv_dup=dup(wv).astype(BF16),
        bq=bq.reshape(1, -1), bk_dup=dup(bk).reshape(1, -1), bv_dup=dup(bv).reshape(1, -1),
        wo=o_w.astype(BF16), bo=o_b.reshape(1, d),
        wq_big=wq_big.astype(BF16), bq_big=bq_big, wk=wk.astype(BF16), wv=wv.astype(BF16),
        bk=bk.reshape(1, -1), bv=bv.reshape(1, -1), wo_big=wo_big.astype(BF16),
    )


def _trunk(x, mod_all, state_ssm, state_conv, cache_k, cache_v, norm_pre, norm_post, ffn_in, ffn_out,
           ssm_w, swa_w, bias, tiles):
    bt, lt, n_sub = tiles["ffn"]
    b, l, d = x.shape
    sample = state_ssm is not None
    outs = {}
    for i in range(2):
        mod = mod_all[i].reshape(b, N_SUB * 3, 1, d)
        gpre = norm_pre[i].reshape(N_SUB, 1, 1, d)
        gpost = norm_post[i].reshape(N_SUB, 1, 1, d)
        x = _ffn(x, mod, 0, gpre[0], gpost[0], ffn_in, ffn_out, i, 0, bt, lt, n_sub)
        if i == 0:
            if sample:
                x, conv_new, ssm_new = _ssm_sample(x, mod, gpre[1], gpost[1], ssm_w, state_conv[0], state_ssm[0],
                                                   tiles["ssm"])
            else:
                x, conv_new, ssm_new = _ssm_prompt(x, mod, gpre[1], gpost[1], ssm_w)
            outs["conv"] = conv_new[None]
            outs["ssm"] = ssm_new[None]
        else:
            if sample:
                x, k_new, v_new = _swa_sample(x, mod, gpre[1], gpost[1], swa_w, bias,
                                              cache_k[0].reshape(b, -1, ATTN_KV_DIM),
                                              cache_v[0].reshape(b, -1, ATTN_KV_DIM), tiles["swa"])
            else:
                x, k_new, v_new = _swa_prompt(x, mod, gpre[1], gpost[1], swa_w, bias, tiles["swa"])
            shape = (1, b, -1, ATTN_KV_HEADS, ATTN_HEAD_DIM)
            outs["k"] = k_new.reshape(shape)
            outs["v"] = v_new.reshape(shape)
        x = _ffn(x, mod, 2, gpre[2], gpost[2], ffn_in, ffn_out, i, 1, bt, lt, n_sub)
    return x, outs["ssm"], outs["conv"], outs["k"], outs["v"]


def kernel(x_prompt, x_sample, state_ssm, state_conv, cache_k, cache_v, c_prompt, c_sample, ada_w, ada_b, norm_pre, norm_post, ffn_w_in, ffn_w_out, ssm_in_w, ssm_conv_w, ssm_conv_b, ssm_dt_bias, ssm_a_log, ssm_d, ssm_norm_w, ssm_out_w, attn_qkv_w, attn_qkv_b, attn_sinks, attn_o_w, attn_o_b, rel_bias):
    nb = x_prompt.shape[0]
    mod_all = _ada(jnp.concatenate([c_prompt, c_sample], axis=0), ada_w, ada_b)
    bias = _bias_table(rel_bias)
    ffn_in = ffn_w_in.astype(BF16)
    ffn_out = ffn_w_out.astype(BF16)
    ssm_w = _ssm_weights(ssm_in_w[0], ssm_conv_w[0], ssm_conv_b[0], ssm_dt_bias[0], ssm_a_log[0], ssm_d[0],
                         ssm_norm_w[0], ssm_out_w[0])
    swa_w = _swa_weights(attn_qkv_w[0], attn_qkv_b[0], attn_sinks[0], attn_o_w[0], attn_o_b[0])
    common = (norm_pre, norm_post, ffn_in, ffn_out, ssm_w, swa_w, bias)

    p_tiles = dict(ffn=(1, FFN_TILE_ROWS, FFN_SUB_TILES), swa=SWA_TILE_ROWS)
    y_p, ssm_p, conv_p, k_p, v_p = _trunk(x_prompt, mod_all[:, :nb], None, None, None, None, *common, p_tiles)
    ns, ls = x_sample.shape[:2]
    s_tiles = dict(ffn=(min(SAMPLE_FFN_SEQS, ns), ls, 1), ssm=min(SAMPLE_SSM_SEQS, ns),
                   swa=min(SAMPLE_SWA_SEQS, ns))
    y_s, ssm_s, conv_s, k_s, v_s = _trunk(x_sample, mod_all[:, nb:], state_ssm, state_conv, cache_k, cache_v,
                                          *common, s_tiles)
    return (y_p, y_s, ssm_p, conv_p, k_p, v_p, ssm_s, conv_s, k_s, v_s)
```

```python
import functools
import math

import numpy as np
import jax
import jax.numpy as jnp
from jax import lax
from jax.experimental import pallas as pl
from jax.experimental.pallas import tpu as pltpu

F32 = jnp.float32
BF16 = jnp.bfloat16

D_MODEL = 1024
N_SUB = 3
RMS_EPS = 1e-6
FFN_RES = 0.5
D_FF = 2816

SSM_D_INNER = 2048
SSM_HEAD_DIM = 64
SSM_HEADS = 32
SSM_GROUPS = 4
SSM_HPG = 8
SSM_STATE = 128
SSM_CONV = 4
SSM_CHUNK = 128
SSM_GN = SSM_GROUPS * SSM_STATE
SSM_CONV_DIM = SSM_D_INNER + 2 * SSM_GN
SSM_GROUP_WIDTH = SSM_HPG * SSM_HEAD_DIM

ATTN_HEAD_DIM = 64
ATTN_HEADS = 16
ATTN_KV_HEADS = 4
ATTN_REP = 4
WINDOW = 128
REL_BUCKETS = 32
ATTN_Q_DIM = ATTN_HEADS * ATTN_HEAD_DIM
ATTN_KV_DIM = ATTN_KV_HEADS * ATTN_HEAD_DIM

LANES = 128
SUBLANES = 8
HALF = LANES // 2
VMEM_LIMIT_BYTES = 56 * 1024 * 1024
LARGE_VMEM_LIMIT_BYTES = 62 * 1024 * 1024
CONV_PAD = SUBLANES

ADA_TILE_COLS = 9 * LANES
FFN_TILE_ROWS = 2048
FFN_SUB_TILES = 8
SWA_TILE_ROWS = 512
SAMPLE_FFN_SEQS = 64
SAMPLE_SSM_SEQS = 8
SAMPLE_SWA_SEQS = 16


def _dot(a, b):
    return jnp.dot(a, b, preferred_element_type=F32)


def _dot_nt(a, b):
    return lax.dot_general(a, b, (((1,), (1,)), ((), ())), preferred_element_type=F32)


def _dot_tn(a, b):
    return lax.dot_general(a, b, (((0,), (0,)), ((), ())), preferred_element_type=F32)


def _silu(x):
    h = 0.5 * x
    return h * jnp.tanh(h) + h


def _rms(x, g):
    return x * lax.rsqrt(jnp.mean(x * x, axis=-1, keepdims=True) + RMS_EPS) * g


def _mod_norm(x, g, scale, shift):
    return _rms(x, g) * (1.0 + scale) + shift


def _split3(x):
    hi = x.astype(BF16)
    r1 = x - hi.astype(F32)
    mid = r1.astype(BF16)
    lo = (r1 - mid.astype(F32)).astype(BF16)
    return hi, mid, lo


def _const_spec(shape, single_buffer=False):
    nd = len(shape)
    kw = {"pipeline_mode": pl.Buffered(1)} if single_buffer else {}
    return pl.BlockSpec(shape, lambda *_: (0,) * nd, **kw)


def _params(n_grid, vmem_limit=VMEM_LIMIT_BYTES):
    return pltpu.CompilerParams(
        dimension_semantics=("arbitrary",) * n_grid,
        vmem_limit_bytes=vmem_limit,
    )


def _ada_kernel(c_ref, w_ref, b_ref, o_ref):
    cs = _silu(c_ref[...]).astype(BF16)
    o_ref[0] = _dot(cs, w_ref[0].astype(BF16)) + b_ref[0]


def _ada(c_all, ada_w, ada_b, tn=ADA_TILE_COLS):
    depth, d, n = ada_w.shape
    bc = c_all.shape[0]
    return pl.pallas_call(
        _ada_kernel,
        out_shape=jax.ShapeDtypeStruct((depth, bc, n), F32),
        grid=(depth, n // tn),
        in_specs=[
            pl.BlockSpec((bc, d), lambda l, j: (0, 0)),
            pl.BlockSpec((1, d, tn), lambda l, j: (l, 0, j)),
            pl.BlockSpec((1, 1, tn), lambda l, j: (l, 0, j)),
        ],
        out_specs=pl.BlockSpec((1, bc, tn), lambda l, j: (l, 0, j)),
        compiler_params=_params(2),
        name="ada_mod",
    )(c_all, ada_w, ada_b.reshape(depth, 1, n))


def _t5_bucket_table():
    i = np.arange(WINDOW)[:, None]
    j = np.arange(2 * WINDOW)[None, :]
    dist = i + WINDOW - j
    exact = REL_BUCKETS // 2
    df = np.maximum(dist, 1).astype(np.float32)
    large = exact + (np.log(df / np.float32(exact)) / np.float32(math.log(WINDOW / exact))
                     * np.float32(REL_BUCKETS - exact)).astype(np.int32)
    large = np.minimum(large, REL_BUCKETS - 1)
    bucket = np.where(dist < exact, dist, large)
    valid = (dist >= 0) & (dist <= WINDOW)
    return np.where(valid, bucket, -1).astype(np.int32)


def _bias_kernel(rb_ref, idx_ref, o_ref):
    h = pl.program_id(0)
    idx = idx_ref[...]
    acc = jnp.full(idx.shape, -jnp.inf, F32)
    for b in range(REL_BUCKETS):
        acc = jnp.where(idx == b, rb_ref[b, h], acc)
    o_ref[0, 0] = acc
    col = lax.broadcasted_iota(jnp.int32, idx.shape, 1)
    o_ref[1, 0] = jnp.where(col < WINDOW, -jnp.inf, acc)


def _bias_table(rel_bias):
    idx = jnp.asarray(_t5_bucket_table())
    return pl.pallas_call(
        _bias_kernel,
        out_shape=jax.ShapeDtypeStruct((2, ATTN_HEADS, WINDOW, 2 * WINDOW), F32),
        grid=(ATTN_HEADS,),
        in_specs=[
            pl.BlockSpec(memory_space=pltpu.SMEM),
            pl.BlockSpec((WINDOW, 2 * WINDOW), lambda h: (0, 0)),
        ],
        out_specs=pl.BlockSpec((2, 1, WINDOW, 2 * WINDOW), lambda h: (0, h, 0, 0)),
        compiler_params=_params(1),
        name="rel_bias_table",
    )(rel_bias, idx)


def _ffn_kernel(x_ref, sh_ref, sc_ref, gt_ref, gpre_ref, gpost_ref, win_ref, wout_ref, o_ref, *, n_sub):
    bt, lt, d = x_ref.shape

    def sub_slices(s):
        if bt == 1:
            return slice(None), slice(s * (lt // n_sub), (s + 1) * (lt // n_sub))
        return slice(s * (bt // n_sub), (s + 1) * (bt // n_sub)), slice(None)

    def pre(s):
        bs, ls = sub_slices(s)
        x = x_ref[bs, ls, :]
        h = _mod_norm(x, gpre_ref[...], sc_ref[bs], sh_ref[bs])
        return h.reshape(x.shape[0] * x.shape[1], d).astype(BF16)

    def post(s, acc):
        bs, ls = sub_slices(s)
        x = x_ref[bs, ls, :]
        o_ref[bs, ls, :] = x + FFN_RES * gt_ref[bs] * _rms(acc.reshape(x.shape), gpost_ref[...])

    hb = pre(0)
    prev = None
    for s in range(n_sub):
        g = _dot(hb, win_ref[:, 0:D_FF])
        u = _dot(hb, win_ref[:, D_FF:2 * D_FF])
        acc = _dot((_silu(g) * u).astype(BF16), wout_ref[...])
        if prev is not None:
            post(s - 1, prev)
        if s + 1 < n_sub:
            hb = pre(s + 1)
        prev = acc
    post(n_sub - 1, prev)


def _mod_specs(bt, sub):
    return [pl.BlockSpec((bt, None, 1, D_MODEL), functools.partial(lambda k, i, j: (i, k, 0, 0), sub * 3 + k))
            for k in range(3)]


def _ffn(x, mod, sub, gpre, gpost, w_in, w_out, layer, which, bt, lt, n_sub):
    b, l, d = x.shape
    assert b % bt == 0 and l % lt == 0 and (lt if bt == 1 else bt) % n_sub == 0
    xspec = pl.BlockSpec((bt, lt, d), lambda i, j: (i, j, 0))
    wspec = [pl.BlockSpec((None, None) + w.shape[2:], lambda i, j: (layer, which, 0, 0),
                          pipeline_mode=pl.Buffered(1)) for w in (w_in, w_out)]
    return pl.pallas_call(
        functools.partial(_ffn_kernel, n_sub=n_sub),
        out_shape=jax.ShapeDtypeStruct(x.shape, F32),
        grid=(b // bt, l // lt),
        in_specs=[xspec] + _mod_specs(bt, sub) + [
            _const_spec((1, 1, d)), _const_spec((1, 1, d))] + wspec,
        out_specs=xspec,
        compiler_params=_params(2, LARGE_VMEM_LIMIT_BYTES),
        name="ffn_sublayer",
    )(x, mod, mod, mod, gpre, gpost, w_in, w_out)


def _chunk_consts(qc, seg):
    r = lax.broadcasted_iota(jnp.int32, (qc, qc), 0)
    c = lax.broadcasted_iota(jnp.int32, (qc, qc), 1)
    seg_shift = seg.bit_length() - 1
    same = jnp.right_shift(r, seg_shift) == jnp.right_shift(c, seg_shift)
    causal = same & (r >= c)
    tri = jnp.where(causal, 1.0, 0.0)
    upper = jnp.where(same & (c > r), 1.0, 0.0)
    tu = jnp.concatenate([tri, upper], axis=0).astype(BF16)
    er = lax.broadcasted_iota(jnp.int32, (LANES, LANES), 0)
    ec = lax.broadcasted_iota(jnp.int32, (LANES, LANES), 1)
    eye = jnp.where(er == ec, 1.0, 0.0).astype(BF16)
    lane_lo = lax.broadcasted_iota(jnp.int32, (qc, LANES), 1) < HALF
    return tu, eye, causal, lane_lo


def _ssm_project(hb, wz_ref, wxbc_ref, wdt_ref, dtb_ref, alog_ref):
    z = _dot(hb, wz_ref[...])
    xbc_raw = _dot(hb, wxbc_ref[...])
    dt_raw = _dot(hb, wdt_ref[...]) + dtb_ref[...]
    dt = jnp.maximum(dt_raw, 0.0) + jnp.log1p(jnp.exp(-jnp.abs(dt_raw)))
    return z, xbc_raw, dt, dt * (-jnp.exp(alog_ref[...]))


def _gate_norm(y, xs, z, dx, nw):
    yg = (y + xs * dx) * _silu(z)
    parts = []
    for g in range(SSM_GROUPS):
        v = yg[:, g * SSM_GROUP_WIDTH:(g + 1) * SSM_GROUP_WIDTH]
        parts.append(v * lax.rsqrt(jnp.mean(v * v, axis=-1, keepdims=True) + RMS_EPS))
    return (jnp.concatenate(parts, axis=1) * nw).astype(BF16)


def _cols(ref, rows, c0, c1):
    if len(ref.shape) == 2:
        return ref[rows, c0:c1]
    return jnp.concatenate([ref[c, rows, :] for c in range(c0 // LANES, c1 // LANES)], axis=1)


def _ssd_chunk(r0, qc, seg, s_first, *, xbc_s, dt_s, da_s, y_s, e2_ref, h_in, h_out, consts, reset=None):
    tu, eye, causal, lane_lo = consts
    rows = pl.ds(r0, qc)
    hi, mid, lo = _split3(da_s[rows, :])
    cs2 = _dot(tu, hi) + _dot(tu, mid) + _dot(tu, lo)
    a_cs = cs2[:qc]
    ea = jnp.exp(a_cs)
    dte = jnp.exp(cs2[qc:])

    stack = jnp.concatenate([dt_s[rows, :], ea, dte], axis=0)
    s_hi = stack.astype(BF16)
    s_lo = (stack - s_hi.astype(F32)).astype(BF16)
    sx = _dot(jnp.concatenate([s_hi, s_lo], axis=1), e2_ref[...])
    dt_x, ea_x, dte_x = sx[:qc], sx[qc:2 * qc], sx[2 * qc:]

    xdt = _cols(xbc_s, rows, 0, SSM_D_INNER) * dt_x
    xdt_b = xdt.astype(BF16)
    xd_b = (xdt * dte_x).astype(BF16)
    bm = _cols(xbc_s, rows, SSM_D_INNER, SSM_D_INNER + SSM_GN).astype(BF16)
    cm = _cols(xbc_s, rows, SSM_D_INNER + SSM_GN, SSM_CONV_DIM).astype(BF16)

    a_hi, a_mid, a_lo = _split3(a_cs)
    a_cs_t = _dot_nt(eye, a_hi) + _dot_nt(eye, a_mid) + _dot_nt(eye, a_lo)

    zero_b = jnp.zeros((), BF16)
    for g in range(SSM_GROUPS):
        gsl = slice(g * SSM_STATE, (g + 1) * SSM_STATE)
        cb = _dot_nt(cm[:, gsl], bm[:, gsl])
        for pr in range(SSM_HPG // 2):
            h0 = g * SSM_HPG + 2 * pr
            psl = slice((h0 // 2) * LANES, (h0 // 2 + 1) * LANES)
            xp = xdt_b[:, psl]
            acc = None
            for half in range(2):
                h = h0 + half
                seg_sum = a_cs[:, h:h + 1] - a_cs_t[h:h + 1, :]
                decay = jnp.exp(jnp.where(causal, seg_sum, -jnp.inf))
                w = (decay * cb).astype(BF16)
                xh = jnp.where(lane_lo if half == 0 else jnp.logical_not(lane_lo), xp, zero_b)
                o = _dot(w, xh)
                acc = o if acc is None else acc + o
            y_s[rows, psl] = acc

    for t in range(qc // seg):
        tr = slice(t * seg, (t + 1) * seg)
        trows = pl.ds(r0 + t * seg, seg)
        last = t * seg + seg - 1
        for g in range(SSM_GROUPS):
            gsl = slice(g * SSM_STATE, (g + 1) * SSM_STATE)
            csl = slice(g * SSM_GROUP_WIDTH, (g + 1) * SSM_GROUP_WIDTH)
            hsl = slice(g * SSM_HPG, (g + 1) * SSM_HPG)
            hg = h_in[s_first + t, hsl].reshape(SSM_GROUP_WIDTH, SSM_STATE)
            if reset is not None:
                hg = jnp.where(reset, 0.0, hg)
            y_off = _dot_nt(cm[tr, gsl], hg.astype(BF16)) * ea_x[tr, csl]
            y_s[trows, csl] = y_s[trows, csl] + y_off
            upd = _dot_tn(xd_b[tr, csl], bm[tr, gsl])
            cdec = jnp.concatenate(
                [jnp.broadcast_to(ea[last:last + 1, g * SSM_HPG + r:g * SSM_HPG + r + 1],
                                  (SSM_HEAD_DIM, SSM_STATE)) for r in range(SSM_HPG)], axis=0)
            h_out[s_first + t, hsl] = (hg * cdec + upd).reshape(SSM_HPG, SSM_HEAD_DIM, SSM_STATE)


HIST = SSM_CONV - 1
_SSM_WEIGHT_NAMES = ("wz", "wxbc", "wdt", "conv_w", "conv_b", "dt_bias", "a_log", "d_x", "norm_w", "wo", "e2")
_STATE_SHAPE = (SSM_HEADS, SSM_HEAD_DIM, SSM_STATE)


def _conv_silu(xpad, cw_ref, cb_ref, lt):
    conv = cb_ref[...]
    for k in range(SSM_CONV):
        conv = conv + xpad[:, CONV_PAD - HIST + k:CONV_PAD - HIST + k + lt, :] * cw_ref[k:k + 1, :]
    return _silu(conv)


def _ssm_sample_kernel(x_ref, sh_ref, sc_ref, gt_ref, gpre_ref, gpost_ref, wz_ref, wxbc_ref, wdt_ref, cw_ref,
                       cb_ref, dtb_ref, alog_ref, dx_ref, nw_ref, wo_ref, e2_ref, conv_in_ref, h0_ref,
                       o_ref, conv_out_ref, h_out_ref, xpad, xbc_s, dt_s, da_s, y_s):
    x = x_ref[...]
    bt, lt, d = x.shape
    m = bt * lt
    hb = _mod_norm(x, gpre_ref[...], sc_ref[...], sh_ref[...]).reshape(m, d).astype(BF16)
    z, xbc_raw, dt, da = _ssm_project(hb, wz_ref, wxbc_ref, wdt_ref, dtb_ref, alog_ref)
    dt_s[...] = dt
    da_s[...] = da
    xpad[:, CONV_PAD:CONV_PAD + lt, :] = xbc_raw.reshape(bt, lt, SSM_CONV_DIM)
    xpad[:, CONV_PAD - HIST:CONV_PAD, :] = conv_in_ref[...]
    xbc_s[...] = _conv_silu(xpad, cw_ref, cb_ref, lt).reshape(m, SSM_CONV_DIM)
    conv_out_ref[...] = xpad[:, CONV_PAD + lt - HIST:CONV_PAD + lt, :]
    _ssd_chunk(0, m, lt, 0, xbc_s=xbc_s, dt_s=dt_s, da_s=da_s, y_s=y_s, e2_ref=e2_ref,
               h_in=h0_ref, h_out=h_out_ref, consts=_chunk_consts(m, lt))
    yn = _gate_norm(y_s[...], xbc_s[:, 0:SSM_D_INNER], z, dx_ref[...], nw_ref[...])
    out = _dot(yn, wo_ref[...]).reshape(bt, lt, d)
    o_ref[...] = x + gt_ref[...] * _rms(out, gpost_ref[...])


def _ssm_sample(x, mod, gpre, gpost, w, conv_in, h0, bt):
    b, lt, d = x.shape
    m = bt * lt
    xspec = pl.BlockSpec((bt, lt, d), lambda i: (i, 0, 0))
    conv_spec = pl.BlockSpec((bt, HIST, SSM_CONV_DIM), lambda i: (i, 0, 0))
    state_spec = pl.BlockSpec((bt,) + _STATE_SHAPE, lambda i: (i, 0, 0, 0))
    mods = [pl.BlockSpec((bt, None, 1, d), functools.partial(lambda k, i: (i, k, 0, 0), 3 + k)) for k in range(3)]
    weights = [w[n] for n in _SSM_WEIGHT_NAMES]
    return pl.pallas_call(
        _ssm_sample_kernel,
        out_shape=(jax.ShapeDtypeStruct(x.shape, F32),
                   jax.ShapeDtypeStruct((b, HIST, SSM_CONV_DIM), F32),
                   jax.ShapeDtypeStruct((b,) + _STATE_SHAPE, F32)),
        grid=(b // bt,),
        in_specs=[xspec] + mods + [_const_spec((1, 1, d)), _const_spec((1, 1, d))]
        + [_const_spec(a.shape, a.dtype == BF16) for a in weights] + [conv_spec, state_spec],
        out_specs=(xspec, conv_spec, state_spec),
        scratch_shapes=[
            pltpu.VMEM((bt, CONV_PAD + lt, SSM_CONV_DIM), F32),
            pltpu.VMEM((m, SSM_CONV_DIM), F32),
            pltpu.VMEM((m, LANES), F32),
            pltpu.VMEM((m, LANES), F32),
            pltpu.VMEM((m, SSM_D_INNER), F32),
        ],
        compiler_params=_params(1, LARGE_VMEM_LIMIT_BYTES),
        name="ssm_sublayer_state",
    )(x, mod, mod, mod, gpre, gpost, *weights, conv_in, h0)


def _ssm_prompt_kernel(xa_ref, xc_ref, sh_ref, sc_ref, gt_ref, gpre_ref, gpost_ref, wz_ref, wxbc_ref, wdt_ref,
                       cw_ref, cb_ref, dtb_ref, alog_ref, dx_ref, nw_ref, wo_ref, e2_ref,
                       o_ref, conv_out_ref, h_out_ref,
                       xpad, z_s, xbc_s, dt_s, da_s, zc_s, xbcc_s, dtc_s, dac_s, y_s, yn_s, ynp_s, h_s,
                       *, steps_per_seq, n_blocks):
    s = pl.program_id(0)
    q = SSM_CHUNK
    lt = 2 * q
    seq_start = (s % steps_per_seq) == 0
    scan_start = ((s + steps_per_seq - 1) % steps_per_seq) == 0
    n_slabs = SSM_CONV_DIM // LANES

    @pl.when(s == 0)
    def _():
        for ref in (z_s, xbc_s, dt_s, da_s, yn_s, h_s):
            ref[...] = jnp.zeros(ref.shape, ref.dtype)

    @pl.when(seq_start)
    def _():
        xpad[:, CONV_PAD - HIST:CONV_PAD, :] = jnp.zeros((n_slabs, HIST, LANES), F32)

    ynp_s[...] = yn_s[...]
    zc_s[...] = z_s[...]
    xbcc_s[...] = xbc_s[...]
    dtc_s[...] = dt_s[...]
    dac_s[...] = da_s[...]
    consts = _chunk_consts(q, q)

    def scan(t, reset):
        rows = slice(t * q, (t + 1) * q)
        xbc_v = xbcc_s.at[:, rows, :]
        _ssd_chunk(0, q, q, 0, xbc_s=xbc_v, dt_s=dtc_s.at[rows], da_s=dac_s.at[rows], y_s=y_s, e2_ref=e2_ref,
                   h_in=h_s, h_out=h_s, consts=consts, reset=reset)
        xs = _cols(xbc_v, slice(None), 0, SSM_D_INNER)
        yn_s[rows, :] = _gate_norm(y_s[...], xs, zc_s[rows, :], dx_ref[...], nw_ref[...])

    out = _dot(ynp_s[...], wo_ref[...])
    o_ref[0] = xc_ref[0] + gt_ref[0] * _rms(out, gpost_ref[0])

    hb = _mod_norm(xa_ref[0], gpre_ref[0], sc_ref[0], sh_ref[0]).astype(BF16)
    z, xbc_raw, dt, da = _ssm_project(hb, wz_ref, wxbc_ref, wdt_ref, dtb_ref, alog_ref)
    z_s[...] = z
    dt_s[...] = dt
    da_s[...] = da
    for c in range(n_slabs):
        xpad[c, CONV_PAD:CONV_PAD + lt, :] = xbc_raw[:, c * LANES:(c + 1) * LANES]

    scan(0, scan_start)

    half = lt // 2
    for c in range(n_slabs):
        csl = slice(c * LANES, (c + 1) * LANES)
        for par in range(2):
            acc = cb_ref[:, csl]
            for k in range(SSM_CONV):
                rows = pl.ds(CONV_PAD - HIST + par + k, half, stride=2)
                acc = acc + xpad[c, rows, :] * cw_ref[k:k + 1, csl]
            xbc_s[c, pl.ds(par, half, stride=2), :] = _silu(acc)
        new_hist = xpad[c, CONV_PAD + lt - HIST:CONV_PAD + lt, :]
        conv_out_ref[0, :, csl] = new_hist
        xpad[c, CONV_PAD - HIST:CONV_PAD, :] = new_hist

    scan(1, None)

    @pl.when(s <= n_blocks)
    def _():
        h_out_ref[...] = h_s[...]


def _ssm_prompt(x, mod, gpre, gpost, w):
    b, l, d = x.shape
    q = SSM_CHUNK
    lt = 2 * q
    assert l % lt == 0
    spq = l // lt
    n_blocks = b * spq

    def lagged(lag):
        def index(s):
            blk = jnp.clip(s - lag, 0, n_blocks - 1)
            return blk // spq, blk % spq
        return index

    cur, prev, prev2 = lagged(0), lagged(1), lagged(2)
    xa_spec = pl.BlockSpec((1, lt, d), lambda s: cur(s) + (0,))
    xc_spec = pl.BlockSpec((1, lt, d), lambda s: prev2(s) + (0,))
    mod_a = [pl.BlockSpec((1, None, 1, d), functools.partial(lambda k, s: (cur(s)[0], k, 0, 0), 3 + k))
             for k in range(2)]
    mod_c = pl.BlockSpec((1, None, 1, d), lambda s: (prev2(s)[0], 5, 0, 0))
    conv_spec = pl.BlockSpec((1, HIST, SSM_CONV_DIM), lambda s: (cur(s)[0], 0, 0))
    state_spec = pl.BlockSpec((1,) + _STATE_SHAPE, lambda s: (prev(s)[0], 0, 0, 0))
    weights = [w[n] for n in _SSM_WEIGHT_NAMES]
    return pl.pallas_call(
        functools.partial(_ssm_prompt_kernel, steps_per_seq=spq, n_blocks=n_blocks),
        out_shape=(jax.ShapeDtypeStruct(x.shape, F32),
                   jax.ShapeDtypeStruct((b, HIST, SSM_CONV_DIM), F32),
                   jax.ShapeDtypeStruct((b,) + _STATE_SHAPE, F32)),
        grid=(n_blocks + 2,),
        in_specs=[xa_spec, xc_spec] + mod_a + [mod_c, _const_spec((1, 1, d)), _const_spec((1, 1, d))]
        + [_const_spec(a.shape, a.dtype == BF16) for a in weights],
        out_specs=(xc_spec, conv_spec, state_spec),
        scratch_shapes=[
            pltpu.VMEM((SSM_CONV_DIM // LANES, CONV_PAD + lt, LANES), F32),
            pltpu.VMEM((lt, SSM_D_INNER), F32),
            pltpu.VMEM((SSM_CONV_DIM // LANES, lt, LANES), F32),
            pltpu.VMEM((lt, LANES), F32),
            pltpu.VMEM((lt, LANES), F32),
            pltpu.VMEM((lt, SSM_D_INNER), F32),
            pltpu.VMEM((SSM_CONV_DIM // LANES, lt, LANES), F32),
            pltpu.VMEM((lt, LANES), F32),
            pltpu.VMEM((lt, LANES), F32),
            pltpu.VMEM((q, SSM_D_INNER), F32),
            pltpu.VMEM((lt, SSM_D_INNER), BF16),
            pltpu.VMEM((lt, SSM_D_INNER), BF16),
            pltpu.VMEM((1,) + _STATE_SHAPE, F32),
        ],
        compiler_params=_params(1),
        name="ssm_sublayer",
    )(x, x, mod, mod, mod, gpre, gpost, *weights)


def _ssm_weights(in_w, conv_w, conv_b, dt_bias, a_log, d_skip, norm_w, out_w):
    pad = LANES - SSM_HEADS
    heads = np.arange(SSM_D_INNER) // SSM_HEAD_DIM
    e = (np.arange(LANES)[:, None] == heads[None, :]).astype(np.float32)
    return dict(
        wz=in_w[:, :SSM_D_INNER].astype(BF16),
        wxbc=in_w[:, SSM_D_INNER:SSM_D_INNER + SSM_CONV_DIM].astype(BF16),
        wdt=jnp.pad(in_w[:, SSM_D_INNER + SSM_CONV_DIM:], ((0, 0), (0, pad))).astype(BF16),
        conv_w=conv_w,
        conv_b=conv_b.reshape(1, SSM_CONV_DIM),
        dt_bias=jnp.pad(dt_bias, (0, pad)).reshape(1, LANES),
        a_log=jnp.pad(a_log, (0, pad)).reshape(1, LANES),
        d_x=jnp.repeat(d_skip, SSM_HEAD_DIM).reshape(1, SSM_D_INNER),
        norm_w=norm_w.reshape(1, SSM_D_INNER),
        wo=out_w.astype(BF16),
        e2=jnp.asarray(np.concatenate([e, e], axis=0), BF16),
    )


SOFTMAX_ROWS = 64


def _softmax_sink(logits, sink):
    mx = jnp.maximum(jnp.max(logits, axis=-1, keepdims=True), sink)
    e = jnp.exp(logits - mx)
    denom = jnp.sum(e, axis=-1, keepdims=True) + jnp.exp(sink - mx)
    return e * (1.0 / denom)


def _dedup(t):
    lane_lo = lax.broadcasted_iota(jnp.int32, (t.shape[0], LANES), 1) < HALF
    tiles = [jnp.where(lane_lo, t[:, (2 * i) * LANES:(2 * i + 1) * LANES],
                       t[:, (2 * i + 1) * LANES:(2 * i + 2) * LANES]) for i in range(ATTN_KV_HEADS // 2)]
    return jnp.concatenate(tiles, axis=1)


def _swa_prompt_kernel(sinks_ref, xa_ref, xc_ref, sh_ref, sc_ref, gt_ref, gpre_ref, gpost_ref, wq_ref, wk_ref,
                       wv_ref, bq_ref, bk_ref, bv_ref, bias_ref, wo_ref, bo_ref,
                       o_ref, kc_ref, vc_ref, qn_s, kn_s, vn_s, q_s, kbuf, vbuf, att_s, lg_s, p_s,
                       *, steps_per_seq):
    s = pl.program_id(0)
    tq = xa_ref.shape[1]
    n_tiles = 2 * ATTN_KV_HEADS
    pairs = ATTN_REP // 2

    @pl.when(s == 0)
    def _():
        for ref in (qn_s, kn_s, vn_s, kbuf, vbuf):
            ref[...] = jnp.zeros(ref.shape, ref.dtype)

    kbuf[0:WINDOW, :] = kbuf[tq:tq + WINDOW, :]
    vbuf[0:WINDOW, :] = vbuf[tq:tq + WINDOW, :]
    kbuf[WINDOW:WINDOW + tq, :] = kn_s[...]
    vbuf[WINDOW:WINDOW + tq, :] = vn_s[...]
    q_s[...] = qn_s[...]

    hb = _mod_norm(xa_ref[0], gpre_ref[0], sc_ref[0], sh_ref[0]).astype(BF16)
    half_q = ATTN_Q_DIM // 2

    def project_q(lo):
        qn_s[:, lo:lo + half_q] = ((_dot(hb, wq_ref[:, lo:lo + half_q]) + bq_ref[:, lo:lo + half_q])
                                   * (ATTN_HEAD_DIM ** -0.5)).astype(BF16)

    def project_kv(w_ref, b_ref, nxt, cache_ref):
        kv = _dot(hb, w_ref[...]) + b_ref[...]
        nxt[...] = kv.astype(BF16)
        cache_ref[0] = _dedup(kv[tq - WINDOW:, :])

    parts = [functools.partial(project_q, 0), functools.partial(project_q, half_q),
             functools.partial(project_kv, wk_ref, bk_ref, kn_s, kc_ref),
             functools.partial(project_kv, wv_ref, bv_ref, vn_s, vc_ref)]

    lane_lo = lax.broadcasted_iota(jnp.int32, (2 * WINDOW, LANES), 1) < HALF
    zero_b = jnp.zeros((), BF16)
    seq_first = jnp.where(((s + steps_per_seq - 1) % steps_per_seq) == 0, n_tiles, 0)
    n_q = tq // WINDOW
    for bi in range(n_q):
        r0 = bi * WINDOW
        qrows = slice(r0, r0 + WINDOW)
        first = seq_first if bi == 0 else 0
        lg, pb = lg_s.at[bi], p_s.at[bi]
        for g in range(ATTN_KV_HEADS):
            kd = kbuf[r0:r0 + 2 * WINDOW, g * LANES:(g + 1) * LANES]
            q2 = jnp.concatenate([q_s[qrows, (g * pairs + pr) * LANES:(g * pairs + pr + 1) * LANES]
                                  for pr in range(pairs)], axis=0)
            for half in range(2):
                kh = jnp.where(lane_lo, kd, zero_b) if half == 0 else jnp.where(lane_lo, zero_b, kd)
                t = 2 * g + half
                lg[t] = _dot_nt(q2, kh) + bias_ref[first + t]
        for part in parts[bi * len(parts) // n_q:(bi + 1) * len(parts) // n_q]:
            part()
        for t in range(n_tiles):
            for rc in range(2 * WINDOW // SOFTMAX_ROWS):
                rs = slice(rc * SOFTMAX_ROWS, (rc + 1) * SOFTMAX_ROWS)
                h = 4 * (t // 2) + (t % 2) + 2 * ((rc * SOFTMAX_ROWS) // WINDOW)
                pb[t, rs, :] = _softmax_sink(lg[t, rs, :], sinks_ref[h]).astype(BF16)
        for g in range(ATTN_KV_HEADS):
            vd = vbuf[r0:r0 + 2 * WINDOW, g * LANES:(g + 1) * LANES]
            acc = (_dot(pb[2 * g], jnp.where(lane_lo, vd, zero_b))
                   + _dot(pb[2 * g + 1], jnp.where(lane_lo, zero_b, vd)))
            for pr in range(pairs):
                pair = g * pairs + pr
                att_s[qrows, pair * LANES:(pair + 1) * LANES] = acc[pr * WINDOW:(pr + 1) * WINDOW].astype(BF16)
        out = _dot(att_s[qrows, :], wo_ref[...]) + bo_ref[...]
        o_ref[0, qrows, :] = xc_ref[0, qrows, :] + gt_ref[0] * _rms(out, gpost_ref[0])


def _swa_prompt(x, mod, gpre, gpost, w, bias, tq):
    b, l, d = x.shape
    kvw = ATTN_KV_HEADS * LANES
    pairs = ATTN_REP // 2
    n_tiles = 2 * ATTN_KV_HEADS
    assert l % tq == 0 and tq % WINDOW == 0
    spq = l // tq
    n_blocks = b * spq
    bias = bias.reshape(2, ATTN_KV_HEADS, pairs, 2, WINDOW, 2 * WINDOW).transpose(0, 1, 3, 2, 4, 5)
    bias = bias.reshape(2 * n_tiles, pairs * WINDOW, 2 * WINDOW)

    def cur(s):
        blk = jnp.minimum(s, n_blocks - 1)
        return blk // spq, blk % spq

    def prev(s):
        blk = jnp.maximum(s - 1, 0)
        return blk // spq, blk % spq

    xa_spec = pl.BlockSpec((1, tq, d), lambda s: cur(s) + (0,))
    xc_spec = pl.BlockSpec((1, tq, d), lambda s: prev(s) + (0,))
    mod_a = [pl.BlockSpec((1, None, 1, d), functools.partial(lambda k, s: (cur(s)[0], k, 0, 0), 3 + k))
             for k in range(2)]
    mod_c = pl.BlockSpec((1, None, 1, d), lambda s: (prev(s)[0], 5, 0, 0))
    cache_spec = pl.BlockSpec((1, WINDOW, ATTN_KV_DIM), lambda s: (cur(s)[0], 0, 0))
    consts = [w["wq"], w["wk_dup"], w["wv_dup"], w["bq"], w["bk_dup"], w["bv_dup"], bias, w["wo"], w["bo"]]
    n_q = tq // WINDOW
    return pl.pallas_call(
        functools.partial(_swa_prompt_kernel, steps_per_seq=spq),
        out_shape=(jax.ShapeDtypeStruct(x.shape, F32),
                   jax.ShapeDtypeStruct((b, WINDOW, ATTN_KV_DIM), F32),
                   jax.ShapeDtypeStruct((b, WINDOW, ATTN_KV_DIM), F32)),
        grid=(n_blocks + 1,),
        in_specs=[pl.BlockSpec(memory_space=pltpu.SMEM), xa_spec, xc_spec] + mod_a
        + [mod_c, _const_spec((1, 1, d)), _const_spec((1, 1, d))]
        + [_const_spec(a.shape, a.dtype == BF16) for a in consts],
        out_specs=(xc_spec, cache_spec, cache_spec),
        scratch_shapes=[
            pltpu.VMEM((tq, ATTN_Q_DIM), BF16),
            pltpu.VMEM((tq, kvw), BF16),
            pltpu.VMEM((tq, kvw), BF16),
            pltpu.VMEM((tq, ATTN_Q_DIM), BF16),
            pltpu.VMEM((WINDOW + tq, kvw), BF16),
            pltpu.VMEM((WINDOW + tq, kvw), BF16),
            pltpu.VMEM((tq, ATTN_Q_DIM), BF16),
            pltpu.VMEM((n_q, n_tiles, pairs * WINDOW, 2 * WINDOW), F32),
            pltpu.VMEM((n_q, n_tiles, pairs * WINDOW, 2 * WINDOW), BF16),
        ],
        compiler_params=_params(1),
        name="swa_prompt_sublayer",
    )(w["sinks"], x, x, mod, mod, mod, gpre, gpost, *consts)


def _swa_sample_kernel(x_ref, sh_ref, sc_ref, gt_ref, gpre_ref, gpost_ref, wq_ref, wk_ref, wv_ref,
                       bq_ref, bk_ref, bv_ref, ck_ref, cv_ref, bias_ref, sink_ref, wo_ref, bo_ref,
                       o_ref, ko_ref, vo_ref, q_s, kn_s, vn_s, att_s):
    x = x_ref[...]
    bt, lt, d = x.shape
    m = bt * lt
    nbuf = ck_ref.shape[1]
    wide = ATTN_KV_HEADS * ATTN_HEAD_DIM

    hb = _mod_norm(x, gpre_ref[...], sc_ref[...], sh_ref[...]).reshape(m, d).astype(BF16)
    q_s[...] = (_dot(hb, wq_ref[...]) + bq_ref[...]) * (ATTN_HEAD_DIM ** -0.5)
    kn_s[...] = _dot(hb, wk_ref[...]) + bk_ref[...]
    vn_s[...] = _dot(hb, wv_ref[...]) + bv_ref[...]
    zpad = jnp.zeros((bias_ref.shape[1] - nbuf - lt, ATTN_KV_DIM), F32)

    def seq(bi, carry):
        rows = pl.ds(pl.multiple_of(bi * lt, lt), lt)
        kn = kn_s[rows, :]
        vn = vn_s[rows, :]
        ck = ck_ref[bi]
        cv = cv_ref[bi]
        kfull = jnp.concatenate([ck, kn, zpad], axis=0).astype(BF16)
        vfull = jnp.concatenate([cv, vn, zpad], axis=0).astype(BF16)
        ko_ref[bi, 0:nbuf - lt, :] = ck[lt:, :]
        vo_ref[bi, 0:nbuf - lt, :] = cv[lt:, :]
        ko_ref[bi, nbuf - lt:nbuf, :] = kn
        vo_ref[bi, nbuf - lt:nbuf, :] = vn
        qbig = jnp.concatenate([q_s[rows, h * wide:(h + 1) * wide] for h in range(ATTN_HEADS)], axis=0)
        logits = _dot_nt(qbig.astype(BF16), kfull) + bias_ref[...]
        p = _softmax_sink(logits, sink_ref[:, 0:1]).astype(BF16)
        res = _dot(p, vfull)
        att_s[rows, :] = jnp.concatenate([res[h * lt:(h + 1) * lt, :] for h in range(ATTN_HEADS)], axis=1)
        return carry

    lax.fori_loop(0, bt, seq, 0, unroll=8)

    out = (_dot(att_s[...].astype(BF16), wo_ref[...]) + bo_ref[...]).reshape(bt, lt, d)
    o_ref[...] = x + gt_ref[...] * _rms(out, gpost_ref[...])


def _swa_sample(x, mod, gpre, gpost, w, bias, cache_k, cache_v, bt):
    b, lt, d = x.shape
    nbuf = cache_k.shape[1]
    assert nbuf == WINDOW and lt <= WINDOW and b % bt == 0
    m = bt * lt
    wide = ATTN_KV_HEADS * ATTN_HEAD_DIM
    big = ATTN_HEADS * wide
    keys = 2 * WINDOW
    xspec = pl.BlockSpec((bt, lt, d), lambda i: (i, 0, 0))
    cache_spec = pl.BlockSpec((bt, nbuf, ATTN_KV_DIM), lambda i: (i, 0, 0))
    bias_s = bias[0, :, :lt, :].reshape(ATTN_HEADS * lt, keys)
    sink_col = jnp.broadcast_to(jnp.repeat(w["sinks"], lt)[:, None], (ATTN_HEADS * lt, LANES))
    mods = [pl.BlockSpec((bt, None, 1, D_MODEL), functools.partial(lambda k, i: (i, k, 0, 0), 3 + k))
            for k in range(3)]
    pre = [w["wq_big"], w["wk"], w["wv"], w["bq_big"], w["bk"], w["bv"]]
    post = [bias_s, sink_col, w["wo_big"], w["bo"]]
    return pl.pallas_call(
        _swa_sample_kernel,
        out_shape=(jax.ShapeDtypeStruct(x.shape, F32),
                   jax.ShapeDtypeStruct(cache_k.shape, F32),
                   jax.ShapeDtypeStruct(cache_v.shape, F32)),
        grid=(b // bt,),
        in_specs=[xspec] + mods + [_const_spec((1, 1, d)), _const_spec((1, 1, d))]
        + [_const_spec(a.shape, a.dtype == BF16) for a in pre] + [cache_spec, cache_spec]
        + [_const_spec(a.shape, a.dtype == BF16) for a in post],
        out_specs=(xspec, cache_spec, cache_spec),
        scratch_shapes=[
            pltpu.VMEM((m, big), F32),
            pltpu.VMEM((m, ATTN_KV_DIM), F32),
            pltpu.VMEM((m, ATTN_KV_DIM), F32),
            pltpu.VMEM((m, big), F32),
        ],
        compiler_params=_params(1),
        name="swa_sample_sublayer",
    )(x, mod, mod, mod, gpre, gpost, *pre, cache_k, cache_v, *post)


def _swa_weights(qkv_w, qkv_b, sinks, o_w, o_b):
    d = qkv_w.shape[0]
    wq = qkv_w[:, :ATTN_Q_DIM]
    wk = qkv_w[:, ATTN_Q_DIM:ATTN_Q_DIM + ATTN_KV_DIM]
    wv = qkv_w[:, ATTN_Q_DIM + ATTN_KV_DIM:]
    bq = qkv_b[:ATTN_Q_DIM]
    bk = qkv_b[ATTN_Q_DIM:ATTN_Q_DIM + ATTN_KV_DIM]
    bv = qkv_b[ATTN_Q_DIM + ATTN_KV_DIM:]

    def dup(a):
        a4 = a.reshape(a.shape[:-1] + (ATTN_KV_HEADS, 1, ATTN_HEAD_DIM))
        a4 = jnp.broadcast_to(a4, a.shape[:-1] + (ATTN_KV_HEADS, 2, ATTN_HEAD_DIM))
        return a4.reshape(a.shape[:-1] + (ATTN_KV_HEADS * LANES,))

    own = jnp.asarray((np.arange(ATTN_HEADS)[:, None] // ATTN_REP == np.arange(ATTN_KV_HEADS)[None, :])
                      .astype(np.float32))
    wq_big = (wq.reshape(d, ATTN_HEADS, 1, ATTN_HEAD_DIM) * own[None, :, :, None]).reshape(d, -1)
    bq_big = (bq.reshape(ATTN_HEADS, 1, ATTN_HEAD_DIM) * own[:, :, None]).reshape(1, -1)
    wo_big = (o_w.reshape(ATTN_HEADS, 1, ATTN_HEAD_DIM, d) * own[:, :, None, None]).reshape(-1, d)
    return dict(
        sinks=sinks,
        wq=wq.astype(BF16), wk_dup=dup(wk).astype(BF16), wv_dup=dup(wv).astype(BF16),
        bq=bq.reshape(1, -1), bk_dup=dup(bk).reshape(1, -1), bv_dup=dup(bv).reshape(1, -1),
        wo=o_w.astype(BF16), bo=o_b.reshape(1, d),
        wq_big=wq_big.astype(BF16), bq_big=bq_big, wk=wk.astype(BF16), wv=wv.astype(BF16),
        bk=bk.reshape(1, -1), bv=bv.reshape(1, -1), wo_big=wo_big.astype(BF16),
    )


def _trunk(x, mod_all, state_ssm, state_conv, cache_k, cache_v, norm_pre, norm_post, ffn_in, ffn_out,
           ssm_w, swa_w, bias, tiles):
    bt, lt, n_sub = tiles["ffn"]
    b, l, d = x.shape
    sample = state_ssm is not None
    outs = {}
    for i in range(2):
        mod = mod_all[i].reshape(b, N_SUB * 3, 1, d)
        gpre = norm_pre[i].reshape(N_SUB, 1, 1, d)
        gpost = norm_post[i].reshape(N_SUB, 1, 1, d)
        x = _ffn(x, mod, 0, gpre[0], gpost[0], ffn_in, ffn_out, i, 0, bt, lt, n_sub)
        if i == 0:
            if sample:
                x, conv_new, ssm_new = _ssm_sample(x, mod, gpre[1], gpost[1], ssm_w, state_conv[0], state_ssm[0],
                                                   tiles["ssm"])
            else:
                x, conv_new, ssm_new = _ssm_prompt(x, mod, gpre[1], gpost[1], ssm_w)
            outs["conv"] = conv_new[None]
            outs["ssm"] = ssm_new[None]
        else:
            if sample:
                x, k_new, v_new = _swa_sample(x, mod, gpre[1], gpost[1], swa_w, bias,
                                              cache_k[0].reshape(b, -1, ATTN_KV_DIM),
                                              cache_v[0].reshape(b, -1, ATTN_KV_DIM), tiles["swa"])
            else:
                x, k_new, v_new = _swa_prompt(x, mod, gpre[1], gpost[1], swa_w, bias, tiles["swa"])
            shape = (1, b, -1, ATTN_KV_HEADS, ATTN_HEAD_DIM)
            outs["k"] = k_new.reshape(shape)
            outs["v"] = v_new.reshape(shape)
        x = _ffn(x, mod, 2, gpre[2], gpost[2], ffn_in, ffn_out, i, 1, bt, lt, n_sub)
    return x, outs["ssm"], outs["conv"], outs["k"], outs["v"]


def kernel(x_prompt, x_sample, state_ssm, state_conv, cache_k, cache_v, c_prompt, c_sample, ada_w, ada_b, norm_pre, norm_post, ffn_w_in, ffn_w_out, ssm_in_w, ssm_conv_w, ssm_conv_b, ssm_dt_bias, ssm_a_log, ssm_d, ssm_norm_w, ssm_out_w, attn_qkv_w, attn_qkv_b, attn_sinks, attn_o_w, attn_o_b, rel_bias):
    nb = x_prompt.shape[0]
    mod_all = _ada(jnp.concatenate([c_prompt, c_sample], axis=0), ada_w, ada_b)
    bias = _bias_table(rel_bias)
    ffn_in = ffn_w_in.astype(BF16)
    ffn_out = ffn_w_out.astype(BF16)
    ssm_w = _ssm_weights(ssm_in_w[0], ssm_conv_w[0], ssm_conv_b[0], ssm_dt_bias[0], ssm_a_log[0], ssm_d[0],
                         ssm_norm_w[0], ssm_out_w[0])
    swa_w = _swa_weights(attn_qkv_w[0], attn_qkv_b[0], attn_sinks[0], attn_o_w[0], attn_o_b[0])
    common = (norm_pre, norm_post, ffn_in, ffn_out, ssm_w, swa_w, bias)

    p_tiles = dict(ffn=(1, FFN_TILE_ROWS, FFN_SUB_TILES), swa=SWA_TILE_ROWS)
    y_p, ssm_p, conv_p, k_p, v_p = _trunk(x_prompt, mod_all[:, :nb], None, None, None, None, *common, p_tiles)
    ns, ls = x_sample.shape[:2]
    s_tiles = dict(ffn=(min(SAMPLE_FFN_SEQS, ns), ls, 1), ssm=min(SAMPLE_SSM_SEQS, ns),
                   swa=min(SAMPLE_SWA_SEQS, ns))
    y_s, ssm_s, conv_s, k_s, v_s = _trunk(x_sample, mod_all[:, nb:], state_ssm, state_conv, cache_k, cache_v,
                                          *common, s_tiles)
    return (y_p, y_s, ssm_p, conv_p, k_p, v_p, ssm_s, conv_s, k_s, v_s)
```

```python
import functools
import math

import numpy as np
import jax
import jax.numpy as jnp
from jax import lax
from jax.experimental import pallas as pl
from jax.experimental.pallas import tpu as pltpu

F32 = jnp.float32
BF16 = jnp.bfloat16

D_MODEL = 1024
N_SUB = 3
RMS_EPS = 1e-6
FFN_RES = 0.5
D_FF = 2816

SSM_D_INNER = 2048
SSM_HEAD_DIM = 64
SSM_HEADS = 32
SSM_GROUPS = 4
SSM_HPG = 8
SSM_STATE = 128
SSM_CONV = 4
SSM_CHUNK = 128
SSM_GN = SSM_GROUPS * SSM_STATE
SSM_CONV_DIM = SSM_D_INNER + 2 * SSM_GN
SSM_GROUP_WIDTH = SSM_HPG * SSM_HEAD_DIM

ATTN_HEAD_DIM = 64
ATTN_HEADS = 16
ATTN_KV_HEADS = 4
ATTN_REP = 4
WINDOW = 128
REL_BUCKETS = 32
ATTN_Q_DIM = ATTN_HEADS * ATTN_HEAD_DIM
ATTN_KV_DIM = ATTN_KV_HEADS * ATTN_HEAD_DIM

LANES = 128
SUBLANES = 8
HALF = LANES // 2
VMEM_LIMIT_BYTES = 56 * 1024 * 1024
SSM_SAMPLE_VMEM_LIMIT_BYTES = 62 * 1024 * 1024
CONV_PAD = SUBLANES

ADA_TILE_COLS = 9 * LANES
FFN_TILE_ROWS = 512
FFN_SUB_TILES = 1
SWA_TILE_ROWS = 512
SAMPLE_FFN_SEQS = 64
SAMPLE_SSM_SEQS = 8
SAMPLE_SWA_SEQS = 16


def _dot(a, b):
    return jnp.dot(a, b, preferred_element_type=F32)


def _dot_nt(a, b):
    return lax.dot_general(a, b, (((1,), (1,)), ((), ())), preferred_element_type=F32)


def _dot_tn(a, b):
    return lax.dot_general(a, b, (((0,), (0,)), ((), ())), preferred_element_type=F32)


def _silu(x):
    h = 0.5 * x
    return h * jnp.tanh(h) + h


def _rms(x, g):
    return x * lax.rsqrt(jnp.mean(x * x, axis=-1, keepdims=True) + RMS_EPS) * g


def _mod_norm(x, g, scale, shift):
    return _rms(x, g) * (1.0 + scale) + shift


def _split3(x):
    hi = x.astype(BF16)
    r1 = x - hi.astype(F32)
    mid = r1.astype(BF16)
    lo = (r1 - mid.astype(F32)).astype(BF16)
    return hi, mid, lo


def _const_spec(shape, single_buffer=False):
    nd = len(shape)
    kw = {"pipeline_mode": pl.Buffered(1)} if single_buffer else {}
    return pl.BlockSpec(shape, lambda *_: (0,) * nd, **kw)


def _params(n_grid, vmem_limit=VMEM_LIMIT_BYTES):
    return pltpu.CompilerParams(
        dimension_semantics=("arbitrary",) * n_grid,
        vmem_limit_bytes=vmem_limit,
    )


def _ada_kernel(c_ref, w_ref, b_ref, o_ref):
    cs = _silu(c_ref[...]).astype(BF16)
    o_ref[0] = _dot(cs, w_ref[0].astype(BF16)) + b_ref[0]


def _ada(c_all, ada_w, ada_b, tn=ADA_TILE_COLS):
    depth, d, n = ada_w.shape
    bc = c_all.shape[0]
    return pl.pallas_call(
        _ada_kernel,
        out_shape=jax.ShapeDtypeStruct((depth, bc, n), F32),
        grid=(depth, n // tn),
        in_specs=[
            pl.BlockSpec((bc, d), lambda l, j: (0, 0)),
            pl.BlockSpec((1, d, tn), lambda l, j: (l, 0, j)),
            pl.BlockSpec((1, 1, tn), lambda l, j: (l, 0, j)),
        ],
        out_specs=pl.BlockSpec((1, bc, tn), lambda l, j: (l, 0, j)),
        compiler_params=_params(2),
        name="ada_mod",
    )(c_all, ada_w, ada_b.reshape(depth, 1, n))


def _t5_bucket_table():
    i = np.arange(WINDOW)[:, None]
    j = np.arange(2 * WINDOW)[None, :]
    dist = i + WINDOW - j
    exact = REL_BUCKETS // 2
    df = np.maximum(dist, 1).astype(np.float32)
    large = exact + (np.log(df / np.float32(exact)) / np.float32(math.log(WINDOW / exact))
                     * np.float32(REL_BUCKETS - exact)).astype(np.int32)
    large = np.minimum(large, REL_BUCKETS - 1)
    bucket = np.where(dist < exact, dist, large)
    valid = (dist >= 0) & (dist <= WINDOW)
    return np.where(valid, bucket, -1).astype(np.int32)


def _bias_kernel(rb_ref, idx_ref, o_ref):
    h = pl.program_id(0)
    idx = idx_ref[...]
    acc = jnp.full(idx.shape, -jnp.inf, F32)
    for b in range(REL_BUCKETS):
        acc = jnp.where(idx == b, rb_ref[b, h], acc)
    o_ref[0, 0] = acc
    col = lax.broadcasted_iota(jnp.int32, idx.shape, 1)
    o_ref[1, 0] = jnp.where(col < WINDOW, -jnp.inf, acc)


def _bias_table(rel_bias):
    idx = jnp.asarray(_t5_bucket_table())
    return pl.pallas_call(
        _bias_kernel,
        out_shape=jax.ShapeDtypeStruct((2, ATTN_HEADS, WINDOW, 2 * WINDOW), F32),
        grid=(ATTN_HEADS,),
        in_specs=[
            pl.BlockSpec(memory_space=pltpu.SMEM),
            pl.BlockSpec((WINDOW, 2 * WINDOW), lambda h: (0, 0)),
        ],
        out_specs=pl.BlockSpec((2, 1, WINDOW, 2 * WINDOW), lambda h: (0, h, 0, 0)),
        compiler_params=_params(1),
        name="rel_bias_table",
    )(rel_bias, idx)


def _ffn_kernel(x_ref, sh_ref, sc_ref, gt_ref, gpre_ref, gpost_ref, win_ref, wout_ref, o_ref, *, n_sub):
    bt, lt, d = x_ref.shape

    def sub_slices(s):
        if bt == 1:
            return slice(None), slice(s * (lt // n_sub), (s + 1) * (lt // n_sub))
        return slice(s * (bt // n_sub), (s + 1) * (bt // n_sub)), slice(None)

    def pre(s):
        bs, ls = sub_slices(s)
        x = x_ref[bs, ls, :]
        h = _mod_norm(x, gpre_ref[...], sc_ref[bs], sh_ref[bs])
        return h.reshape(x.shape[0] * x.shape[1], d).astype(BF16)

    def post(s, acc):
        bs, ls = sub_slices(s)
        x = x_ref[bs, ls, :]
        o_ref[bs, ls, :] = x + FFN_RES * gt_ref[bs] * _rms(acc.reshape(x.shape), gpost_ref[...])

    hb = pre(0)
    prev = None
    for s in range(n_sub):
        g = _dot(hb, win_ref[:, 0:D_FF])
        u = _dot(hb, win_ref[:, D_FF:2 * D_FF])
        acc = _dot((_silu(g) * u).astype(BF16), wout_ref[...])
        if prev is not None:
            post(s - 1, prev)
        if s + 1 < n_sub:
            hb = pre(s + 1)
        prev = acc
    post(n_sub - 1, prev)


def _mod_specs(bt, sub):
    return [pl.BlockSpec((bt, None, 1, D_MODEL), functools.partial(lambda k, i, j: (i, k, 0, 0), sub * 3 + k))
            for k in range(3)]


def _ffn(x, mod, sub, gpre, gpost, w_in, w_out, layer, which, bt, lt, n_sub):
    b, l, d = x.shape
    assert b % bt == 0 and l % lt == 0 and (lt if bt == 1 else bt) % n_sub == 0
    xspec = pl.BlockSpec((bt, lt, d), lambda i, j: (i, j, 0))
    wspec = [pl.BlockSpec((None, None) + w.shape[2:], lambda i, j: (layer, which, 0, 0),
                          pipeline_mode=pl.Buffered(1)) for w in (w_in, w_out)]
    return pl.pallas_call(
        functools.partial(_ffn_kernel, n_sub=n_sub),
        out_shape=jax.ShapeDtypeStruct(x.shape, F32),
        grid=(b // bt, l // lt),
        in_specs=[xspec] + _mod_specs(bt, sub) + [
            _const_spec((1, 1, d)), _const_spec((1, 1, d))] + wspec,
        out_specs=xspec,
        compiler_params=_params(2, SSM_SAMPLE_VMEM_LIMIT_BYTES),
        name="ffn_sublayer",
    )(x, mod, mod, mod, gpre, gpost, w_in, w_out)


def _chunk_consts(qc, seg):
    r = lax.broadcasted_iota(jnp.int32, (qc, qc), 0)
    c = lax.broadcasted_iota(jnp.int32, (qc, qc), 1)
    seg_shift = seg.bit_length() - 1
    same = jnp.right_shift(r, seg_shift) == jnp.right_shift(c, seg_shift)
    causal = same & (r >= c)
    tri = jnp.where(causal, 1.0, 0.0)
    upper = jnp.where(same & (c > r), 1.0, 0.0)
    tu = jnp.concatenate([tri, upper], axis=0).astype(BF16)
    er = lax.broadcasted_iota(jnp.int32, (LANES, LANES), 0)
    ec = lax.broadcasted_iota(jnp.int32, (LANES, LANES), 1)
    eye = jnp.where(er == ec, 1.0, 0.0).astype(BF16)
    lane_lo = lax.broadcasted_iota(jnp.int32, (qc, LANES), 1) < HALF
    return tu, eye, causal, lane_lo


def _ssm_project(hb, wz_ref, wxbc_ref, wdt_ref, dtb_ref, alog_ref):
    z = _dot(hb, wz_ref[...])
    xbc_raw = _dot(hb, wxbc_ref[...])
    dt_raw = _dot(hb, wdt_ref[...]) + dtb_ref[...]
    dt = jnp.maximum(dt_raw, 0.0) + jnp.log1p(jnp.exp(-jnp.abs(dt_raw)))
    return z, xbc_raw, dt, dt * (-jnp.exp(alog_ref[...]))


def _gate_norm(y, xs, z, dx, nw):
    yg = (y + xs * dx) * _silu(z)
    parts = []
    for g in range(SSM_GROUPS):
        v = yg[:, g * SSM_GROUP_WIDTH:(g + 1) * SSM_GROUP_WIDTH]
        parts.append(v * lax.rsqrt(jnp.mean(v * v, axis=-1, keepdims=True) + RMS_EPS))
    return (jnp.concatenate(parts, axis=1) * nw).astype(BF16)


def _cols(ref, rows, c0, c1):
    if len(ref.shape) == 2:
        return ref[rows, c0:c1]
    return jnp.concatenate([ref[c, rows, :] for c in range(c0 // LANES, c1 // LANES)], axis=1)


def _ssd_chunk(r0, qc, seg, s_first, *, xbc_s, dt_s, da_s, y_s, e2_ref, h_in, h_out, consts, reset=None):
    tu, eye, causal, lane_lo = consts
    rows = pl.ds(r0, qc)
    hi, mid, lo = _split3(da_s[rows, :])
    cs2 = _dot(tu, hi) + _dot(tu, mid) + _dot(tu, lo)
    a_cs = cs2[:qc]
    ea = jnp.exp(a_cs)
    dte = jnp.exp(cs2[qc:])

    stack = jnp.concatenate([dt_s[rows, :], ea, dte], axis=0)
    s_hi = stack.astype(BF16)
    s_lo = (stack - s_hi.astype(F32)).astype(BF16)
    sx = _dot(jnp.concatenate([s_hi, s_lo], axis=1), e2_ref[...])
    dt_x, ea_x, dte_x = sx[:qc], sx[qc:2 * qc], sx[2 * qc:]

    xdt = _cols(xbc_s, rows, 0, SSM_D_INNER) * dt_x
    xdt_b = xdt.astype(BF16)
    xd_b = (xdt * dte_x).astype(BF16)
    bm = _cols(xbc_s, rows, SSM_D_INNER, SSM_D_INNER + SSM_GN).astype(BF16)
    cm = _cols(xbc_s, rows, SSM_D_INNER + SSM_GN, SSM_CONV_DIM).astype(BF16)

    a_hi, a_mid, a_lo = _split3(a_cs)
    a_cs_t = _dot_nt(eye, a_hi) + _dot_nt(eye, a_mid) + _dot_nt(eye, a_lo)

    zero_b = jnp.zeros((), BF16)
    for g in range(SSM_GROUPS):
        gsl = slice(g * SSM_STATE, (g + 1) * SSM_STATE)
        cb = _dot_nt(cm[:, gsl], bm[:, gsl])
        for pr in range(SSM_HPG // 2):
            h0 = g * SSM_HPG + 2 * pr
            psl = slice((h0 // 2) * LANES, (h0 // 2 + 1) * LANES)
            xp = xdt_b[:, psl]
            acc = None
            for half in range(2):
                h = h0 + half
                seg_sum = a_cs[:, h:h + 1] - a_cs_t[h:h + 1, :]
                decay = jnp.exp(jnp.where(causal, seg_sum, -jnp.inf))
                w = (decay * cb).astype(BF16)
                xh = jnp.where(lane_lo if half == 0 else jnp.logical_not(lane_lo), xp, zero_b)
                o = _dot(w, xh)
                acc = o if acc is None else acc + o
            y_s[rows, psl] = acc

    for t in range(qc // seg):
        tr = slice(t * seg, (t + 1) * seg)
        trows = pl.ds(r0 + t * seg, seg)
        last = t * seg + seg - 1
        for g in range(SSM_GROUPS):
            gsl = slice(g * SSM_STATE, (g + 1) * SSM_STATE)
            csl = slice(g * SSM_GROUP_WIDTH, (g + 1) * SSM_GROUP_WIDTH)
            hsl = slice(g * SSM_HPG, (g + 1) * SSM_HPG)
            hg = h_in[s_first + t, hsl].reshape(SSM_GROUP_WIDTH, SSM_STATE)
            if reset is not None:
                hg = jnp.where(reset, 0.0, hg)
            y_off = _dot_nt(cm[tr, gsl], hg.astype(BF16)) * ea_x[tr, csl]
            y_s[trows, csl] = y_s[trows, csl] + y_off
            upd = _dot_tn(xd_b[tr, csl], bm[tr, gsl])
            cdec = jnp.concatenate(
                [jnp.broadcast_to(ea[last:last + 1, g * SSM_HPG + r:g * SSM_HPG + r + 1],
                                  (SSM_HEAD_DIM, SSM_STATE)) for r in range(SSM_HPG)], axis=0)
            h_out[s_first + t, hsl] = (hg * cdec + upd).reshape(SSM_HPG, SSM_HEAD_DIM, SSM_STATE)


HIST = SSM_CONV - 1
_SSM_WEIGHT_NAMES = ("wz", "wxbc", "wdt", "conv_w", "conv_b", "dt_bias", "a_log", "d_x", "norm_w", "wo", "e2")
_STATE_SHAPE = (SSM_HEADS, SSM_HEAD_DIM, SSM_STATE)


def _conv_silu(xpad, cw_ref, cb_ref, lt):
    conv = cb_ref[...]
    for k in range(SSM_CONV):
        conv = conv + xpad[:, CONV_PAD - HIST + k:CONV_PAD - HIST + k + lt, :] * cw_ref[k:k + 1, :]
    return _silu(conv)


def _ssm_sample_kernel(x_ref, sh_ref, sc_ref, gt_ref, gpre_ref, gpost_ref, wz_ref, wxbc_ref, wdt_ref, cw_ref,
                       cb_ref, dtb_ref, alog_ref, dx_ref, nw_ref, wo_ref, e2_ref, conv_in_ref, h0_ref,
                       o_ref, conv_out_ref, h_out_ref, xpad, xbc_s, dt_s, da_s, y_s):
    x = x_ref[...]
    bt, lt, d = x.shape
    m = bt * lt
    hb = _mod_norm(x, gpre_ref[...], sc_ref[...], sh_ref[...]).reshape(m, d).astype(BF16)
    z, xbc_raw, dt, da = _ssm_project(hb, wz_ref, wxbc_ref, wdt_ref, dtb_ref, alog_ref)
    dt_s[...] = dt
    da_s[...] = da
    xpad[:, CONV_PAD:CONV_PAD + lt, :] = xbc_raw.reshape(bt, lt, SSM_CONV_DIM)
    xpad[:, CONV_PAD - HIST:CONV_PAD, :] = conv_in_ref[...]
    xbc_s[...] = _conv_silu(xpad, cw_ref, cb_ref, lt).reshape(m, SSM_CONV_DIM)
    conv_out_ref[...] = xpad[:, CONV_PAD + lt - HIST:CONV_PAD + lt, :]
    _ssd_chunk(0, m, lt, 0, xbc_s=xbc_s, dt_s=dt_s, da_s=da_s, y_s=y_s, e2_ref=e2_ref,
               h_in=h0_ref, h_out=h_out_ref, consts=_chunk_consts(m, lt))
    yn = _gate_norm(y_s[...], xbc_s[:, 0:SSM_D_INNER], z, dx_ref[...], nw_ref[...])
    out = _dot(yn, wo_ref[...]).reshape(bt, lt, d)
    o_ref[...] = x + gt_ref[...] * _rms(out, gpost_ref[...])


def _ssm_sample(x, mod, gpre, gpost, w, conv_in, h0, bt):
    b, lt, d = x.shape
    m = bt * lt
    xspec = pl.BlockSpec((bt, lt, d), lambda i: (i, 0, 0))
    conv_spec = pl.BlockSpec((bt, HIST, SSM_CONV_DIM), lambda i: (i, 0, 0))
    state_spec = pl.BlockSpec((bt,) + _STATE_SHAPE, lambda i: (i, 0, 0, 0))
    mods = [pl.BlockSpec((bt, None, 1, d), functools.partial(lambda k, i: (i, k, 0, 0), 3 + k)) for k in range(3)]
    weights = [w[n] for n in _SSM_WEIGHT_NAMES]
    return pl.pallas_call(
        _ssm_sample_kernel,
        out_shape=(jax.ShapeDtypeStruct(x.shape, F32),
                   jax.ShapeDtypeStruct((b, HIST, SSM_CONV_DIM), F32),
                   jax.ShapeDtypeStruct((b,) + _STATE_SHAPE, F32)),
        grid=(b // bt,),
        in_specs=[xspec] + mods + [_const_spec((1, 1, d)), _const_spec((1, 1, d))]
        + [_const_spec(a.shape, a.dtype == BF16) for a in weights] + [conv_spec, state_spec],
        out_specs=(xspec, conv_spec, state_spec),
        scratch_shapes=[
            pltpu.VMEM((bt, CONV_PAD + lt, SSM_CONV_DIM), F32),
            pltpu.VMEM((m, SSM_CONV_DIM), F32),
            pltpu.VMEM((m, LANES), F32),
            pltpu.VMEM((m, LANES), F32),
            pltpu.VMEM((m, SSM_D_INNER), F32),
        ],
        compiler_params=_params(1, SSM_SAMPLE_VMEM_LIMIT_BYTES),
        name="ssm_sublayer_state",
    )(x, mod, mod, mod, gpre, gpost, *weights, conv_in, h0)


def _ssm_prompt_kernel(xa_ref, xc_ref, sh_ref, sc_ref, gt_ref, gpre_ref, gpost_ref, wz_ref, wxbc_ref, wdt_ref,
                       cw_ref, cb_ref, dtb_ref, alog_ref, dx_ref, nw_ref, wo_ref, e2_ref,
                       o_ref, conv_out_ref, h_out_ref,
                       xpad, z_s, xbc_s, dt_s, da_s, zc_s, xbcc_s, dtc_s, dac_s, y_s, yn_s, ynp_s, h_s,
                       *, steps_per_seq, n_blocks):
    s = pl.program_id(0)
    q = SSM_CHUNK
    lt = 2 * q
    seq_start = (s % steps_per_seq) == 0
    scan_start = ((s + steps_per_seq - 1) % steps_per_seq) == 0
    n_slabs = SSM_CONV_DIM // LANES

    @pl.when(s == 0)
    def _():
        for ref in (z_s, xbc_s, dt_s, da_s, yn_s, h_s):
            ref[...] = jnp.zeros(ref.shape, ref.dtype)

    @pl.when(seq_start)
    def _():
        xpad[:, CONV_PAD - HIST:CONV_PAD, :] = jnp.zeros((n_slabs, HIST, LANES), F32)

    ynp_s[...] = yn_s[...]
    zc_s[...] = z_s[...]
    xbcc_s[...] = xbc_s[...]
    dtc_s[...] = dt_s[...]
    dac_s[...] = da_s[...]
    consts = _chunk_consts(q, q)

    def scan(t, reset):
        rows = slice(t * q, (t + 1) * q)
        xbc_v = xbcc_s.at[:, rows, :]
        _ssd_chunk(0, q, q, 0, xbc_s=xbc_v, dt_s=dtc_s.at[rows], da_s=dac_s.at[rows], y_s=y_s, e2_ref=e2_ref,
                   h_in=h_s, h_out=h_s, consts=consts, reset=reset)
        xs = _cols(xbc_v, slice(None), 0, SSM_D_INNER)
        yn_s[rows, :] = _gate_norm(y_s[...], xs, zc_s[rows, :], dx_ref[...], nw_ref[...])

    out = _dot(ynp_s[...], wo_ref[...])
    o_ref[0] = xc_ref[0] + gt_ref[0] * _rms(out, gpost_ref[0])

    hb = _mod_norm(xa_ref[0], gpre_ref[0], sc_ref[0], sh_ref[0]).astype(BF16)
    z, xbc_raw, dt, da = _ssm_project(hb, wz_ref, wxbc_ref, wdt_ref, dtb_ref, alog_ref)
    z_s[...] = z
    dt_s[...] = dt
    da_s[...] = da
    for c in range(n_slabs):
        xpad[c, CONV_PAD:CONV_PAD + lt, :] = xbc_raw[:, c * LANES:(c + 1) * LANES]

    scan(0, scan_start)

    half = lt // 2
    for c in range(n_slabs):
        csl = slice(c * LANES, (c + 1) * LANES)
        for par in range(2):
            acc = cb_ref[:, csl]
            for k in range(SSM_CONV):
                rows = pl.ds(CONV_PAD - HIST + par + k, half, stride=2)
                acc = acc + xpad[c, rows, :] * cw_ref[k:k + 1, csl]
            xbc_s[c, pl.ds(par, half, stride=2), :] = _silu(acc)
        new_hist = xpad[c, CONV_PAD + lt - HIST:CONV_PAD + lt, :]
        conv_out_ref[0, :, csl] = new_hist
        xpad[c, CONV_PAD - HIST:CONV_PAD, :] = new_hist

    scan(1, None)

    @pl.when(s <= n_blocks)
    def _():
        h_out_ref[...] = h_s[...]


def _ssm_prompt(x, mod, gpre, gpost, w):
    b, l, d = x.shape
    q = SSM_CHUNK
    lt = 2 * q
    assert l % lt == 0
    spq = l // lt
    n_blocks = b * spq

    def lagged(lag):
        def index(s):
            blk = jnp.clip(s - lag, 0, n_blocks - 1)
            return blk // spq, blk % spq
        return index

    cur, prev, prev2 = lagged(0), lagged(1), lagged(2)
    xa_spec = pl.BlockSpec((1, lt, d), lambda s: cur(s) + (0,))
    xc_spec = pl.BlockSpec((1, lt, d), lambda s: prev2(s) + (0,))
    mod_a = [pl.BlockSpec((1, None, 1, d), functools.partial(lambda k, s: (cur(s)[0], k, 0, 0), 3 + k))
             for k in range(2)]
    mod_c = pl.BlockSpec((1, None, 1, d), lambda s: (prev2(s)[0], 5, 0, 0))
    conv_spec = pl.BlockSpec((1, HIST, SSM_CONV_DIM), lambda s: (cur(s)[0], 0, 0))
    state_spec = pl.BlockSpec((1,) + _STATE_SHAPE, lambda s: (prev(s)[0], 0, 0, 0))
    weights = [w[n] for n in _SSM_WEIGHT_NAMES]
    return pl.pallas_call(
        functools.partial(_ssm_prompt_kernel, steps_per_seq=spq, n_blocks=n_blocks),
        out_shape=(jax.ShapeDtypeStruct(x.shape, F32),
                   jax.ShapeDtypeStruct((b, HIST, SSM_CONV_DIM), F32),
                   jax.ShapeDtypeStruct((b,) + _STATE_SHAPE, F32)),
        grid=(n_blocks + 2,),
        in_specs=[xa_spec, xc_spec] + mod_a + [mod_c, _const_spec((1, 1, d)), _const_spec((1, 1, d))]
        + [_const_spec(a.shape, a.dtype == BF16) for a in weights],
        out_specs=(xc_spec, conv_spec, state_spec),
        scratch_shapes=[
            pltpu.VMEM((SSM_CONV_DIM // LANES, CONV_PAD + lt, LANES), F32),
            pltpu.VMEM((lt, SSM_D_INNER), F32),
            pltpu.VMEM((SSM_CONV_DIM // LANES, lt, LANES), F32),
            pltpu.VMEM((lt, LANES), F32),
            pltpu.VMEM((lt, LANES), F32),
            pltpu.VMEM((lt, SSM_D_INNER), F32),
            pltpu.VMEM((SSM_CONV_DIM // LANES, lt, LANES), F32),
            pltpu.VMEM((lt, LANES), F32),
            pltpu.VMEM((lt, LANES), F32),
            pltpu.VMEM((q, SSM_D_INNER), F32),
            pltpu.VMEM((lt, SSM_D_INNER), BF16),
            pltpu.VMEM((lt, SSM_D_INNER), BF16),
            pltpu.VMEM((1,) + _STATE_SHAPE, F32),
        ],
        compiler_params=_params(1),
        name="ssm_sublayer",
    )(x, x, mod, mod, mod, gpre, gpost, *weights)


def _ssm_weights(in_w, conv_w, conv_b, dt_bias, a_log, d_skip, norm_w, out_w):
    pad = LANES - SSM_HEADS
    heads = np.arange(SSM_D_INNER) // SSM_HEAD_DIM
    e = (np.arange(LANES)[:, None] == heads[None, :]).astype(np.float32)
    return dict(
        wz=in_w[:, :SSM_D_INNER].astype(BF16),
        wxbc=in_w[:, SSM_D_INNER:SSM_D_INNER + SSM_CONV_DIM].astype(BF16),
        wdt=jnp.pad(in_w[:, SSM_D_INNER + SSM_CONV_DIM:], ((0, 0), (0, pad))).astype(BF16),
        conv_w=conv_w,
        conv_b=conv_b.reshape(1, SSM_CONV_DIM),
        dt_bias=jnp.pad(dt_bias, (0, pad)).reshape(1, LANES),
        a_log=jnp.pad(a_log, (0, pad)).reshape(1, LANES),
        d_x=jnp.repeat(d_skip, SSM_HEAD_DIM).reshape(1, SSM_D_INNER),
        norm_w=norm_w.reshape(1, SSM_D_INNER),
        wo=out_w.astype(BF16),
        e2=jnp.asarray(np.concatenate([e, e], axis=0), BF16),
    )


SOFTMAX_ROWS = 64


def _softmax_sink(logits, sink):
    mx = jnp.maximum(jnp.max(logits, axis=-1, keepdims=True), sink)
    e = jnp.exp(logits - mx)
    denom = jnp.sum(e, axis=-1, keepdims=True) + jnp.exp(sink - mx)
    return e * (1.0 / denom)


def _dedup(t):
    lane_lo = lax.broadcasted_iota(jnp.int32, (t.shape[0], LANES), 1) < HALF
    tiles = [jnp.where(lane_lo, t[:, (2 * i) * LANES:(2 * i + 1) * LANES],
                       t[:, (2 * i + 1) * LANES:(2 * i + 2) * LANES]) for i in range(ATTN_KV_HEADS // 2)]
    return jnp.concatenate(tiles, axis=1)


def _swa_prompt_kernel(sinks_ref, xa_ref, xc_ref, sh_ref, sc_ref, gt_ref, gpre_ref, gpost_ref, wq_ref, wk_ref,
                       wv_ref, bq_ref, bk_ref, bv_ref, bias_ref, wo_ref, bo_ref,
                       o_ref, kc_ref, vc_ref, qn_s, kn_s, vn_s, q_s, kbuf, vbuf, att_s, lg_s, p_s,
                       *, steps_per_seq):
    s = pl.program_id(0)
    tq = xa_ref.shape[1]
    n_tiles = 2 * ATTN_KV_HEADS
    pairs = ATTN_REP // 2

    @pl.when(s == 0)
    def _():
        for ref in (qn_s, kn_s, vn_s, kbuf, vbuf):
            ref[...] = jnp.zeros(ref.shape, ref.dtype)

    kbuf[0:WINDOW, :] = kbuf[tq:tq + WINDOW, :]
    vbuf[0:WINDOW, :] = vbuf[tq:tq + WINDOW, :]
    kbuf[WINDOW:WINDOW + tq, :] = kn_s[...]
    vbuf[WINDOW:WINDOW + tq, :] = vn_s[...]
    q_s[...] = qn_s[...]

    hb = _mod_norm(xa_ref[0], gpre_ref[0], sc_ref[0], sh_ref[0]).astype(BF16)
    half_q = ATTN_Q_DIM // 2

    def project_q(lo):
        qn_s[:, lo:lo + half_q] = ((_dot(hb, wq_ref[:, lo:lo + half_q]) + bq_ref[:, lo:lo + half_q])
                                   * (ATTN_HEAD_DIM ** -0.5)).astype(BF16)

    def project_kv(w_ref, b_ref, nxt, cache_ref):
        kv = _dot(hb, w_ref[...]) + b_ref[...]
        nxt[...] = kv.astype(BF16)
        cache_ref[0] = _dedup(kv[tq - WINDOW:, :])

    parts = [functools.partial(project_q, 0), functools.partial(project_q, half_q),
             functools.partial(project_kv, wk_ref, bk_ref, kn_s, kc_ref),
             functools.partial(project_kv, wv_ref, bv_ref, vn_s, vc_ref)]

    lane_lo = lax.broadcasted_iota(jnp.int32, (2 * WINDOW, LANES), 1) < HALF
    zero_b = jnp.zeros((), BF16)
    seq_first = jnp.where(((s + steps_per_seq - 1) % steps_per_seq) == 0, n_tiles, 0)
    n_q = tq // WINDOW
    for bi in range(n_q):
        r0 = bi * WINDOW
        qrows = slice(r0, r0 + WINDOW)
        first = seq_first if bi == 0 else 0
        lg, pb = lg_s.at[bi], p_s.at[bi]
        for g in range(ATTN_KV_HEADS):
            kd = kbuf[r0:r0 + 2 * WINDOW, g * LANES:(g + 1) * LANES]
            q2 = jnp.concatenate([q_s[qrows, (g * pairs + pr) * LANES:(g * pairs + pr + 1) * LANES]
                                  for pr in range(pairs)], axis=0)
            for half in range(2):
                kh = jnp.where(lane_lo, kd, zero_b) if half == 0 else jnp.where(lane_lo, zero_b, kd)
                t = 2 * g + half
                lg[t] = _dot_nt(q2, kh) + bias_ref[first + t]
        for part in parts[bi * len(parts) // n_q:(bi + 1) * len(parts) // n_q]:
            part()
        for t in range(n_tiles):
            for rc in range(2 * WINDOW // SOFTMAX_ROWS):
                rs = slice(rc * SOFTMAX_ROWS, (rc + 1) * SOFTMAX_ROWS)
                h = 4 * (t // 2) + (t % 2) + 2 * ((rc * SOFTMAX_ROWS) // WINDOW)
                pb[t, rs, :] = _softmax_sink(lg[t, rs, :], sinks_ref[h]).astype(BF16)
        for g in range(ATTN_KV_HEADS):
            vd = vbuf[r0:r0 + 2 * WINDOW, g * LANES:(g + 1) * LANES]
            acc = (_dot(pb[2 * g], jnp.where(lane_lo, vd, zero_b))
                   + _dot(pb[2 * g + 1], jnp.where(lane_lo, zero_b, vd)))
            for pr in range(pairs):
                pair = g * pairs + pr
                att_s[qrows, pair * LANES:(pair + 1) * LANES] = acc[pr * WINDOW:(pr + 1) * WINDOW].astype(BF16)
        out = _dot(att_s[qrows, :], wo_ref[...]) + bo_ref[...]
        o_ref[0, qrows, :] = xc_ref[0, qrows, :] + gt_ref[0] * _rms(out, gpost_ref[0])


def _swa_prompt(x, mod, gpre, gpost, w, bias, tq):
    b, l, d = x.shape
    kvw = ATTN_KV_HEADS * LANES
    pairs = ATTN_REP // 2
    n_tiles = 2 * ATTN_KV_HEADS
    assert l % tq == 0 and tq % WINDOW == 0
    spq = l // tq
    n_blocks = b * spq
    bias = bias.reshape(2, ATTN_KV_HEADS, pairs, 2, WINDOW, 2 * WINDOW).transpose(0, 1, 3, 2, 4, 5)
    bias = bias.reshape(2 * n_tiles, pairs * WINDOW, 2 * WINDOW)

    def cur(s):
        blk = jnp.minimum(s, n_blocks - 1)
        return blk // spq, blk % spq

    def prev(s):
        blk = jnp.maximum(s - 1, 0)
        return blk // spq, blk % spq

    xa_spec = pl.BlockSpec((1, tq, d), lambda s: cur(s) + (0,))
    xc_spec = pl.BlockSpec((1, tq, d), lambda s: prev(s) + (0,))
    mod_a = [pl.BlockSpec((1, None, 1, d), functools.partial(lambda k, s: (cur(s)[0], k, 0, 0), 3 + k))
             for k in range(2)]
    mod_c = pl.BlockSpec((1, None, 1, d), lambda s: (prev(s)[0], 5, 0, 0))
    cache_spec = pl.BlockSpec((1, WINDOW, ATTN_KV_DIM), lambda s: (cur(s)[0], 0, 0))
    consts = [w["wq"], w["wk_dup"], w["wv_dup"], w["bq"], w["bk_dup"], w["bv_dup"], bias, w["wo"], w["bo"]]
    n_q = tq // WINDOW
    return pl.pallas_call(
        functools.partial(_swa_prompt_kernel, steps_per_seq=spq),
        out_shape=(jax.ShapeDtypeStruct(x.shape, F32),
                   jax.ShapeDtypeStruct((b, WINDOW, ATTN_KV_DIM), F32),
                   jax.ShapeDtypeStruct((b, WINDOW, ATTN_KV_DIM), F32)),
        grid=(n_blocks + 1,),
        in_specs=[pl.BlockSpec(memory_space=pltpu.SMEM), xa_spec, xc_spec] + mod_a
        + [mod_c, _const_spec((1, 1, d)), _const_spec((1, 1, d))]
        + [_const_spec(a.shape, a.dtype == BF16) for a in consts],
        out_specs=(xc_spec, cache_spec, cache_spec),
        scratch_shapes=[
            pltpu.VMEM((tq, ATTN_Q_DIM), BF16),
            pltpu.VMEM((tq, kvw), BF16),
            pltpu.VMEM((tq, kvw), BF16),
            pltpu.VMEM((tq, ATTN_Q_DIM), BF16),
            pltpu.VMEM((WINDOW + tq, kvw), BF16),
            pltpu.VMEM((WINDOW + tq, kvw), BF16),
            pltpu.VMEM((tq, ATTN_Q_DIM), BF16),
            pltpu.VMEM((n_q, n_tiles, pairs * WINDOW, 2 * WINDOW), F32),
            pltpu.VMEM((n_q, n_tiles, pairs * WINDOW, 2 * WINDOW), BF16),
        ],
        compiler_params=_params(1),
        name="swa_prompt_sublayer",
    )(w["sinks"], x, x, mod, mod, mod, gpre, gpost, *consts)


def _swa_sample_kernel(x_ref, sh_ref, sc_ref, gt_ref, gpre_ref, gpost_ref, wq_ref, wk_ref, wv_ref,
                       bq_ref, bk_ref, bv_ref, ck_ref, cv_ref, bias_ref, sink_ref, wo_ref, bo_ref,
                       o_ref, ko_ref, vo_ref, q_s, kn_s, vn_s, att_s):
    x = x_ref[...]
    bt, lt, d = x.shape
    m = bt * lt
    nbuf = ck_ref.shape[1]
    wide = ATTN_KV_HEADS * ATTN_HEAD_DIM

    hb = _mod_norm(x, gpre_ref[...], sc_ref[...], sh_ref[...]).reshape(m, d).astype(BF16)
    q_s[...] = (_dot(hb, wq_ref[...]) + bq_ref[...]) * (ATTN_HEAD_DIM ** -0.5)
    kn_s[...] = _dot(hb, wk_ref[...]) + bk_ref[...]
    vn_s[...] = _dot(hb, wv_ref[...]) + bv_ref[...]
    zpad = jnp.zeros((bias_ref.shape[1] - nbuf - lt, ATTN_KV_DIM), F32)

    def seq(bi, carry):
        rows = pl.ds(pl.multiple_of(bi * lt, lt), lt)
        kn = kn_s[rows, :]
        vn = vn_s[rows, :]
        ck = ck_ref[bi]
        cv = cv_ref[bi]
        kfull = jnp.concatenate([ck, kn, zpad], axis=0).astype(BF16)
        vfull = jnp.concatenate([cv, vn, zpad], axis=0).astype(BF16)
        ko_ref[bi, 0:nbuf - lt, :] = ck[lt:, :]
        vo_ref[bi, 0:nbuf - lt, :] = cv[lt:, :]
        ko_ref[bi, nbuf - lt:nbuf, :] = kn
        vo_ref[bi, nbuf - lt:nbuf, :] = vn
        qbig = jnp.concatenate([q_s[rows, h * wide:(h + 1) * wide] for h in range(ATTN_HEADS)], axis=0)
        logits = _dot_nt(qbig.astype(BF16), kfull) + bias_ref[...]
        p = _softmax_sink(logits, sink_ref[:, 0:1]).astype(BF16)
        res = _dot(p, vfull)
        att_s[rows, :] = jnp.concatenate([res[h * lt:(h + 1) * lt, :] for h in range(ATTN_HEADS)], axis=1)
        return carry

    lax.fori_loop(0, bt, seq, 0, unroll=8)

    out = (_dot(att_s[...].astype(BF16), wo_ref[...]) + bo_ref[...]).reshape(bt, lt, d)
    o_ref[...] = x + gt_ref[...] * _rms(out, gpost_ref[...])


def _swa_sample(x, mod, gpre, gpost, w, bias, cache_k, cache_v, bt):
    b, lt, d = x.shape
    nbuf = cache_k.shape[1]
    assert nbuf == WINDOW and lt <= WINDOW and b % bt == 0
    m = bt * lt
    wide = ATTN_KV_HEADS * ATTN_HEAD_DIM
    big = ATTN_HEADS * wide
    keys = 2 * WINDOW
    xspec = pl.BlockSpec((bt, lt, d), lambda i: (i, 0, 0))
    cache_spec = pl.BlockSpec((bt, nbuf, ATTN_KV_DIM), lambda i: (i, 0, 0))
    bias_s = bias[0, :, :lt, :].reshape(ATTN_HEADS * lt, keys)
    sink_col = jnp.broadcast_to(jnp.repeat(w["sinks"], lt)[:, None], (ATTN_HEADS * lt, LANES))
    mods = [pl.BlockSpec((bt, None, 1, D_MODEL), functools.partial(lambda k, i: (i, k, 0, 0), 3 + k))
            for k in range(3)]
    pre = [w["wq_big"], w["wk"], w["wv"], w["bq_big"], w["bk"], w["bv"]]
    post = [bias_s, sink_col, w["wo_big"], w["bo"]]
    return pl.pallas_call(
        _swa_sample_kernel,
        out_shape=(jax.ShapeDtypeStruct(x.shape, F32),
                   jax.ShapeDtypeStruct(cache_k.shape, F32),
                   jax.ShapeDtypeStruct(cache_v.shape, F32)),
        grid=(b // bt,),
        in_specs=[xspec] + mods + [_const_spec((1, 1, d)), _const_spec((1, 1, d))]
        + [_const_spec(a.shape, a.dtype == BF16) for a in pre] + [cache_spec, cache_spec]
        + [_const_spec(a.shape, a.dtype == BF16) for a in post],
        out_specs=(xspec, cache_spec, cache_spec),
        scratch_shapes=[
            pltpu.VMEM((m, big), F32),
            pltpu.VMEM((m, ATTN_KV_DIM), F32),
            pltpu.VMEM((m, ATTN_KV_DIM), F32),
            pltpu.VMEM((m, big), F32),
        ],
        compiler_params=_params(1),
        name="swa_sample_sublayer",
    )(x, mod, mod, mod, gpre, gpost, *pre, cache_k, cache_v, *post)


def _swa_weights(qkv_w, qkv_b, sinks, o_w, o_b):
    d = qkv_w.shape[0]
    wq = qkv_w[:, :ATTN_Q_DIM]
    wk = qkv_w[:, ATTN_Q_DIM:ATTN_Q_DIM + ATTN_KV_DIM]
    wv = qkv_w[:, ATTN_Q_DIM + ATTN_KV_DIM:]
    bq = qkv_b[:ATTN_Q_DIM]
    bk = qkv_b[ATTN_Q_DIM:ATTN_Q_DIM + ATTN_KV_DIM]
    bv = qkv_b[ATTN_Q_DIM + ATTN_KV_DIM:]

    def dup(a):
        a4 = a.reshape(a.shape[:-1] + (ATTN_KV_HEADS, 1, ATTN_HEAD_DIM))
        a4 = jnp.broadcast_to(a4, a.shape[:-1] + (ATTN_KV_HEADS, 2, ATTN_HEAD_DIM))
        return a4.reshape(a.shape[:-1] + (ATTN_KV_HEADS * LANES,))

    own = jnp.asarray((np.arange(ATTN_HEADS)[:, None] // ATTN_REP == np.arange(ATTN_KV_HEADS)[None, :])
                      .astype(np.float32))
    wq_big = (wq.reshape(d, ATTN_HEADS, 1, ATTN_HEAD_DIM) * own[None, :, :, None]).reshape(d, -1)
    bq_big = (bq.reshape(ATTN_HEADS, 1, ATTN_HEAD_DIM) * own[:, :, None]).reshape(1, -1)
    wo_big = (o_w.reshape(ATTN_HEADS, 1, ATTN_HEAD_DIM, d) * own[:, :, None, None]).reshape(-1, d)
    return dict(
        sinks=sinks,
        wq=wq.astype(BF16), wk_dup=dup(wk).astype(BF16), wv_dup=dup(wv).astype(BF16),
        bq=bq.reshape(1, -1), bk_dup=dup(bk).reshape(1, -1), bv_dup=dup(bv).reshape(1, -1),
        wo=o_w.astype(BF16), bo=o_b.reshape(1, d),
        wq_big=wq_big.astype(BF16), bq_big=bq_big, wk=wk.astype(BF16), wv=wv.astype(BF16),
        bk=bk.reshape(1, -1), bv=bv.reshape(1, -1), wo_big=wo_big.astype(BF16),
    )


def _trunk(x, mod_all, state_ssm, state_conv, cache_k, cache_v, norm_pre, norm_post, ffn_in, ffn_out,
           ssm_w, swa_w, bias, tiles):
    bt, lt, n_sub = tiles["ffn"]
    b, l, d = x.shape
    sample = state_ssm is not None
    outs = {}
    for i in range(2):
        mod = mod_all[i].reshape(b, N_SUB * 3, 1, d)
        gpre = norm_pre[i].reshape(N_SUB, 1, 1, d)
        gpost = norm_post[i].reshape(N_SUB, 1, 1, d)
        x = _ffn(x, mod, 0, gpre[0], gpost[0], ffn_in, ffn_out, i, 0, bt, lt, n_sub)
        if i == 0:
            if sample:
                x, conv_new, ssm_new = _ssm_sample(x, mod, gpre[1], gpost[1], ssm_w, state_conv[0], state_ssm[0],
                                                   tiles["ssm"])
            else:
                x, conv_new, ssm_new = _ssm_prompt(x, mod, gpre[1], gpost[1], ssm_w)
            outs["conv"] = conv_new[None]
            outs["ssm"] = ssm_new[None]
        else:
            if sample:
                x, k_new, v_new = _swa_sample(x, mod, gpre[1], gpost[1], swa_w, bias,
                                              cache_k[0].reshape(b, -1, ATTN_KV_DIM),
                                              cache_v[0].reshape(b, -1, ATTN_KV_DIM), tiles["swa"])
            else:
                x, k_new, v_new = _swa_prompt(x, mod, gpre[1], gpost[1], swa_w, bias, tiles["swa"])
            shape = (1, b, -1, ATTN_KV_HEADS, ATTN_HEAD_DIM)
            outs["k"] = k_new.reshape(shape)
            outs["v"] = v_new.reshape(shape)
        x = _ffn(x, mod, 2, gpre[2], gpost[2], ffn_in, ffn_out, i, 1, bt, lt, n_sub)
    return x, outs["ssm"], outs["conv"], outs["k"], outs["v"]


def kernel(x_prompt, x_sample, state_ssm, state_conv, cache_k, cache_v, c_prompt, c_sample, ada_w, ada_b, norm_pre, norm_post, ffn_w_in, ffn_w_out, ssm_in_w, ssm_conv_w, ssm_conv_b, ssm_dt_bias, ssm_a_log, ssm_d, ssm_norm_w, ssm_out_w, attn_qkv_w, attn_qkv_b, attn_sinks, attn_o_w, attn_o_b, rel_bias):
    nb = x_prompt.shape[0]
    mod_all = _ada(jnp.concatenate([c_prompt, c_sample], axis=0), ada_w, ada_b)
    bias = _bias_table(rel_bias)
    ffn_in = ffn_w_in
    ffn_out = ffn_w_out
    ssm_w = _ssm_weights(ssm_in_w[0], ssm_conv_w[0], ssm_conv_b[0], ssm_dt_bias[0], ssm_a_log[0], ssm_d[0],
                         ssm_norm_w[0], ssm_out_w[0])
    swa_w = _swa_weights(attn_qkv_w[0], attn_qkv_b[0], attn_sinks[0], attn_o_w[0], attn_o_b[0])
    common = (norm_pre, norm_post, ffn_in, ffn_out, ssm_w, swa_w, bias)

    p_tiles = dict(ffn=(1, FFN_TILE_ROWS, FFN_SUB_TILES), swa=SWA_TILE_ROWS)
    y_p, ssm_p, conv_p, k_p, v_p = _trunk(x_prompt, mod_all[:, :nb], None, None, None, None, *common, p_tiles)
    ns, ls = x_sample.shape[:2]
    s_tiles = dict(ffn=(min(SAMPLE_FFN_SEQS, ns), ls, 1), ssm=min(SAMPLE_SSM_SEQS, ns),
                   swa=min(SAMPLE_SWA_SEQS, ns))
    y_s, ssm_s, conv_s, k_s, v_s = _trunk(x_sample, mod_all[:, nb:], state_ssm, state_conv, cache_k, cache_v,
                                          *common, s_tiles)
    return (y_p, y_s, ssm_p, conv_p, k_p, v_p, ssm_s, conv_s, k_s, v_s)
```
